```python
import jax, jax.numpy as jnp
from jax import lax
import numpy as np

D_MODEL = 2048
BATCH = 1
SEQ = 8192
DEPTH = 2
DEC_BATCH = 32
DEC_SEQ = 4
PAST_LEN = 8192
PAGE_SIZE = 128

N_A_LAYERS = DEPTH // 2
N_B_LAYERS = DEPTH - N_A_LAYERS
HGRN_EXPAND = 128
HGRN_HEADS = D_MODEL // HGRN_EXPAND
HGRN_K = HGRN_EXPAND
HGRN_V = D_MODEL // HGRN_HEADS
HGRN_DK = HGRN_HEADS * HGRN_K
HGRN_DV = HGRN_HEADS * HGRN_V
GLA_CHUNK = 64
HEAD_DIM = 128
N_HEADS = D_MODEL // HEAD_DIM
N_KV_HEADS = 4
GROUP = N_HEADS // N_KV_HEADS
MOBA_BLOCK = 256
MOBA_TOPK = 3
QUERY_TOKENS = 64
SCALE = HEAD_DIM ** -0.5
D_FF = ((8 * D_MODEL + 3 * 256 - 1) // (3 * 256)) * 256
EPS = 1e-6

kernel_name = 'yoco_hgrn2_moba_decode_step'


def _rmsnorm(x, g):
    xf = x.astype(jnp.float32)
    r = lax.rsqrt(jnp.mean(xf * xf, axis=-1, keepdims=True) + EPS)
    return (xf * r * g.astype(jnp.float32)).astype(x.dtype)


def _divisor_at_most(n, cap):
    c = max(1, min(n, cap))
    while n % c:
        c -= 1
    return c


def _query_chunk(lq, b):
    cap = max(1, QUERY_TOKENS // b)
    c = 1
    while c * 2 <= cap and lq % (c * 2) == 0:
        c *= 2
    return c


def _swiglu(x, g, w_gu, w_down):
    a, u = jnp.split(_rmsnorm(x, g) @ w_gu, 2, axis=-1)
    return (jax.nn.silu(a) * u) @ w_down


def _gla_chunked(q, k, v, log_f, s0):
    b_, l_, h_, _ = q.shape
    dv = v.shape[-1]
    c = _divisor_at_most(l_, GLA_CHUNK)
    n = l_ // c

    def to_chunks(a):
        return a.reshape(b_, n, c, h_, a.shape[-1]).transpose(1, 0, 3, 2, 4)

    qc, kc, vc, gc = to_chunks(q), to_chunks(k), to_chunks(v), to_chunks(log_f)
    cum = jnp.cumsum(gc, axis=3)
    mid = (c - 1) // 2
    ref = cum[:, :, :, mid:mid + 1]
    last = cum[:, :, :, c - 1:c]
    att = jnp.einsum('nbhtk,nbhsk->nbhts', qc * jnp.exp(cum - ref), kc * jnp.exp(ref - cum))
    att = jnp.where(jnp.tril(jnp.ones((c, c), dtype=bool)), att, 0.0)
    o_intra = jnp.einsum('nbhts,nbhsv->nbhtv', att, vc)
    q_in = qc * jnp.exp(cum)
    k_out = kc * jnp.exp(last - cum)
    decay = jnp.exp(last[:, :, :, 0])

    def step(s, xs):
        q_i, k_i, v_i, d_i = xs
        o_i = jnp.einsum('bhtk,bhkv->bhtv', q_i, s)
        s = d_i[..., None] * s + jnp.einsum('bhtk,bhtv->bhkv', k_i, v_i)
        return s, o_i

    s_fin, o_inter = lax.scan(step, s0, (q_in, k_out, vc, decay))
    o = (o_intra + o_inter).transpose(1, 0, 3, 2, 4).reshape(b_, l_, h_, dv)
    return o, s_fin


def _hgrn2_mixer(x, s0, norm_g, w_in, lb, onorm_g, w_out):
    b_, l_, _ = x.shape
    f32 = jnp.float32
    zq, zf, zi, zg = jnp.split(_rmsnorm(x, norm_g) @ w_in, 4, axis=-1)
    hk = (b_, l_, HGRN_HEADS, HGRN_K)
    hv = (b_, l_, HGRN_HEADS, HGRN_V)
    lb = lb.reshape(HGRN_HEADS, HGRN_K)
    f = lb + (1.0 - lb) * jax.nn.sigmoid(zf.astype(f32).reshape(hk))
    q = jax.nn.silu(zq.astype(f32).reshape(hk))
    o, s_fin = _gla_chunked(q, 1.0 - f, zi.astype(f32).reshape(hv), jnp.log(f), s0.astype(f32))
    o = _rmsnorm(o, onorm_g) * jax.nn.silu(zg.astype(f32).reshape(hv))
    return o.reshape(b_, l_, HGRN_DV).astype(x.dtype) @ w_out, s_fin.astype(s0.dtype)


def _moba(q, k, v, q_pos):
    b_, lq, _, _ = q.shape
    t_ = k.shape[1]
    f32 = jnp.float32
    nb = -(-t_ // MOBA_BLOCK)
    pad = nb * MOBA_BLOCK - t_
    k = jnp.pad(k, ((0, 0), (0, pad), (0, 0), (0, 0)))
    v = jnp.pad(v, ((0, 0), (0, pad), (0, 0), (0, 0)))
    kb = k.reshape(b_, nb, MOBA_BLOCK, N_KV_HEADS, HEAD_DIM)
    vb = v.reshape(b_, nb, MOBA_BLOCK, N_KV_HEADS, HEAD_DIM)
    kmean = jnp.mean(kb.astype(f32), axis=2)
    kb_t = kb.transpose(0, 3, 1, 2, 4)
    vb_t = vb.transpose(0, 3, 1, 2, 4)
    k_sel = min(MOBA_TOPK, nb)
    n_q = _query_chunk(lq, b_)
    n_chunks = lq // n_q
    qg = q.reshape(b_, n_chunks, n_q, N_KV_HEADS, GROUP, HEAD_DIM).transpose(1, 0, 2, 3, 4, 5)
    pos = q_pos.reshape(n_chunks, n_q)
    b_ix = jnp.arange(b_)[:, None, None, None, None]
    kv_ix = jnp.arange(N_KV_HEADS)[None, None, :, None, None]

    def attend(args):
        q_c, p_c = args
        qf = q_c.astype(f32)
        qblk = p_c // MOBA_BLOCK
        gate = jnp.einsum('bqkgd,bnkd->bqkgn', qf, kmean)
        full_past = jnp.arange(nb)[None, :] < qblk[:, None]
        gate = jnp.where(full_past[None, :, None, None, :], gate, -jnp.inf)
        _, top = lax.top_k(gate, k_sel)
        own = jnp.broadcast_to(qblk[None, :, None, None, None], top.shape[:-1] + (1,))
        blk = jnp.concatenate([top, own.astype(top.dtype)], axis=-1)
        rank_ok = jnp.concatenate([jnp.arange(k_sel)[None, :] < qblk[:, None],
                                   jnp.ones((n_q, 1), dtype=bool)], axis=-1)
        kg = kb_t[b_ix, kv_ix, blk].astype(f32)
        vg = vb_t[b_ix, kv_ix, blk].astype(f32)
        s = jnp.einsum('bqkgd,bqkgrpd->bqkgrp', qf, kg) * SCALE
        kpos = blk[..., None] * MOBA_BLOCK + jnp.arange(MOBA_BLOCK)
        mask = rank_ok[None, :, None, None, :, None] & (kpos <= p_c[None, :, None, None, None, None])
        s = jnp.where(mask, s, -jnp.inf)
        p = jax.nn.softmax(s.reshape(s.shape[:-2] + (-1,)), axis=-1).reshape(s.shape)
        return jnp.einsum('bqkgrp,bqkgrpd->bqkgd', p, vg)

    out = lax.map(attend, (qg, pos))
    return out.transpose(1, 0, 2, 3, 4, 5).reshape(b_, lq, N_HEADS, HEAD_DIM).astype(q.dtype)


def _trunk(x, s0, k_past, v_past, norm_mix_a, w_in_a, lb_all, onorm_a, w_out_a, norm_kv, w_kv, k_norm,
           norm_mix_b, w_q_b, q_norm, w_o_b, norm_ffn, w_gate_up, w_down):
    b_, l_, _ = x.shape
    pos0 = 0 if k_past is None else k_past.shape[1]
    q_pos = pos0 + jnp.arange(l_, dtype=jnp.int32)
    h = x
    states = []
    k_new = v_new = k_all = v_all = None
    for layer in range(DEPTH):
        if layer < N_A_LAYERS:
            o, s = _hgrn2_mixer(h, s0[layer], norm_mix_a[layer], w_in_a[layer], lb_all[layer],
                                onorm_a[layer], w_out_a[layer])
            h = h + o
            states.append(s)
        else:
            if layer == N_A_LAYERS:
                k_new, v_new = jnp.split(_rmsnorm(h, norm_kv) @ w_kv, 2, axis=-1)
                k_new = _rmsnorm(k_new.reshape(b_, l_, N_KV_HEADS, HEAD_DIM), k_norm)
                v_new = v_new.reshape(b_, l_, N_KV_HEADS, HEAD_DIM)
                if k_past is None:
                    k_all, v_all = k_new, v_new
                else:
                    k_all = jnp.concatenate([k_past.astype(k_new.dtype), k_new], axis=1)
                    v_all = jnp.concatenate([v_past.astype(v_new.dtype), v_new], axis=1)
            j = layer - N_A_LAYERS
            q = (_rmsnorm(h, norm_mix_b[j]) @ w_q_b[j]).reshape(b_, l_, N_HEADS, HEAD_DIM)
            q = _rmsnorm(q, q_norm[j])
            h = h + _moba(q, k_all, v_all, q_pos).reshape(b_, l_, N_HEADS * HEAD_DIM) @ w_o_b[j]
        h = h + _swiglu(h, norm_ffn[layer], w_gate_up[layer], w_down[layer])
    return h, jnp.stack(states), k_new, v_new


def setup_inputs(seed: int = 0) -> dict:
    key = jax.random.key(seed)
    ks = jax.random.split(key, 24)
    f32 = jnp.float32
    n_pages = PAST_LEN // PAGE_SIZE
    n_phys = (5 * DEC_BATCH * n_pages + 3) // 4

    def nrm(k, shape, scale):
        return jax.random.normal(k, shape, f32) * scale

    def gain(k, shape):
        return 1.0 + 0.02 * jax.random.normal(k, shape, f32)

    perm = jax.random.permutation(ks[5], n_phys)
    page_table = perm[:DEC_BATCH * n_pages].reshape(DEC_BATCH, n_pages).astype(jnp.int32)
    return {
        'x_prompt': nrm(ks[0], (BATCH, SEQ, D_MODEL), 1.0),
        'x_sample': nrm(ks[1], (DEC_BATCH, DEC_SEQ, D_MODEL), 1.0),
        'state_hgrn': nrm(ks[2], (N_A_LAYERS, DEC_BATCH, HGRN_HEADS, HGRN_K, HGRN_V), 0.5),
        'cache_k': nrm(ks[3], (n_phys, PAGE_SIZE, N_KV_HEADS, HEAD_DIM), 1.0),
        'cache_v': nrm(ks[4], (n_phys, PAGE_SIZE, N_KV_HEADS, HEAD_DIM), 1.0),
        'page_table': page_table,
        'norm_mix_a': gain(ks[6], (N_A_LAYERS, D_MODEL)),
        'w_in_a': nrm(ks[7], (N_A_LAYERS, D_MODEL, 2 * HGRN_DK + 2 * HGRN_DV), D_MODEL ** -0.5),
        'lb_logits': nrm(ks[8], (N_A_LAYERS + 1, HGRN_DK), 0.5),
        'onorm_a': gain(ks[9], (N_A_LAYERS, HGRN_V)),
        'w_out_a': nrm(ks[10], (N_A_LAYERS, HGRN_DV, D_MODEL), HGRN_DV ** -0.5),
        'norm_kv': gain(ks[11], (D_MODEL,)),
        'w_kv': nrm(ks[12], (D_MODEL, 2 * N_KV_HEADS * HEAD_DIM), D_MODEL ** -0.5),
        'k_norm': gain(ks[13], (HEAD_DIM,)),
        'norm_mix_b': gain(ks[14], (N_B_LAYERS, D_MODEL)),
        'w_q_b': nrm(ks[15], (N_B_LAYERS, D_MODEL, N_HEADS * HEAD_DIM), D_MODEL ** -0.5),
        'q_norm': gain(ks[16], (N_B_LAYERS, HEAD_DIM)),
        'w_o_b': nrm(ks[17], (N_B_LAYERS, N_HEADS * HEAD_DIM, D_MODEL), (N_HEADS * HEAD_DIM) ** -0.5),
        'norm_ffn': gain(ks[18], (DEPTH, D_MODEL)),
        'w_gate_up': nrm(ks[19], (DEPTH, D_MODEL, 2 * D_FF), D_MODEL ** -0.5),
        'w_down': nrm(ks[20], (DEPTH, D_FF, D_MODEL), D_FF ** -0.5),
    }


def reference(x_prompt, x_sample, state_hgrn, cache_k, cache_v, page_table, norm_mix_a, w_in_a, lb_logits,
              onorm_a, w_out_a, norm_kv, w_kv, k_norm, norm_mix_b, w_q_b, q_norm, w_o_b, norm_ffn,
              w_gate_up, w_down):
    lb_all = jnp.cumsum(jax.nn.softmax(lb_logits.astype(jnp.float32), axis=0), axis=0)[:N_A_LAYERS]
    weights = (norm_mix_a, w_in_a, lb_all, onorm_a, w_out_a, norm_kv, w_kv, k_norm,
               norm_mix_b, w_q_b, q_norm, w_o_b, norm_ffn, w_gate_up, w_down)
    s0_prompt = jnp.zeros((N_A_LAYERS, x_prompt.shape[0], HGRN_HEADS, HGRN_K, HGRN_V), state_hgrn.dtype)
    y_prompt, s_prompt, k_prompt, v_prompt = _trunk(x_prompt, s0_prompt, None, None, *weights)
    dec_b, n_pages = page_table.shape
    past = n_pages * cache_k.shape[1]
    k_past = cache_k[page_table].reshape(dec_b, past, N_KV_HEADS, HEAD_DIM)
    v_past = cache_v[page_table].reshape(dec_b, past, N_KV_HEADS, HEAD_DIM)
    y_sample, s_sample, k_sample, v_sample = _trunk(x_sample, state_hgrn, k_past, v_past, *weights)
    return (y_prompt, y_sample, s_prompt, s_sample, k_prompt, v_prompt, k_sample, v_sample)
```

```python
import functools

import jax
import jax.numpy as jnp
from jax import lax
from jax.experimental import pallas as pl
from jax.experimental.pallas import tpu as pltpu

F32 = jnp.float32
BF16 = jnp.bfloat16

EPS = 1e-6
HEAD = 128
GLA_CHUNK = 64
MOBA_BLOCK = 256
MOBA_TOPK = 3
SCALE = HEAD ** -0.5
NEG = -1e30
LANES = 128
NORM_ROWS = 16
ROW_TILE_CAP = 1040
VMEM_LIMIT = 56 * 1024 * 1024


def _dot(a, b):
    return jnp.dot(a, b, preferred_element_type=F32)


def _dot_nt(a, b, precision=None):
    return lax.dot_general(a, b, (((1,), (1,)), ((), ())), precision=precision,
                           preferred_element_type=F32)


def _dot_tn(a, b):
    return lax.dot_general(a, b, (((0,), (0,)), ((), ())), preferred_element_type=F32)


def _div_mod(x, n):
    if n & (n - 1) == 0:
        return jnp.right_shift(x, n.bit_length() - 1), x & (n - 1)
    q = x // n
    return q, x - q * n


def _sigmoid(x):
    return 1.0 / (1.0 + jnp.exp(-x))


def _rms(x, g):
    return x * lax.rsqrt(jnp.mean(x * x, axis=-1, keepdims=True) + EPS) * g


def _row_tile(m, cap=ROW_TILE_CAP):
    for t in range(min(m, cap), 0, -1):
        if m % t == 0 and t % NORM_ROWS == 0:
            return t
    raise ValueError(f"no row tile for {m} rows")


def _col_tile(n, cap):
    for t in range(min(n, cap), 0, -1):
        if n % t == 0 and t % LANES == 0:
            return t
    raise ValueError(f"no column tile for {n} columns")


def _params(*sem):
    return pltpu.CompilerParams(dimension_semantics=sem, vmem_limit_bytes=VMEM_LIMIT)


def _normalise_tile(x_ref, g_ref, xn_ref):
    g = g_ref[...]

    def body(i, carry):
        rows = pl.ds(pl.multiple_of(i * NORM_ROWS, NORM_ROWS), NORM_ROWS)
        xn_ref[rows, :] = _rms(x_ref[rows, :], g).astype(BF16)
        return carry

    lax.fori_loop(0, x_ref.shape[0] // NORM_ROWS, body, 0)


def _norm_matmul_kernel(x_ref, g_ref, w_ref, hg_ref, o_ref, xn_ref, *, norm_blocks):
    j = pl.program_id(1)

    @pl.when(j == 0)
    def _():
        _normalise_tile(x_ref, g_ref, xn_ref)

    acc = _dot(xn_ref[...], w_ref[...].astype(BF16))
    if norm_blocks == 0:
        o_ref[...] = acc
        return

    @pl.when(j < norm_blocks)
    def _():
        for h in range(acc.shape[1] // HEAD):
            cols = slice(h * HEAD, (h + 1) * HEAD)
            o_ref[:, cols] = _rms(acc[:, cols], hg_ref[...])

    @pl.when(j >= norm_blocks)
    def _():
        o_ref[...] = acc


def _norm_matmul(x, g, w, head_gain=None, norm_blocks=0, tn_cap=512, name="norm_matmul"):
    m, d = x.shape
    n = w.shape[1]
    tm, tn = _row_tile(m), _col_tile(n, tn_cap)
    if head_gain is None:
        head_gain = jnp.ones((HEAD,), F32)
    return pl.pallas_call(
        functools.partial(_norm_matmul_kernel, norm_blocks=norm_blocks),
        grid=(m // tm, n // tn),
        in_specs=[
            pl.BlockSpec((tm, d), lambda i, j: (i, 0)),
            pl.BlockSpec((1, d), lambda i, j: (0, 0)),
            pl.BlockSpec((d, tn), lambda i, j: (0, j)),
            pl.BlockSpec((1, HEAD), lambda i, j: (0, 0)),
        ],
        out_specs=pl.BlockSpec((tm, tn), lambda i, j: (i, j)),
        out_shape=jax.ShapeDtypeStruct((m, n), F32),
        scratch_shapes=[pltpu.VMEM((tm, d), BF16)],
        compiler_params=_params("parallel", "arbitrary"),
        name=name,
    )(x, g.reshape(1, d), w, head_gain.reshape(1, HEAD))


def _swiglu_up_kernel(x_ref, g_ref, wa_ref, wu_ref, o_ref, xn_ref):
    @pl.when(pl.program_id(1) == 0)
    def _():
        _normalise_tile(x_ref, g_ref, xn_ref)

    xn = xn_ref[...]
    a = _dot(xn, wa_ref[...].astype(BF16))
    u = _dot(xn, wu_ref[...].astype(BF16))
    o_ref[...] = (a * _sigmoid(a) * u).astype(BF16)


def _swiglu_up(x, g, w_gu, name):
    m, d = x.shape
    d_ff = w_gu.shape[1] // 2
    tm, tn = _row_tile(m), _col_tile(d_ff, 512)
    nblk = d_ff // tn
    return pl.pallas_call(
        _swiglu_up_kernel,
        grid=(m // tm, nblk),
        in_specs=[
            pl.BlockSpec((tm, d), lambda i, j: (i, 0)),
            pl.BlockSpec((1, d), lambda i, j: (0, 0)),
            pl.BlockSpec((d, tn), lambda i, j: (0, j)),
            pl.BlockSpec((d, tn), lambda i, j: (0, j + nblk)),
        ],
        out_specs=pl.BlockSpec((tm, tn), lambda i, j: (i, j)),
        out_shape=jax.ShapeDtypeStruct((m, d_ff), BF16),
        scratch_shapes=[pltpu.VMEM((tm, d), BF16)],
        compiler_params=_params("parallel", "arbitrary"),
        name=name,
    )(x, g.reshape(1, d), w_gu, w_gu)


def _matmul_residual_kernel(a_ref, w_ref, r_ref, o_ref):
    o_ref[...] = r_ref[...] + _dot(a_ref[...], w_ref[...].astype(BF16))


def _matmul_residual(a, w, res, tn_cap, name):
    m, k = a.shape
    n = w.shape[1]
    tm, tn = _row_tile(m), _col_tile(n, tn_cap)
    return pl.pallas_call(
        _matmul_residual_kernel,
        grid=(m // tm, n // tn),
        in_specs=[
            pl.BlockSpec((tm, k), lambda i, j: (i, 0)),
            pl.BlockSpec((k, tn), lambda i, j: (0, j)),
            pl.BlockSpec((tm, tn), lambda i, j: (i, j)),
        ],
        out_specs=pl.BlockSpec((tm, tn), lambda i, j: (i, j)),
        out_shape=jax.ShapeDtypeStruct((m, n), F32),
        compiler_params=_params("parallel", "arbitrary"),
        name=name,
    )(a, w, res)


def _lower_bound(logits, layer):
    e = jnp.exp(logits - jnp.max(logits, axis=0, keepdims=True))
    return jnp.sum(e[:layer + 1], axis=0, keepdims=True) / jnp.sum(e, axis=0, keepdims=True)


def _gate_inputs(zq, zf, lb):
    f = lb + (1.0 - lb) * _sigmoid(zf)
    return zq * _sigmoid(zq), 1.0 - f, jnp.log(f)


def _gated_output(o, zg, gain):
    return _rms(o, gain) * (zg * _sigmoid(zg))


def _cumsum_rows(g, seg=None):
    n = g.shape[0]
    row = lax.broadcasted_iota(jnp.int32, g.shape, 0)
    pos = row if seg is None else _div_mod(row, seg)[1]
    span = n if seg is None else seg
    s = 1
    while s < span:
        g = g + jnp.where(pos >= s, pltpu.roll(g, s, 0), 0.0)
        s *= 2
    return g


def _gla_prompt_kernel(zq_ref, zf_ref, zi_ref, zg_ref, lbl_ref, gain_ref, o_ref, s_ref, st_ref,
                       *, layer):
    t = pl.program_id(1)
    c = GLA_CHUNK
    mid = (c - 1) // 2

    @pl.when(t == 0)
    def _():
        st_ref[...] = jnp.zeros_like(st_ref)

    lb = _lower_bound(lbl_ref[...], layer)
    gain = gain_ref[...]
    r2 = lax.broadcasted_iota(jnp.int32, (c, c), 0)
    c2 = lax.broadcasted_iota(jnp.int32, (c, c), 1)

    def body(i, carry):
        rows = pl.ds(pl.multiple_of(i * c, c), c)
        q, k, g = _gate_inputs(zq_ref[rows, :], zf_ref[rows, :], lb)
        v16 = zi_ref[rows, :].astype(BF16)
        cum = _cumsum_rows(g)
        ref = cum[mid:mid + 1, :]
        last = cum[c - 1:c, :]
        att = _dot_nt((q * jnp.exp(cum - ref)).astype(BF16), (k * jnp.exp(ref - cum)).astype(BF16))
        att = jnp.where(r2 >= c2, att, 0.0)
        st = st_ref[...]
        o = _dot(att.astype(BF16), v16) + _dot_nt((q * jnp.exp(cum)).astype(BF16), st.astype(BF16))
        k_out = (k * jnp.exp(last - cum)).astype(BF16)
        st_ref[...] = jnp.exp(last) * st + _dot_tn(v16, k_out)
        o_ref[rows, :] = _gated_output(o, zg_ref[rows, :], gain).astype(BF16)
        return carry

    lax.fori_loop(0, o_ref.shape[0] // c, body, 0)

    @pl.when(t == pl.num_programs(1) - 1)
    def _():
        s_ref[...] = st_ref[...].T


def _gla_prompt(z, lb_logits, gain, layer, seq, tl=512):
    heads = z.shape[1] // (4 * HEAD)
    tl = min(tl, seq)
    slots = lb_logits.shape[0]
    zspec = lambda off: pl.BlockSpec((tl, HEAD), lambda h, t: (t, h + off * heads))
    return pl.pallas_call(
        functools.partial(_gla_prompt_kernel, layer=layer),
        grid=(heads, seq // tl),
        in_specs=[zspec(0), zspec(1), zspec(2), zspec(3),
                  pl.BlockSpec((slots, HEAD), lambda h, t: (0, h)),
                  pl.BlockSpec((1, HEAD), lambda h, t: (0, 0))],
        out_specs=[pl.BlockSpec((tl, HEAD), lambda h, t: (t, h)),
                   pl.BlockSpec((None, HEAD, HEAD), lambda h, t: (h, 0, 0))],
        out_shape=[jax.ShapeDtypeStruct((seq, heads * HEAD), BF16),
                   jax.ShapeDtypeStruct((heads, HEAD, HEAD), F32)],
        scratch_shapes=[pltpu.VMEM((HEAD, HEAD), F32)],
        compiler_params=_params("parallel", "arbitrary"),
        name="gla_prompt",
    )(z, z, z, z, lb_logits, gain.reshape(1, HEAD))


def _gla_sample_kernel(zq_ref, zf_ref, zi_ref, zg_ref, lbl_ref, gain_ref, s0_ref, o_ref, s_ref,
                       last_ref, kout_ref, qin_ref, v_ref, *, layer, steps):
    rows = zq_ref.shape[0]
    mid = (steps - 1) // 2
    lb = _lower_bound(lbl_ref[...], layer)
    q, k, g = _gate_inputs(zq_ref[...], zf_ref[...], lb)
    v16 = zi_ref[...].astype(BF16)
    cum = _cumsum_rows(g, seg=steps)

    row = lax.broadcasted_iota(jnp.int32, cum.shape, 0)
    pos = _div_mod(row, steps)[1]

    def spread(src_pos):
        picked = jnp.where(pos == src_pos, cum, 0.0)
        out = picked
        for d in range(steps):
            if d != src_pos:
                out = out + pltpu.roll(picked, (d - src_pos) % rows, 0)
        return out

    ref = spread(mid)
    last = spread(steps - 1)
    r2 = lax.broadcasted_iota(jnp.int32, (rows, rows), 0)
    c2 = lax.broadcasted_iota(jnp.int32, (rows, rows), 1)
    att = _dot_nt((q * jnp.exp(cum - ref)).astype(BF16), (k * jnp.exp(ref - cum)).astype(BF16))
    att = jnp.where((r2 >= c2) & (_div_mod(r2, steps)[0] == _div_mod(c2, steps)[0]), att, 0.0)
    o_ref[...] = _dot(att.astype(BF16), v16)
    qin_ref[...] = (q * jnp.exp(cum)).astype(BF16)
    v_ref[...] = v16
    last_ref[...] = last
    kout_ref[...] = k * jnp.exp(last - cum)

    def body(b, carry):
        lo = b * steps
        mine = (row >= lo) & (row < lo + steps)
        st = s0_ref[b].T
        o_ref[...] += jnp.where(mine, _dot_nt(qin_ref[...], st.astype(BF16)), 0.0)
        k_out = jnp.where(mine, kout_ref[...], 0.0).astype(BF16)
        decay = jnp.exp(last_ref[pl.ds(lo, 1), :])
        s_ref[b] = (decay * st + _dot_tn(v_ref[...], k_out)).T
        return carry

    lax.fori_loop(0, s0_ref.shape[0], body, 0)
    o_ref[...] = _gated_output(o_ref[...], zg_ref[...], gain_ref[...])


def _gla_sample(z, row0, n_seq, steps, s0, lb_logits, gain, layer):
    heads = z.shape[1] // (4 * HEAD)
    rows = n_seq * steps
    slots = lb_logits.shape[0]
    rb = row0 // rows
    zspec = lambda off: pl.BlockSpec((rows, HEAD), lambda h: (rb, h + off * heads))
    sspec = pl.BlockSpec((n_seq, None, HEAD, HEAD), lambda h: (0, h, 0, 0))
    return pl.pallas_call(
        functools.partial(_gla_sample_kernel, layer=layer, steps=steps),
        grid=(heads,),
        in_specs=[zspec(0), zspec(1), zspec(2), zspec(3),
                  pl.BlockSpec((slots, HEAD), lambda h: (0, h)),
                  pl.BlockSpec((1, HEAD), lambda h: (0, 0)),
                  sspec],
        out_specs=[pl.BlockSpec((rows, HEAD), lambda h: (0, h)), sspec],
        out_shape=[jax.ShapeDtypeStruct((rows, heads * HEAD), F32),
                   jax.ShapeDtypeStruct(s0.shape, s0.dtype)],
        scratch_shapes=[pltpu.VMEM((rows, HEAD), F32), pltpu.VMEM((rows, HEAD), F32),
                        pltpu.VMEM((rows, HEAD), BF16), pltpu.VMEM((rows, HEAD), BF16)],
        compiler_params=_params("parallel"),
        name="gla_sample",
    )(z, z, z, z, lb_logits, gain.reshape(1, HEAD), s0)


def _top_blocks_bits(gate, lane, n_valid):
    gate = jnp.where(lane < n_valid, gate, -jnp.inf)
    bits = jnp.zeros(gate.shape, jnp.int32)
    for _ in range(MOBA_TOPK):
        mx = jnp.max(gate, axis=1, keepdims=True)
        cand = jnp.where((gate == mx) & (mx > -jnp.inf), lane, LANES)
        idx = jnp.min(cand, axis=1, keepdims=True)
        pick = lane == idx
        gate = jnp.where(pick, -jnp.inf, gate)
        bits = bits | jnp.where(idx < LANES, jnp.left_shift(1, jnp.minimum(idx, 31)), 0)
    return bits


def _moba_prompt_kernel(q_ref, k_ref, v_ref, o_ref, kmean_ref, qb_ref, sel_ref, m_ref, l_ref,
                        acc_ref, *, n_blocks, group):
    qi = pl.program_id(1)
    blk = MOBA_BLOCK

    @pl.when(qi == 0)
    def _():
        kmean_ref[...] = jnp.zeros_like(kmean_ref)
        for n in range(n_blocks):
            kmean_ref[n:n + 1, :] = jnp.mean(k_ref[n * blk:(n + 1) * blk, :], axis=0, keepdims=True)

    lane = lax.broadcasted_iota(jnp.int32, (blk, LANES), 1)
    r2 = lax.broadcasted_iota(jnp.int32, (blk, blk), 0)
    c2 = lax.broadcasted_iota(jnp.int32, (blk, blk), 1)
    own = pl.ds(pl.multiple_of(qi * blk, blk), blk)
    kd = k_ref[own, :].astype(BF16)
    vd = v_ref[own, :].astype(BF16)
    for g in range(group):
        qg = q_ref[:, g * HEAD:(g + 1) * HEAD]
        qb_ref[g] = qg.astype(BF16)
        gate = _dot_nt(qg, kmean_ref[...], precision=lax.Precision.HIGHEST)
        sel_ref[g] = _top_blocks_bits(gate, lane, qi)
        s = jnp.where(c2 <= r2, _dot_nt(qb_ref[g], kd) * SCALE, NEG)
        m = jnp.max(s, axis=1, keepdims=True)
        p = jnp.exp(s - m)
        m_ref[g] = jnp.broadcast_to(m, (blk, LANES))
        l_ref[g] = jnp.broadcast_to(jnp.sum(p, axis=1, keepdims=True), (blk, LANES))
        acc_ref[g] = _dot(p.astype(BF16), vd)

    def body(j, carry):
        past = pl.ds(pl.multiple_of(j * blk, blk), blk)
        kj = k_ref[past, :].astype(BF16)
        vj = v_ref[past, :].astype(BF16)
        bit = jnp.left_shift(jnp.int32(1), j)
        for g in range(group):
            on = (sel_ref[g] & bit) != 0
            s = _dot_nt(qb_ref[g], kj) * SCALE
            s0 = jnp.where(on, s[:, :LANES], NEG)
            s1 = jnp.where(on, s[:, LANES:], NEG)
            m_prev = m_ref[g]
            m_new = jnp.maximum(m_prev, jnp.max(jnp.maximum(s0, s1), axis=1, keepdims=True))
            p0 = jnp.exp(s0 - m_new)
            p1 = jnp.exp(s1 - m_new)
            alpha = jnp.exp(m_prev - m_new)
            l_ref[g] = alpha * l_ref[g] + jnp.sum(p0 + p1, axis=1, keepdims=True)
            p = jnp.concatenate([p0, p1], axis=1).astype(BF16)
            acc_ref[g] = alpha * acc_ref[g] + _dot(p, vj)
            m_ref[g] = m_new
        return carry

    lax.fori_loop(0, qi, body, 0)
    for g in range(group):
        o_ref[:, g * HEAD:(g + 1) * HEAD] = (acc_ref[g] / l_ref[g]).astype(BF16)


def _moba_prompt(q, kv, seq, n_kv):
    heads = q.shape[1] // HEAD
    group = heads // n_kv
    n_blocks = seq // MOBA_BLOCK
    blk = MOBA_BLOCK
    return pl.pallas_call(
        functools.partial(_moba_prompt_kernel, n_blocks=n_blocks, group=group),
        grid=(n_kv, n_blocks),
        in_specs=[pl.BlockSpec((blk, group * HEAD), lambda h, i: (i, h)),
                  pl.BlockSpec((seq, HEAD), lambda h, i: (0, h)),
                  pl.BlockSpec((seq, HEAD), lambda h, i: (0, n_kv + h))],
        out_specs=pl.BlockSpec((blk, group * HEAD), lambda h, i: (i, h)),
        out_shape=jax.ShapeDtypeStruct((seq, heads * HEAD), BF16),
        scratch_shapes=[pltpu.VMEM((LANES, HEAD), F32),
                        pltpu.VMEM((group, blk, HEAD), BF16),
                        pltpu.VMEM((group, blk, LANES), jnp.int32),
                        pltpu.VMEM((group, blk, LANES), F32),
                        pltpu.VMEM((group, blk, LANES), F32),
                        pltpu.VMEM((group, blk, HEAD), F32)],
        compiler_params=_params("arbitrary", "arbitrary"),
        name="moba_prompt",
    )(q, kv, kv)


def _moba_sample_kernel(pt_ref, q_ref, kn_ref, vn_ref, *refs, n_kv, group, steps, pages_per_block):
    del pt_ref
    kp = refs[:pages_per_block]
    vp = refs[pages_per_block:2 * pages_per_block]
    o_ref, m_s, l_s, g_s, o_s = refs[2 * pages_per_block:]
    j = pl.program_id(1)
    n_blocks = pl.num_programs(1)
    rq = group * steps

    for h in range(n_kv):
        cols = slice(h * HEAD, (h + 1) * HEAD)
        rows = slice(h * rq, (h + 1) * rq)
        kb = jnp.concatenate([r[0, :, cols] for r in kp], axis=0)
        vb = jnp.concatenate([r[0, :, cols] for r in vp], axis=0)
        qf = q_ref[0, rows, :]
        s = _dot_nt(qf.astype(BF16), kb.astype(BF16)) * SCALE
        m = jnp.max(s, axis=1, keepdims=True)
        p = jnp.exp(s - m)
        gate = jnp.sum(qf * jnp.mean(kb, axis=0, keepdims=True), axis=1, keepdims=True)
        m_s[j, rows, :] = jnp.broadcast_to(m, (rq, LANES))
        l_s[j, rows, :] = jnp.broadcast_to(jnp.sum(p, axis=1, keepdims=True), (rq, LANES))
        g_s[j, rows, :] = jnp.broadcast_to(gate, (rq, LANES))
        o_s[j, rows, :] = _dot(p.astype(BF16), vb.astype(BF16))

    @pl.when(j == n_blocks - 1)
    def _():
        gates = g_s[...]
        n_io = lax.broadcasted_iota(jnp.int32, gates.shape, 0)
        sel = jnp.zeros(gates.shape, jnp.bool_)
        for _ in range(min(MOBA_TOPK, g_s.shape[0])):
            mx = jnp.max(gates, axis=0, keepdims=True)
            idx = jnp.min(jnp.where(gates == mx, n_io, g_s.shape[0]), axis=0, keepdims=True)
            pick = n_io == idx
            sel = sel | pick
            gates = jnp.where(pick, -jnp.inf, gates)

        row = lax.broadcasted_iota(jnp.int32, (rq, LANES), 0)
        lane = lax.broadcasted_iota(jnp.int32, (rq, LANES), 1)
        step = row
        for gg in range(1, group):
            step = step - jnp.where(row >= gg * steps, steps, 0)
        m_new, l_new, o_new = [], [], []
        for h in range(n_kv):
            cols = slice(h * HEAD, (h + 1) * HEAD)
            qf = q_ref[0, h * rq:(h + 1) * rq, :]
            s = _dot_nt(qf.astype(BF16), kn_ref[0, :, cols].astype(BF16)) * SCALE
            s = jnp.where(lane <= step, s, NEG)
            m = jnp.max(s, axis=1, keepdims=True)
            p = jnp.exp(s - m)
            m_new.append(jnp.broadcast_to(m, (rq, LANES)))
            l_new.append(jnp.broadcast_to(jnp.sum(p, axis=1, keepdims=True), (rq, LANES)))
            o_new.append(_dot(p.astype(BF16), vn_ref[0, :, cols].astype(BF16)))
        m_own = jnp.concatenate(m_new, axis=0)
        l_own = jnp.concatenate(l_new, axis=0)
        o_own = jnp.concatenate(o_new, axis=0)

        m_all = m_s[...]
        m_top = jnp.maximum(m_own, jnp.max(jnp.where(sel, m_all, -jnp.inf), axis=0))
        w = jnp.where(sel, jnp.exp(jnp.minimum(m_all - m_top[None], 0.0)), 0.0)
        w_own = jnp.exp(m_own - m_top)
        den = w_own * l_own + jnp.sum(w * l_s[...], axis=0)
        num = w_own * o_own + jnp.sum(w * o_s[...], axis=0)
        o_ref[0] = num / den


def _moba_sample(q, k_new, v_new, cache_k, cache_v, page_table, n_kv, steps):
    n_seq, n_pages = page_table.shape
    n_phys, page, _, _ = cache_k.shape
    ppb = MOBA_BLOCK // page
    n_blocks = n_pages // ppb
    rows = q.shape[1]
    group = rows // (n_kv * steps)
    width = n_kv * HEAD
    ck = cache_k.reshape(n_phys, page, width)
    cv = cache_v.reshape(n_phys, page, width)

    def page_spec(p):
        return pl.BlockSpec((1, page, width), lambda b, j, pt: (pt[b, j * ppb + p], 0, 0))

    per_seq = lambda shape: pl.BlockSpec(shape, lambda b, j, pt: (b, 0, 0))
    grid_spec = pltpu.PrefetchScalarGridSpec(
        num_scalar_prefetch=1,
        grid=(n_seq, n_blocks),
        in_specs=[per_seq((1, rows, HEAD)), per_seq((1, LANES, width)), per_seq((1, LANES, width))]
                 + [page_spec(p) for p in range(ppb)] + [page_spec(p) for p in range(ppb)],
        out_specs=per_seq((1, rows, HEAD)),
        scratch_shapes=[pltpu.VMEM((n_blocks, rows, LANES), F32) for _ in range(4)],
    )
    return pl.pallas_call(
        functools.partial(_moba_sample_kernel, n_kv=n_kv, group=group, steps=steps,
                          pages_per_block=ppb),
        grid_spec=grid_spec,
        out_shape=jax.ShapeDtypeStruct((n_seq, rows, HEAD), F32),
        compiler_params=_params("arbitrary", "arbitrary"),
        name="moba_sample",
    )(page_table, q, k_new, v_new, *([ck] * ppb), *([cv] * ppb))


def kernel(x_prompt, x_sample, state_hgrn, cache_k, cache_v, page_table, norm_mix_a, w_in_a, lb_logits,
           onorm_a, w_out_a, norm_kv, w_kv, k_norm, norm_mix_b, w_q_b, q_norm, w_o_b, norm_ffn,
           w_gate_up, w_down):
    batch, seq, d = x_prompt.shape
    n_seq, steps, _ = x_sample.shape
    n_a = w_in_a.shape[0]
    depth = norm_ffn.shape[0]
    heads = d // HEAD
    n_kv = cache_k.shape[2]
    group = heads // n_kv
    kv_width = n_kv * HEAD
    n_dec = n_seq * steps
    past = page_table.shape[1] * cache_k.shape[1]
    assert batch == 1 and seq % MOBA_BLOCK == 0 and seq // MOBA_BLOCK <= 32
    assert past % MOBA_BLOCK == 0 and MOBA_BLOCK % cache_k.shape[1] == 0
    assert steps <= LANES and seq % n_dec == 0 and seq % GLA_CHUNK == 0

    h = jnp.concatenate([x_prompt.reshape(seq, d), x_sample.reshape(n_dec, d)], axis=0)
    states_p, states_s = [], []
    kv = None
    for layer in range(depth):
        if layer < n_a:
            z = _norm_matmul(h, norm_mix_a[layer], w_in_a[layer], name="hgrn_in")
            o_p, s_p = _gla_prompt(z, lb_logits, onorm_a[layer], layer, seq)
            o_s, s_s = _gla_sample(z, seq, n_seq, steps, state_hgrn[layer], lb_logits,
                                   onorm_a[layer], layer)
            o = jnp.concatenate([o_p, o_s.astype(BF16)], axis=0)
            h = _matmul_residual(o, w_out_a[layer], h, 512, "hgrn_out")
            states_p.append(s_p.reshape(1, heads, HEAD, HEAD).astype(state_hgrn.dtype))
            states_s.append(s_s)
        else:
            if kv is None:
                kv = _norm_matmul(h, norm_kv, w_kv, head_gain=k_norm, norm_blocks=1,
                                  tn_cap=kv_width, name="kv_proj")
            jb = layer - n_a
            q = _norm_matmul(h, norm_mix_b[jb], w_q_b[jb], head_gain=q_norm[jb],
                             norm_blocks=heads, name="q_proj")
            att_p = _moba_prompt(q, kv, seq, n_kv)
            q_s = q[seq:].reshape(n_seq, steps, n_kv, group, HEAD).transpose(0, 2, 3, 1, 4)
            pad = ((0, 0), (0, LANES - steps), (0, 0))
            k_s = jnp.pad(kv[seq:, :kv_width].reshape(n_seq, steps, kv_width), pad)
            v_s = jnp.pad(kv[seq:, kv_width:].reshape(n_seq, steps, kv_width), pad)
            att_s = _moba_sample(q_s.reshape(n_seq, n_kv * group * steps, HEAD), k_s, v_s,
                                 cache_k, cache_v, page_table, n_kv, steps)
            att_s = att_s.reshape(n_seq, n_kv, group, steps, HEAD).transpose(0, 3, 1, 2, 4)
            att = jnp.concatenate([att_p, att_s.reshape(n_dec, d).astype(BF16)], axis=0)
            h = _matmul_residual(att, w_o_b[jb], h, 512, "attn_out")
        hf = _swiglu_up(h, norm_ffn[layer], w_gate_up[layer], "ffn_up")
        h = _matmul_residual(hf, w_down[layer], h, 256, "ffn_down")

    k_new = kv[:, :kv_width]
    v_new = kv[:, kv_width:]
    return (h[:seq].reshape(batch, seq, d),
            h[seq:].reshape(n_seq, steps, d),
            jnp.stack(states_p),
            jnp.stack(states_s),
            k_new[:seq].reshape(batch, seq, n_kv, HEAD),
            v_new[:seq].reshape(batch, seq, n_kv, HEAD),
            k_new[seq:].reshape(n_seq, steps, n_kv, HEAD),
            v_new[seq:].reshape(n_seq, steps, n_kv, HEAD))
```

```python
import functools

import jax
import jax.numpy as jnp
from jax import lax
from jax.experimental import pallas as pl
from jax.experimental.pallas import tpu as pltpu

F32 = jnp.float32
BF16 = jnp.bfloat16

EPS = 1e-6
HEAD = 128
GLA_CHUNK = 64
MOBA_BLOCK = 256
MOBA_TOPK = 3
SCALE = HEAD ** -0.5
SCALE_LOG2E = SCALE * 1.4426950408889634
PROMPT_ROW_CHUNK = 128
SCORE_LOOKAHEAD = 3
NEG = -1e30
LANES = 128
SUBLANES = 8
NORM_ROWS = 16
ROW_TILE_CAP = 1040
VMEM_LIMIT = 56 * 1024 * 1024


def _dot(a, b):
    return jnp.dot(a, b, preferred_element_type=F32)


def _dot_nt(a, b, precision=None):
    return lax.dot_general(a, b, (((1,), (1,)), ((), ())), precision=precision,
                           preferred_element_type=F32)


def _dot_tn(a, b):
    return lax.dot_general(a, b, (((0,), (0,)), ((), ())), preferred_element_type=F32)


def _div_mod(x, n):
    if n & (n - 1) == 0:
        return jnp.right_shift(x, n.bit_length() - 1), x & (n - 1)
    q = x // n
    return q, x - q * n


def _sigmoid(x):
    return 1.0 / (1.0 + jnp.exp(-x))


def _rms(x, g):
    return x * lax.rsqrt(jnp.mean(x * x, axis=-1, keepdims=True) + EPS) * g


def _row_tile(m, cap=ROW_TILE_CAP):
    for t in range(min(m, cap), 0, -1):
        if m % t == 0 and t % NORM_ROWS == 0:
            return t
    raise ValueError(f"no row tile for {m} rows")


def _col_tile(n, cap):
    for t in range(min(n, cap), 0, -1):
        if n % t == 0 and t % LANES == 0:
            return t
    raise ValueError(f"no column tile for {n} columns")


def _params(*sem):
    return pltpu.CompilerParams(dimension_semantics=sem, vmem_limit_bytes=VMEM_LIMIT)


def _normalise_tile(x_ref, g_ref, xn_ref):
    g = g_ref[...]

    def body(i, carry):
        rows = pl.ds(pl.multiple_of(i * NORM_ROWS, NORM_ROWS), NORM_ROWS)
        xn_ref[rows, :] = _rms(x_ref[rows, :], g).astype(BF16)
        return carry

    lax.fori_loop(0, x_ref.shape[0] // NORM_ROWS, body, 0)


def _norm_matmul_kernel(x_ref, g_ref, w_ref, hg_ref, o_ref, xn_ref, *, norm_blocks):
    j = pl.program_id(1)

    @pl.when(j == 0)
    def _():
        _normalise_tile(x_ref, g_ref, xn_ref)

    acc = _dot(xn_ref[...], w_ref[...].astype(BF16))
    if norm_blocks == 0:
        o_ref[...] = acc
        return

    @pl.when(j < norm_blocks)
    def _():
        for h in range(acc.shape[1] // HEAD):
            cols = slice(h * HEAD, (h + 1) * HEAD)
            o_ref[:, cols] = _rms(acc[:, cols], hg_ref[...])

    @pl.when(j >= norm_blocks)
    def _():
        o_ref[...] = acc


def _norm_matmul(x, g, w, head_gain=None, norm_blocks=0, tn_cap=512, name="norm_matmul"):
    m, d = x.shape
    n = w.shape[1]
    tm, tn = _row_tile(m), _col_tile(n, tn_cap)
    if head_gain is None:
        head_gain = jnp.ones((HEAD,), F32)
    return pl.pallas_call(
        functools.partial(_norm_matmul_kernel, norm_blocks=norm_blocks),
        grid=(m // tm, n // tn),
        in_specs=[
            pl.BlockSpec((tm, d), lambda i, j: (i, 0)),
            pl.BlockSpec((1, d), lambda i, j: (0, 0)),
            pl.BlockSpec((d, tn), lambda i, j: (0, j)),
            pl.BlockSpec((1, HEAD), lambda i, j: (0, 0)),
        ],
        out_specs=pl.BlockSpec((tm, tn), lambda i, j: (i, j)),
        out_shape=jax.ShapeDtypeStruct((m, n), F32),
        scratch_shapes=[pltpu.VMEM((tm, d), BF16)],
        compiler_params=_params("parallel", "arbitrary"),
        name=name,
    )(x, g.reshape(1, d), w, head_gain.reshape(1, HEAD))


def _swiglu_up_kernel(x_ref, g_ref, wa_ref, wu_ref, o_ref, xn_ref):
    @pl.when(pl.program_id(1) == 0)
    def _():
        _normalise_tile(x_ref, g_ref, xn_ref)

    xn = xn_ref[...]
    a = _dot(xn, wa_ref[...].astype(BF16))
    u = _dot(xn, wu_ref[...].astype(BF16))
    o_ref[...] = (a * _sigmoid(a) * u).astype(BF16)


def _swiglu_up(x, g, w_gu, name):
    m, d = x.shape
    d_ff = w_gu.shape[1] // 2
    tm, tn = _row_tile(m), _col_tile(d_ff, 512)
    nblk = d_ff // tn
    return pl.pallas_call(
        _swiglu_up_kernel,
        grid=(m // tm, nblk),
        in_specs=[
            pl.BlockSpec((tm, d), lambda i, j: (i, 0)),
            pl.BlockSpec((1, d), lambda i, j: (0, 0)),
            pl.BlockSpec((d, tn), lambda i, j: (0, j)),
            pl.BlockSpec((d, tn), lambda i, j: (0, j + nblk)),
        ],
        out_specs=pl.BlockSpec((tm, tn), lambda i, j: (i, j)),
        out_shape=jax.ShapeDtypeStruct((m, d_ff), BF16),
        scratch_shapes=[pltpu.VMEM((tm, d), BF16)],
        compiler_params=_params("parallel", "arbitrary"),
        name=name,
    )(x, g.reshape(1, d), w_gu, w_gu)


def _matmul_residual_kernel(a_ref, w_ref, r_ref, o_ref):
    o_ref[...] = r_ref[...] + _dot(a_ref[...], w_ref[...].astype(BF16))


def _matmul_residual(a, w, res, tn_cap, name):
    m, k = a.shape
    n = w.shape[1]
    tm, tn = _row_tile(m), _col_tile(n, tn_cap)
    return pl.pallas_call(
        _matmul_residual_kernel,
        grid=(m // tm, n // tn),
        in_specs=[
            pl.BlockSpec((tm, k), lambda i, j: (i, 0)),
            pl.BlockSpec((k, tn), lambda i, j: (0, j)),
            pl.BlockSpec((tm, tn), lambda i, j: (i, j)),
        ],
        out_specs=pl.BlockSpec((tm, tn), lambda i, j: (i, j)),
        out_shape=jax.ShapeDtypeStruct((m, n), F32),
        compiler_params=_params("parallel", "arbitrary"),
        name=name,
    )(a, w, res)


def _lower_bound(logits, layer):
    e = jnp.exp(logits - jnp.max(logits, axis=0, keepdims=True))
    return jnp.sum(e[:layer + 1], axis=0, keepdims=True) / jnp.sum(e, axis=0, keepdims=True)


def _gate_inputs(zq, zf, lb):
    f = lb + (1.0 - lb) * _sigmoid(zf)
    return zq * _sigmoid(zq), 1.0 - f, jnp.log(f)


def _gated_output(o, zg, gain):
    return _rms(o, gain) * (zg * _sigmoid(zg))


def _cumsum_rows(g, seg=None):
    n = g.shape[0]
    row = lax.broadcasted_iota(jnp.int32, g.shape, 0)
    pos = row if seg is None else _div_mod(row, seg)[1]
    span = n if seg is None else seg
    s = 1
    while s < span:
        g = g + jnp.where(pos >= s, pltpu.roll(g, s, 0), 0.0)
        s *= 2
    return g


def _gla_prompt_kernel(zq_ref, zf_ref, zi_ref, zg_ref, lbl_ref, gain_ref, o_ref, s_ref, st_ref,
                       *, layer):
    t = pl.program_id(1)
    c = GLA_CHUNK
    mid = (c - 1) // 2

    @pl.when(t == 0)
    def _():
        st_ref[...] = jnp.zeros_like(st_ref)

    lb = _lower_bound(lbl_ref[...], layer)
    gain = gain_ref[...]
    r2 = lax.broadcasted_iota(jnp.int32, (c, c), 0)
    c2 = lax.broadcasted_iota(jnp.int32, (c, c), 1)

    def body(i, carry):
        rows = pl.ds(pl.multiple_of(i * c, c), c)
        q, k, g = _gate_inputs(zq_ref[rows, :], zf_ref[rows, :], lb)
        v16 = zi_ref[rows, :].astype(BF16)
        cum = _cumsum_rows(g)
        ref = cum[mid:mid + 1, :]
        last = cum[c - 1:c, :]
        att = _dot_nt((q * jnp.exp(cum - ref)).astype(BF16), (k * jnp.exp(ref - cum)).astype(BF16))
        att = jnp.where(r2 >= c2, att, 0.0)
        st = st_ref[...]
        o = _dot(att.astype(BF16), v16) + _dot_nt((q * jnp.exp(cum)).astype(BF16), st.astype(BF16))
        k_out = (k * jnp.exp(last - cum)).astype(BF16)
        st_ref[...] = jnp.exp(last) * st + _dot_tn(v16, k_out)
        o_ref[rows, :] = _gated_output(o, zg_ref[rows, :], gain).astype(BF16)
        return carry

    lax.fori_loop(0, o_ref.shape[0] // c, body, 0)

    @pl.when(t == pl.num_programs(1) - 1)
    def _():
        s_ref[...] = st_ref[...].T


def _gla_prompt(z, lb_logits, gain, layer, seq, tl=512):
    heads = z.shape[1] // (4 * HEAD)
    tl = min(tl, seq)
    slots = lb_logits.shape[0]
    zspec = lambda off: pl.BlockSpec((tl, HEAD), lambda h, t: (t, h + off * heads))
    return pl.pallas_call(
        functools.partial(_gla_prompt_kernel, layer=layer),
        grid=(heads, seq // tl),
        in_specs=[zspec(0), zspec(1), zspec(2), zspec(3),
                  pl.BlockSpec((slots, HEAD), lambda h, t: (0, h)),
                  pl.BlockSpec((1, HEAD), lambda h, t: (0, 0))],
        out_specs=[pl.BlockSpec((tl, HEAD), lambda h, t: (t, h)),
                   pl.BlockSpec((None, HEAD, HEAD), lambda h, t: (h, 0, 0))],
        out_shape=[jax.ShapeDtypeStruct((seq, heads * HEAD), BF16),
                   jax.ShapeDtypeStruct((heads, HEAD, HEAD), F32)],
        scratch_shapes=[pltpu.VMEM((HEAD, HEAD), F32)],
        compiler_params=_params("parallel", "arbitrary"),
        name="gla_prompt",
    )(z, z, z, z, lb_logits, gain.reshape(1, HEAD))


def _gla_sample_kernel(zq_ref, zf_ref, zi_ref, zg_ref, lbl_ref, gain_ref, s0_ref, o_ref, s_ref,
                       last_ref, kout_ref, qin_ref, v_ref, *, layer, steps):
    rows = zq_ref.shape[0]
    mid = (steps - 1) // 2
    lb = _lower_bound(lbl_ref[...], layer)
    q, k, g = _gate_inputs(zq_ref[...], zf_ref[...], lb)
    v16 = zi_ref[...].astype(BF16)
    cum = _cumsum_rows(g, seg=steps)

    row = lax.broadcasted_iota(jnp.int32, cum.shape, 0)
    pos = _div_mod(row, steps)[1]

    def spread(src_pos):
        picked = jnp.where(pos == src_pos, cum, 0.0)
        out = picked
        for d in range(steps):
            if d != src_pos:
                out = out + pltpu.roll(picked, (d - src_pos) % rows, 0)
        return out

    ref = spread(mid)
    last = spread(steps - 1)
    r2 = lax.broadcasted_iota(jnp.int32, (rows, rows), 0)
    c2 = lax.broadcasted_iota(jnp.int32, (rows, rows), 1)
    att = _dot_nt((q * jnp.exp(cum - ref)).astype(BF16), (k * jnp.exp(ref - cum)).astype(BF16))
    att = jnp.where((r2 >= c2) & (_div_mod(r2, steps)[0] == _div_mod(c2, steps)[0]), att, 0.0)
    o_ref[...] = _dot(att.astype(BF16), v16)
    qin_ref[...] = (q * jnp.exp(cum)).astype(BF16)
    v_ref[...] = v16
    last_ref[...] = last
    kout_ref[...] = k * jnp.exp(last - cum)

    def body(b, carry):
        lo = b * steps
        mine = (row >= lo) & (row < lo + steps)
        st = s0_ref[b].T
        o_ref[...] += jnp.where(mine, _dot_nt(qin_ref[...], st.astype(BF16)), 0.0)
        k_out = jnp.where(mine, kout_ref[...], 0.0).astype(BF16)
        decay = jnp.exp(last_ref[pl.ds(lo, 1), :])
        s_ref[b] = (decay * st + _dot_tn(v_ref[...], k_out)).T
        return carry

    lax.fori_loop(0, s0_ref.shape[0], body, 0)
    o_ref[...] = _gated_output(o_ref[...], zg_ref[...], gain_ref[...])


def _gla_sample(z, row0, n_seq, steps, s0, lb_logits, gain, layer):
    heads = z.shape[1] // (4 * HEAD)
    rows = n_seq * steps
    slots = lb_logits.shape[0]
    rb = row0 // rows
    zspec = lambda off: pl.BlockSpec((rows, HEAD), lambda h: (rb, h + off * heads))
    sspec = pl.BlockSpec((n_seq, None, HEAD, HEAD), lambda h: (0, h, 0, 0))
    return pl.pallas_call(
        functools.partial(_gla_sample_kernel, layer=layer, steps=steps),
        grid=(heads,),
        in_specs=[zspec(0), zspec(1), zspec(2), zspec(3),
                  pl.BlockSpec((slots, HEAD), lambda h: (0, h)),
                  pl.BlockSpec((1, HEAD), lambda h: (0, 0)),
                  sspec],
        out_specs=[pl.BlockSpec((rows, HEAD), lambda h: (0, h)), sspec],
        out_shape=[jax.ShapeDtypeStruct((rows, heads * HEAD), F32),
                   jax.ShapeDtypeStruct(s0.shape, s0.dtype)],
        scratch_shapes=[pltpu.VMEM((rows, HEAD), F32), pltpu.VMEM((rows, HEAD), F32),
                        pltpu.VMEM((rows, HEAD), BF16), pltpu.VMEM((rows, HEAD), BF16)],
        compiler_params=_params("parallel"),
        name="gla_sample",
    )(z, z, z, z, lb_logits, gain.reshape(1, HEAD), s0)


def _top_blocks_bits(gate, lane_f, n_valid):
    gate = jnp.where(lane_f < n_valid, gate, -jnp.inf)
    bits = jnp.zeros(gate.shape, jnp.int32)
    for _ in range(MOBA_TOPK):
        mx = jnp.max(gate, axis=1, keepdims=True)
        idx = jnp.min(jnp.where(gate == mx, lane_f, float(LANES)), axis=1, keepdims=True)
        live = mx > -jnp.inf
        gate = jnp.where(lane_f == idx, -jnp.inf, gate)
        shift = jnp.minimum(idx, 31.0).astype(jnp.int32)
        bits = bits | jnp.where(live, jnp.left_shift(1, shift), 0)
    return bits


def _moba_prompt_kernel(q_ref, k_ref, v_ref, o_ref, kmean_ref, qb_ref, sel_ref, m_ref, l_ref,
                        acc_ref, *, n_blocks, group, chunk):
    qi = pl.program_id(1)
    blk = MOBA_BLOCK
    rows = group * blk

    @pl.when(qi == 0)
    def _():
        kmean_ref[...] = jnp.zeros_like(kmean_ref)
        for n in range(n_blocks):
            kmean_ref[n:n + 1, :] = jnp.mean(k_ref[n * blk:(n + 1) * blk, :], axis=0, keepdims=True)

    lane_f = lax.broadcasted_iota(jnp.int32, (blk, LANES), 1).astype(F32)
    qi_f = qi.astype(F32)
    for g in range(group):
        qg = q_ref[:, g * HEAD:(g + 1) * HEAD]
        qb_ref[g * blk:(g + 1) * blk, :] = qg.astype(BF16)
        gate = _dot_nt(qg, kmean_ref[...], precision=lax.Precision.HIGHEST)
        sel_ref[g * blk:(g + 1) * blk, :] = _top_blocks_bits(gate, lane_f, qi_f)

    own = pl.ds(pl.multiple_of(qi * blk, blk), blk)
    kd = k_ref[own, :].astype(BF16)
    vd = v_ref[own, :].astype(BF16)
    for c in range(rows // chunk):
        r = slice(c * chunk, (c + 1) * chunk)
        q_pos = lax.broadcasted_iota(jnp.int32, (chunk, blk), 0) + (c * chunk) % blk
        k_pos = lax.broadcasted_iota(jnp.int32, (chunk, blk), 1)
        s = jnp.where(k_pos <= q_pos, _dot_nt(qb_ref[r, :], kd) * SCALE_LOG2E, NEG)
        m = jnp.max(s, axis=1, keepdims=True)
        p = jnp.exp2(s - m)
        m_ref[r, :] = jnp.broadcast_to(m, (chunk, LANES))
        l_ref[r, :] = jnp.broadcast_to(jnp.sum(p, axis=1, keepdims=True), (chunk, LANES))
        acc_ref[r, :] = _dot(p.astype(BF16), vd)

    def body(j, carry):
        past = pl.ds(pl.multiple_of(j * blk, blk), blk)
        kj = k_ref[past, :].astype(BF16)
        vj = v_ref[past, :].astype(BF16)
        bit = jnp.left_shift(jnp.int32(1), j)
        n_chunks = rows // chunk

        def scores(c):
            return _dot_nt(qb_ref[c * chunk:(c + 1) * chunk, :], kj)

        pending = [scores(c) for c in range(min(SCORE_LOOKAHEAD, n_chunks))]
        for c in range(n_chunks):
            r = slice(c * chunk, (c + 1) * chunk)
            on = (sel_ref[r, :] & bit) != 0
            s = pending.pop(0) * SCALE_LOG2E
            if c + SCORE_LOOKAHEAD < n_chunks:
                pending.append(scores(c + SCORE_LOOKAHEAD))
            s0 = jnp.where(on, s[:, :LANES], NEG)
            s1 = jnp.where(on, s[:, LANES:], NEG)
            m_prev = m_ref[r, :]
            m_new = jnp.maximum(m_prev, jnp.max(jnp.maximum(s0, s1), axis=1, keepdims=True))
            p0 = jnp.exp2(s0 - m_new)
            p1 = jnp.exp2(s1 - m_new)
            alpha = jnp.exp2(m_prev - m_new)
            l_ref[r, :] = alpha * l_ref[r, :] + jnp.sum(p0 + p1, axis=1, keepdims=True)
            p = jnp.concatenate([p0, p1], axis=1).astype(BF16)
            acc_ref[r, :] = alpha * acc_ref[r, :] + _dot(p, vj)
            m_ref[r, :] = m_new
        return carry

    lax.fori_loop(0, qi, body, 0)
    for g in range(group):
        r = slice(g * blk, (g + 1) * blk)
        o_ref[:, g * HEAD:(g + 1) * HEAD] = (acc_ref[r, :] / l_ref[r, :]).astype(BF16)


def _moba_prompt(q, kv, seq, n_kv, chunk=PROMPT_ROW_CHUNK):
    heads = q.shape[1] // HEAD
    group = heads // n_kv
    n_blocks = seq // MOBA_BLOCK
    blk = MOBA_BLOCK
    rows = group * blk
    return pl.pallas_call(
        functools.partial(_moba_prompt_kernel, n_blocks=n_blocks, group=group, chunk=chunk),
        grid=(n_kv, n_blocks),
        in_specs=[pl.BlockSpec((blk, group * HEAD), lambda h, i: (i, h)),
                  pl.BlockSpec((seq, HEAD), lambda h, i: (0, h)),
                  pl.BlockSpec((seq, HEAD), lambda h, i: (0, n_kv + h))],
        out_specs=pl.BlockSpec((blk, group * HEAD), lambda h, i: (i, h)),
        out_shape=jax.ShapeDtypeStruct((seq, heads * HEAD), BF16),
        scratch_shapes=[pltpu.VMEM((LANES, HEAD), F32),
                        pltpu.VMEM((rows, HEAD), BF16),
                        pltpu.VMEM((rows, LANES), jnp.int32),
                        pltpu.VMEM((rows, LANES), F32),
                        pltpu.VMEM((rows, LANES), F32),
                        pltpu.VMEM((rows, HEAD), F32)],
        compiler_params=_params("arbitrary", "arbitrary"),
        name="moba_prompt",
    )(q, kv, kv)


def _moba_sample_kernel(pt_ref, q_ref, kn_ref, vn_ref, *refs, n_kv, group, steps, pages_per_block):
    del pt_ref
    kp = refs[:pages_per_block]
    vp = refs[pages_per_block:2 * pages_per_block]
    o_ref, m_s, l_s, g_s, o_s = refs[2 * pages_per_block:]
    j = pl.program_id(1)
    n_blocks = pl.num_programs(1)
    rows = q_ref.shape[1]
    rq = group * steps
    tok_per_vreg = SUBLANES // n_kv

    qf = q_ref[0]
    q16 = qf.astype(BF16)
    kb = jnp.concatenate([r[0] for r in kp], axis=0)
    vb = jnp.concatenate([r[0] for r in vp], axis=0)
    n_col = kb.shape[0]
    row_head = _div_mod(lax.broadcasted_iota(jnp.int32, (rows, n_col), 0), rq)[0]
    col_head = _div_mod(lax.broadcasted_iota(jnp.int32, (rows, n_col), 1), n_kv)[1]
    s = jnp.where(row_head == col_head, _dot_nt(q16, kb.astype(BF16)) * SCALE, NEG)
    m = jnp.max(s, axis=1, keepdims=True)
    p = jnp.exp(s - m)

    parts = [kb[SUBLANES * i:SUBLANES * (i + 1), :] for i in range(n_col // SUBLANES)]
    while len(parts) > 1:
        parts = [a + b for a, b in zip(parts[::2], parts[1::2])]
    folded = parts[0]
    head64 = _div_mod(lax.broadcasted_iota(jnp.int32, (rows, HEAD), 0), rq)[0]
    kmean = jnp.zeros((rows, HEAD), F32)
    for h in range(n_kv):
        total = folded[h:h + 1, :]
        for i in range(1, tok_per_vreg):
            total = total + folded[h + i * n_kv:h + i * n_kv + 1, :]
        kmean = jnp.where(head64 == h, total * (1.0 / MOBA_BLOCK), kmean)
    gate = jnp.sum(qf * kmean, axis=1, keepdims=True)

    m_s[j] = jnp.broadcast_to(m, (rows, LANES))
    l_s[j] = jnp.broadcast_to(jnp.sum(p, axis=1, keepdims=True), (rows, LANES))
    g_s[j] = jnp.broadcast_to(gate, (rows, LANES))
    o_s[j] = _dot(p.astype(BF16), vb.astype(BF16))

    @pl.when(j == n_blocks - 1)
    def _():
        gates = g_s[...]
        n_io = lax.broadcasted_iota(jnp.int32, gates.shape, 0)
        sel = jnp.zeros(gates.shape, jnp.bool_)
        for _ in range(min(MOBA_TOPK, g_s.shape[0])):
            mx = jnp.max(gates, axis=0, keepdims=True)
            idx = jnp.min(jnp.where(gates == mx, n_io, g_s.shape[0]), axis=0, keepdims=True)
            pick = n_io == idx
            sel = sel | pick
            gates = jnp.where(pick, -jnp.inf, gates)

        row = lax.broadcasted_iota(jnp.int32, (rows, LANES), 0)
        lane = lax.broadcasted_iota(jnp.int32, (rows, LANES), 1)
        q_head, q_step = _div_mod(row, rq)[0], _div_mod(row, steps)[1]
        k_step, k_head = _div_mod(lane, n_kv)
        ok = (q_head == k_head) & (k_step <= q_step) & (k_step < steps)
        s_own = jnp.where(ok, _dot_nt(q16, kn_ref[0].astype(BF16)) * SCALE, NEG)
        m_own = jnp.max(s_own, axis=1, keepdims=True)
        p_own = jnp.exp(s_own - m_own)
        l_own = jnp.sum(p_own, axis=1, keepdims=True)
        o_own = _dot(p_own.astype(BF16), vn_ref[0].astype(BF16))

        m_all = m_s[...]
        m_top = jnp.maximum(m_own, jnp.max(jnp.where(sel, m_all, -jnp.inf), axis=0))
        w = jnp.where(sel, jnp.exp(jnp.minimum(m_all - m_top[None], 0.0)), 0.0)
        w_own = jnp.exp(m_own - m_top)
        den = w_own * l_own + jnp.sum(w * l_s[...], axis=0)
        num = w_own * o_own + jnp.sum(w * o_s[...], axis=0)
        o_ref[0] = num / den


def _moba_sample(q, k_new, v_new, cache_k, cache_v, page_table, n_kv, steps):
    n_seq, n_pages = page_table.shape
    n_phys, page, _, _ = cache_k.shape
    ppb = MOBA_BLOCK // page
    n_blocks = n_pages // ppb
    rows = q.shape[1]
    group = rows // (n_kv * steps)
    assert SUBLANES % n_kv == 0 and steps * n_kv <= LANES
    ck = cache_k.reshape(n_phys, page * n_kv, HEAD)
    cv = cache_v.reshape(n_phys, page * n_kv, HEAD)

    def page_spec(p):
        return pl.BlockSpec((1, page * n_kv, HEAD), lambda b, j, pt: (pt[b, j * ppb + p], 0, 0))

    per_seq = lambda shape: pl.BlockSpec(shape, lambda b, j, pt: (b, 0, 0))
    grid_spec = pltpu.PrefetchScalarGridSpec(
        num_scalar_prefetch=1,
        grid=(n_seq, n_blocks),
        in_specs=[per_seq((1, rows, HEAD)), per_seq((1, LANES, HEAD)), per_seq((1, LANES, HEAD))]
                 + [page_spec(p) for p in range(ppb)] + [page_spec(p) for p in range(ppb)],
        out_specs=per_seq((1, rows, HEAD)),
        scratch_shapes=[pltpu.VMEM((n_blocks, rows, LANES), F32) for _ in range(4)],
    )
    return pl.pallas_call(
        functools.partial(_moba_sample_kernel, n_kv=n_kv, group=group, steps=steps,
                          pages_per_block=ppb),
        grid_spec=grid_spec,
        out_shape=jax.ShapeDtypeStruct((n_seq, rows, HEAD), F32),
        compiler_params=_params("arbitrary", "arbitrary"),
        name="moba_sample",
    )(page_table, q, k_new, v_new, *([ck] * ppb), *([cv] * ppb))


def kernel(x_prompt, x_sample, state_hgrn, cache_k, cache_v, page_table, norm_mix_a, w_in_a, lb_logits,
           onorm_a, w_out_a, norm_kv, w_kv, k_norm, norm_mix_b, w_q_b, q_norm, w_o_b, norm_ffn,
           w_gate_up, w_down):
    batch, seq, d = x_prompt.shape
    n_seq, steps, _ = x_sample.shape
    n_a = w_in_a.shape[0]
    depth = norm_ffn.shape[0]
    heads = d // HEAD
    n_kv = cache_k.shape[2]
    group = heads // n_kv
    kv_width = n_kv * HEAD
    n_dec = n_seq * steps
    past = page_table.shape[1] * cache_k.shape[1]
    assert batch == 1 and seq % MOBA_BLOCK == 0 and seq // MOBA_BLOCK <= 32
    assert past % MOBA_BLOCK == 0 and MOBA_BLOCK % cache_k.shape[1] == 0
    assert steps <= LANES and seq % n_dec == 0 and seq % GLA_CHUNK == 0

    h = jnp.concatenate([x_prompt.reshape(seq, d), x_sample.reshape(n_dec, d)], axis=0)
    states_p, states_s = [], []
    kv = None
    for layer in range(depth):
        if layer < n_a:
            z = _norm_matmul(h, norm_mix_a[layer], w_in_a[layer], name="hgrn_in")
            o_p, s_p = _gla_prompt(z, lb_logits, onorm_a[layer], layer, seq)
            o_s, s_s = _gla_sample(z, seq, n_seq, steps, state_hgrn[layer], lb_logits,
                                   onorm_a[layer], layer)
            o = jnp.concatenate([o_p, o_s.astype(BF16)], axis=0)
            h = _matmul_residual(o, w_out_a[layer], h, 512, "hgrn_out")
            states_p.append(s_p.reshape(1, heads, HEAD, HEAD).astype(state_hgrn.dtype))
            states_s.append(s_s)
        else:
            if kv is None:
                kv = _norm_matmul(h, norm_kv, w_kv, head_gain=k_norm, norm_blocks=1,
                                  tn_cap=kv_width, name="kv_proj")
            jb = layer - n_a
            q = _norm_matmul(h, norm_mix_b[jb], w_q_b[jb], head_gain=q_norm[jb],
                             norm_blocks=heads, name="q_proj")
            att_p = _moba_prompt(q, kv, seq, n_kv)
            q_s = q[seq:].reshape(n_seq, steps, n_kv, group, HEAD).transpose(0, 2, 3, 1, 4)
            pad = ((0, 0), (0, LANES - steps * n_kv), (0, 0))
            k_s = jnp.pad(kv[seq:, :kv_width].reshape(n_seq, steps * n_kv, HEAD), pad)
            v_s = jnp.pad(kv[seq:, kv_width:].reshape(n_seq, steps * n_kv, HEAD), pad)
            att_s = _moba_sample(q_s.reshape(n_seq, n_kv * group * steps, HEAD), k_s, v_s,
                                 cache_k, cache_v, page_table, n_kv, steps)
            att_s = att_s.reshape(n_seq, n_kv, group, steps, HEAD).transpose(0, 3, 1, 2, 4)
            att = jnp.concatenate([att_p, att_s.reshape(n_dec, d).astype(BF16)], axis=0)
            h = _matmul_residual(att, w_o_b[jb], h, 512, "attn_out")
        hf = _swiglu_up(h, norm_ffn[layer], w_gate_up[layer], "ffn_up")
        h = _matmul_residual(hf, w_down[layer], h, 256, "ffn_down")

    k_new = kv[:, :kv_width]
    v_new = kv[:, kv_width:]
    return (h[:seq].reshape(batch, seq, d),
            h[seq:].reshape(n_seq, steps, d),
            jnp.stack(states_p),
            jnp.stack(states_s),
            k_new[:seq].reshape(batch, seq, n_kv, HEAD),
            v_new[:seq].reshape(batch, seq, n_kv, HEAD),
            k_new[seq:].reshape(n_seq, steps, n_kv, HEAD),
            v_new[seq:].reshape(n_seq, steps, n_kv, HEAD))
```

```python
import functools

import jax
import jax.numpy as jnp
from jax import lax
from jax.experimental import pallas as pl
from jax.experimental.pallas import tpu as pltpu

F32 = jnp.float32
BF16 = jnp.bfloat16

EPS = 1e-6
HEAD = 128
GLA_CHUNK = 64
MOBA_BLOCK = 256
MOBA_TOPK = 3
SCALE = HEAD ** -0.5
SCALE_LOG2E = SCALE * 1.4426950408889634
NEG = -1e30
LANES = 128
SUBLANES = 8
NORM_ROWS = 16
NORM_UNROLL = 5
ROW_TILE_CAP = 1040
PROMPT_ROW_CHUNK = 128
SCORE_LOOKAHEAD = 2
SAMPLE_BLOCKS_PER_STEP = 4
SAMPLE_SEQ_UNROLL = 4
VMEM_LIMIT = 56 * 1024 * 1024


def _dot(a, b):
    return jnp.dot(a, b, preferred_element_type=F32)


def _dot_nt(a, b, precision=None):
    return lax.dot_general(a, b, (((1,), (1,)), ((), ())), precision=precision,
                           preferred_element_type=F32)


def _dot_tn(a, b):
    return lax.dot_general(a, b, (((0,), (0,)), ((), ())), preferred_element_type=F32)


def _div_mod(x, n):
    if n & (n - 1) == 0:
        return jnp.right_shift(x, n.bit_length() - 1), x & (n - 1)
    q = x // n
    return q, x - q * n


def _sigmoid(x):
    return 1.0 / (1.0 + jnp.exp(-x))


def _rms(x, g):
    return x * lax.rsqrt(jnp.mean(x * x, axis=-1, keepdims=True) + EPS) * g


def _row_tile(m, cap=ROW_TILE_CAP):
    for t in range(min(m, cap), 0, -1):
        if m % t == 0 and t % NORM_ROWS == 0:
            return t
    raise ValueError(f"no row tile for {m} rows")


def _col_tile(n, cap):
    for t in range(min(n, cap), 0, -1):
        if n % t == 0 and t % LANES == 0:
            return t
    raise ValueError(f"no column tile for {n} columns")


def _unroll(trips, cap):
    for u in range(min(trips, cap), 0, -1):
        if trips % u == 0:
            return u
    return 1


def _params(*sem):
    return pltpu.CompilerParams(dimension_semantics=sem, vmem_limit_bytes=VMEM_LIMIT)


def _normalise_tile(x_ref, g_ref, xn_ref):
    g = g_ref[...]
    trips = x_ref.shape[0] // NORM_ROWS

    def body(i, carry):
        rows = pl.ds(pl.multiple_of(i * NORM_ROWS, NORM_ROWS), NORM_ROWS)
        xn_ref[rows, :] = _rms(x_ref[rows, :], g).astype(BF16)
        return carry

    lax.fori_loop(0, trips, body, 0, unroll=_unroll(trips, NORM_UNROLL))


def _norm_matmul_kernel(x_ref, g_ref, w_ref, hg_ref, o_ref, xn_ref, *, norm_blocks):
    j = pl.program_id(1)

    @pl.when(j == 0)
    def _():
        _normalise_tile(x_ref, g_ref, xn_ref)

    acc = _dot(xn_ref[...], w_ref[...].astype(BF16))
    if norm_blocks == 0:
        o_ref[...] = acc
        return

    @pl.when(j < norm_blocks)
    def _():
        for h in range(acc.shape[1] // HEAD):
            cols = slice(h * HEAD, (h + 1) * HEAD)
            o_ref[:, cols] = _rms(acc[:, cols], hg_ref[...])

    @pl.when(j >= norm_blocks)
    def _():
        o_ref[...] = acc


def _norm_matmul(x, g, w, layer, head_gain=None, norm_blocks=0, tn_cap=512, name="norm_matmul"):
    m, d = x.shape
    n = w.shape[-1]
    tm, tn = _row_tile(m), _col_tile(n, tn_cap)
    if head_gain is None:
        head_gain = jnp.ones((HEAD,), F32)
    if w.ndim == 2:
        w = w[None]
    return pl.pallas_call(
        functools.partial(_norm_matmul_kernel, norm_blocks=norm_blocks),
        grid=(m // tm, n // tn),
        in_specs=[
            pl.BlockSpec((tm, d), lambda i, j: (i, 0)),
            pl.BlockSpec((1, d), lambda i, j: (0, 0)),
            pl.BlockSpec((None, d, tn), lambda i, j: (layer, 0, j)),
            pl.BlockSpec((1, HEAD), lambda i, j: (0, 0)),
        ],
        out_specs=pl.BlockSpec((tm, tn), lambda i, j: (i, j)),
        out_shape=jax.ShapeDtypeStruct((m, n), F32),
        scratch_shapes=[pltpu.VMEM((tm, d), BF16)],
        compiler_params=_params("parallel", "arbitrary"),
        name=name,
    )(x, g.reshape(1, d), w, head_gain.reshape(1, HEAD))


def _swiglu_up_kernel(x_ref, g_ref, wa_ref, wu_ref, o_ref, xn_ref):
    @pl.when(pl.program_id(1) == 0)
    def _():
        _normalise_tile(x_ref, g_ref, xn_ref)

    xn = xn_ref[...]
    a = _dot(xn, wa_ref[...].astype(BF16))
    u = _dot(xn, wu_ref[...].astype(BF16))
    o_ref[...] = (a * _sigmoid(a) * u).astype(BF16)


def _swiglu_up(x, g, w_gu, layer, name):
    m, d = x.shape
    d_ff = w_gu.shape[-1] // 2
    tm, tn = _row_tile(m), _col_tile(d_ff, 512)
    nblk = d_ff // tn
    return pl.pallas_call(
        _swiglu_up_kernel,
        grid=(m // tm, nblk),
        in_specs=[
            pl.BlockSpec((tm, d), lambda i, j: (i, 0)),
            pl.BlockSpec((1, d), lambda i, j: (0, 0)),
            pl.BlockSpec((None, d, tn), lambda i, j: (layer, 0, j)),
            pl.BlockSpec((None, d, tn), lambda i, j: (layer, 0, j + nblk)),
        ],
        out_specs=pl.BlockSpec((tm, tn), lambda i, j: (i, j)),
        out_shape=jax.ShapeDtypeStruct((m, d_ff), BF16),
        scratch_shapes=[pltpu.VMEM((tm, d), BF16)],
        compiler_params=_params("parallel", "arbitrary"),
        name=name,
    )(x, g.reshape(1, d), w_gu, w_gu)


def _matmul_residual_kernel(a_ref, w_ref, r_ref, o_ref):
    o_ref[...] = r_ref[...] + _dot(a_ref[...], w_ref[...].astype(BF16))


def _matmul_residual(a, w, layer, res, tn_cap, name):
    m, k = a.shape
    n = w.shape[-1]
    tm, tn = _row_tile(m), _col_tile(n, tn_cap)
    return pl.pallas_call(
        _matmul_residual_kernel,
        grid=(m // tm, n // tn),
        in_specs=[
            pl.BlockSpec((tm, k), lambda i, j: (i, 0)),
            pl.BlockSpec((None, k, tn), lambda i, j: (layer, 0, j)),
            pl.BlockSpec((tm, tn), lambda i, j: (i, j)),
        ],
        out_specs=pl.BlockSpec((tm, tn), lambda i, j: (i, j)),
        out_shape=jax.ShapeDtypeStruct((m, n), F32),
        compiler_params=_params("parallel", "arbitrary"),
        name=name,
    )(a, w, res)


def _lower_bound(logits, layer):
    e = jnp.exp(logits - jnp.max(logits, axis=0, keepdims=True))
    return jnp.sum(e[:layer + 1], axis=0, keepdims=True) / jnp.sum(e, axis=0, keepdims=True)


def _gate_inputs(zq, zf, lb):
    f = lb + (1.0 - lb) * _sigmoid(zf)
    return zq * _sigmoid(zq), 1.0 - f, jnp.log(f)


def _gated_output(o, zg, gain):
    return _rms(o, gain) * (zg * _sigmoid(zg))


def _cumsum_rows(g, seg=None):
    n = g.shape[0]
    row = lax.broadcasted_iota(jnp.int32, g.shape, 0)
    pos = row if seg is None else _div_mod(row, seg)[1]
    span = n if seg is None else seg
    s = 1
    while s < span:
        g = g + jnp.where(pos >= s, pltpu.roll(g, s, 0), 0.0)
        s *= 2
    return g


def _gla_prompt_kernel(zq_ref, zf_ref, zi_ref, zg_ref, lbl_ref, gain_ref, o_ref, s_ref, st_ref,
                       *, layer):
    t = pl.program_id(1)
    c = GLA_CHUNK
    mid = (c - 1) // 2

    @pl.when(t == 0)
    def _():
        st_ref[...] = jnp.zeros_like(st_ref)

    lb = _lower_bound(lbl_ref[...], layer)
    gain = gain_ref[...]
    r2 = lax.broadcasted_iota(jnp.int32, (c, c), 0)
    c2 = lax.broadcasted_iota(jnp.int32, (c, c), 1)
    st = st_ref[...]
    for i in range(o_ref.shape[0] // c):
        rows = slice(i * c, (i + 1) * c)
        q, k, g = _gate_inputs(zq_ref[rows, :], zf_ref[rows, :], lb)
        v16 = zi_ref[rows, :].astype(BF16)
        cum = _cumsum_rows(g)
        ref = cum[mid:mid + 1, :]
        last = cum[c - 1:c, :]
        att = _dot_nt((q * jnp.exp(cum - ref)).astype(BF16), (k * jnp.exp(ref - cum)).astype(BF16))
        att = jnp.where(r2 >= c2, att, 0.0)
        o = _dot(att.astype(BF16), v16) + _dot_nt((q * jnp.exp(cum)).astype(BF16), st.astype(BF16))
        k_out = (k * jnp.exp(last - cum)).astype(BF16)
        st = jnp.exp(last) * st + _dot_tn(v16, k_out)
        o_ref[rows, :] = _gated_output(o, zg_ref[rows, :], gain).astype(BF16)
    st_ref[...] = st

    @pl.when(t == pl.num_programs(1) - 1)
    def _():
        s_ref[...] = st.T


def _gla_prompt(z, lb_logits, gain, layer, seq, tl=512):
    heads = z.shape[1] // (4 * HEAD)
    tl = min(tl, seq)
    slots = lb_logits.shape[0]
    zspec = lambda off: pl.BlockSpec((tl, HEAD), lambda h, t: (t, h + off * heads))
    return pl.pallas_call(
        functools.partial(_gla_prompt_kernel, layer=layer),
        grid=(heads, seq // tl),
        in_specs=[zspec(0), zspec(1), zspec(2), zspec(3),
                  pl.BlockSpec((slots, HEAD), lambda h, t: (0, h)),
                  pl.BlockSpec((1, HEAD), lambda h, t: (0, 0))],
        out_specs=[pl.BlockSpec((tl, HEAD), lambda h, t: (t, h)),
                   pl.BlockSpec((None, HEAD, HEAD), lambda h, t: (h, 0, 0))],
        out_shape=[jax.ShapeDtypeStruct((z.shape[0], heads * HEAD), BF16),
                   jax.ShapeDtypeStruct((heads, HEAD, HEAD), F32)],
        scratch_shapes=[pltpu.VMEM((HEAD, HEAD), F32)],
        compiler_params=_params("parallel", "arbitrary"),
        name="gla_prompt",
    )(z, z, z, z, lb_logits, gain.reshape(1, HEAD))


def _gla_sample_kernel(zq_ref, zf_ref, zi_ref, zg_ref, lbl_ref, gain_ref, s0_ref, o_in_ref,
                       o_ref, s_ref, last_ref, kout_ref, qin_ref, v_ref, *, layer, steps):
    del o_in_ref
    rows = zq_ref.shape[0]
    mid = (steps - 1) // 2
    lb = _lower_bound(lbl_ref[...], layer)
    q, k, g = _gate_inputs(zq_ref[...], zf_ref[...], lb)
    v16 = zi_ref[...].astype(BF16)
    cum = _cumsum_rows(g, seg=steps)

    row = lax.broadcasted_iota(jnp.int32, cum.shape, 0)
    pos = _div_mod(row, steps)[1]

    def spread(src_pos):
        picked = jnp.where(pos == src_pos, cum, 0.0)
        out = picked
        for d in range(steps):
            if d != src_pos:
                out = out + pltpu.roll(picked, (d - src_pos) % rows, 0)
        return out

    ref = spread(mid)
    last = spread(steps - 1)
    r2 = lax.broadcasted_iota(jnp.int32, (rows, rows), 0)
    c2 = lax.broadcasted_iota(jnp.int32, (rows, rows), 1)
    att = _dot_nt((q * jnp.exp(cum - ref)).astype(BF16), (k * jnp.exp(ref - cum)).astype(BF16))
    att = jnp.where((r2 >= c2) & (_div_mod(r2, steps)[0] == _div_mod(c2, steps)[0]), att, 0.0)
    o_intra = _dot(att.astype(BF16), v16)
    qin_ref[...] = (q * jnp.exp(cum)).astype(BF16)
    v_ref[...] = v16
    last_ref[...] = last
    kout_ref[...] = k * jnp.exp(last - cum)
    n_seq = s0_ref.shape[0]

    def body(b, o_inter):
        lo = b * steps
        mine = (row >= lo) & (row < lo + steps)
        st = s0_ref[b].T
        o_inter = o_inter + jnp.where(mine, _dot_nt(qin_ref[...], st.astype(BF16)), 0.0)
        k_out = jnp.where(mine, kout_ref[...], 0.0).astype(BF16)
        decay = jnp.exp(last_ref[pl.ds(lo, 1), :])
        s_ref[b] = (decay * st + _dot_tn(v_ref[...], k_out)).T
        return o_inter

    o_inter = lax.fori_loop(0, n_seq, body, jnp.zeros(cum.shape, F32),
                            unroll=_unroll(n_seq, SAMPLE_SEQ_UNROLL))
    o_ref[...] = _gated_output(o_intra + o_inter, zg_ref[...], gain_ref[...]).astype(BF16)


def _gla_sample(z, o_all, row0, n_seq, steps, s0, layer_s0, lb_logits, gain, layer):
    heads = z.shape[1] // (4 * HEAD)
    rows = n_seq * steps
    slots = lb_logits.shape[0]
    rb = row0 // rows
    zspec = lambda off: pl.BlockSpec((rows, HEAD), lambda h: (rb, h + off * heads))
    s_in = pl.BlockSpec((None, n_seq, None, HEAD, HEAD), lambda h: (layer_s0, 0, h, 0, 0))
    s_out = pl.BlockSpec((n_seq, None, HEAD, HEAD), lambda h: (0, h, 0, 0))
    return pl.pallas_call(
        functools.partial(_gla_sample_kernel, layer=layer, steps=steps),
        grid=(heads,),
        in_specs=[zspec(0), zspec(1), zspec(2), zspec(3),
                  pl.BlockSpec((slots, HEAD), lambda h: (0, h)),
                  pl.BlockSpec((1, HEAD), lambda h: (0, 0)),
                  s_in,
                  pl.BlockSpec(memory_space=pl.ANY)],
        out_specs=[pl.BlockSpec((rows, HEAD), lambda h: (rb, h)), s_out],
        out_shape=[jax.ShapeDtypeStruct(o_all.shape, o_all.dtype),
                   jax.ShapeDtypeStruct(s0.shape[1:], s0.dtype)],
        scratch_shapes=[pltpu.VMEM((rows, HEAD), F32), pltpu.VMEM((rows, HEAD), F32),
                        pltpu.VMEM((rows, HEAD), BF16), pltpu.VMEM((rows, HEAD), BF16)],
        input_output_aliases={7: 0},
        compiler_params=_params("parallel"),
        name="gla_sample",
    )(z, z, z, z, lb_logits, gain.reshape(1, HEAD), s0, o_all)


def _top_blocks_bits(gate, lane_f, n_valid):
    gate = jnp.where(lane_f < n_valid, gate, -jnp.inf)
    bits = jnp.zeros(gate.shape, jnp.int32)
    for _ in range(MOBA_TOPK):
        mx = jnp.max(gate, axis=1, keepdims=True)
        idx = jnp.min(jnp.where(gate == mx, lane_f, float(LANES)), axis=1, keepdims=True)
        live = mx > -jnp.inf
        gate = jnp.where(lane_f == idx, -jnp.inf, gate)
        shift = jnp.minimum(idx, 31.0).astype(jnp.int32)
        bits = bits | jnp.where(live, jnp.left_shift(1, shift), 0)
    return bits


def _moba_prompt_kernel(q_ref, k_ref, v_ref, o_ref, kmean_ref, qb_ref, sel_ref, m_ref, l_ref,
                        acc_ref, *, n_blocks, group, chunk):
    qi = pl.program_id(1)
    blk = MOBA_BLOCK
    rows = group * blk
    n_chunks = rows // chunk

    @pl.when(qi == 0)
    def _():
        kmean_ref[...] = jnp.zeros_like(kmean_ref)
        for n in range(n_blocks):
            kmean_ref[n:n + 1, :] = jnp.mean(k_ref[n * blk:(n + 1) * blk, :], axis=0, keepdims=True)

    lane_f = lax.broadcasted_iota(jnp.int32, (blk, LANES), 1).astype(F32)
    qi_f = qi.astype(F32)
    for g in range(group):
        qg = q_ref[:, g * HEAD:(g + 1) * HEAD]
        qb_ref[g * blk:(g + 1) * blk, :] = qg.astype(BF16)
        gate = _dot_nt(qg, kmean_ref[...], precision=lax.Precision.HIGHEST)
        sel_ref[g * blk:(g + 1) * blk, :] = _top_blocks_bits(gate, lane_f, qi_f)

    own = pl.ds(pl.multiple_of(qi * blk, blk), blk)
    kd = k_ref[own, :].astype(BF16)
    vd = v_ref[own, :].astype(BF16)
    for c in range(n_chunks):
        r = slice(c * chunk, (c + 1) * chunk)
        q_pos = lax.broadcasted_iota(jnp.int32, (chunk, blk), 0) + (c * chunk) % blk
        k_pos = lax.broadcasted_iota(jnp.int32, (chunk, blk), 1)
        s = jnp.where(k_pos <= q_pos, _dot_nt(qb_ref[r, :], kd) * SCALE_LOG2E, NEG)
        m = jnp.max(s, axis=1, keepdims=True)
        p = jnp.exp2(s - m)
        m_ref[r, :] = jnp.broadcast_to(m, (chunk, LANES))
        l_ref[r, :] = jnp.broadcast_to(jnp.sum(p, axis=1, keepdims=True), (chunk, LANES))
        acc_ref[r, :] = _dot(p.astype(BF16), vd)

    def body(jp, carry):
        ja = 2 * jp
        jb = jnp.minimum(ja + 1, qi - 1)
        bit_a = jnp.left_shift(jnp.int32(1), ja)
        bit_b = jnp.where(ja + 1 < qi, jnp.left_shift(jnp.int32(1), jb), 0)
        rows_a = pl.ds(pl.multiple_of(ja * blk, blk), blk)
        rows_b = pl.ds(pl.multiple_of(jb * blk, blk), blk)
        ka = k_ref[rows_a, :].astype(BF16)
        kb = k_ref[rows_b, :].astype(BF16)
        va = v_ref[rows_a, :].astype(BF16)
        vb = v_ref[rows_b, :].astype(BF16)

        def scores(c):
            qc = qb_ref[c * chunk:(c + 1) * chunk, :]
            return _dot_nt(qc, ka), _dot_nt(qc, kb)

        pending = [scores(c) for c in range(min(SCORE_LOOKAHEAD, n_chunks))]
        for c in range(n_chunks):
            r = slice(c * chunk, (c + 1) * chunk)
            sel = sel_ref[r, :]
            on_a = (sel & bit_a) != 0
            on_b = (sel & bit_b) != 0
            sa, sb = pending.pop(0)
            if c + SCORE_LOOKAHEAD < n_chunks:
                pending.append(scores(c + SCORE_LOOKAHEAD))
            parts = [jnp.where(on_a, sa[:, :LANES] * SCALE_LOG2E, NEG),
                     jnp.where(on_a, sa[:, LANES:] * SCALE_LOG2E, NEG),
                     jnp.where(on_b, sb[:, :LANES] * SCALE_LOG2E, NEG),
                     jnp.where(on_b, sb[:, LANES:] * SCALE_LOG2E, NEG)]
            m_prev = m_ref[r, :]
            top = jnp.maximum(jnp.maximum(parts[0], parts[1]), jnp.maximum(parts[2], parts[3]))
            m_new = jnp.maximum(m_prev, jnp.max(top, axis=1, keepdims=True))
            ps = [jnp.exp2(x - m_new) for x in parts]
            alpha = jnp.exp2(m_prev - m_new)
            l_ref[r, :] = alpha * l_ref[r, :] + jnp.sum((ps[0] + ps[1]) + (ps[2] + ps[3]), axis=1,
                                                        keepdims=True)
            pa = jnp.concatenate(ps[:2], axis=1).astype(BF16)
            pb = jnp.concatenate(ps[2:], axis=1).astype(BF16)
            acc_ref[r, :] = alpha * acc_ref[r, :] + (_dot(pa, va) + _dot(pb, vb))
            m_ref[r, :] = m_new
        return carry

    lax.fori_loop(0, (qi + 1) // 2, body, 0)
    for g in range(group):
        r = slice(g * blk, (g + 1) * blk)
        o_ref[:, g * HEAD:(g + 1) * HEAD] = (acc_ref[r, :] / l_ref[r, :]).astype(BF16)


def _moba_prompt(q, kv, seq, n_kv, chunk=PROMPT_ROW_CHUNK):
    heads = q.shape[1] // HEAD
    group = heads // n_kv
    n_blocks = seq // MOBA_BLOCK
    blk = MOBA_BLOCK
    rows = group * blk
    return pl.pallas_call(
        functools.partial(_moba_prompt_kernel, n_blocks=n_blocks, group=group, chunk=chunk),
        grid=(n_kv, n_blocks),
        in_specs=[pl.BlockSpec((blk, group * HEAD), lambda h, i: (i, h)),
                  pl.BlockSpec((seq, HEAD), lambda h, i: (0, h)),
                  pl.BlockSpec((seq, HEAD), lambda h, i: (0, n_kv + h))],
        out_specs=pl.BlockSpec((blk, group * HEAD), lambda h, i: (i, h)),
        out_shape=jax.ShapeDtypeStruct((q.shape[0], heads * HEAD), BF16),
        scratch_shapes=[pltpu.VMEM((LANES, HEAD), F32),
                        pltpu.VMEM((rows, HEAD), BF16),
                        pltpu.VMEM((rows, LANES), jnp.int32),
                        pltpu.VMEM((rows, LANES), F32),
                        pltpu.VMEM((rows, LANES), F32),
                        pltpu.VMEM((rows, HEAD), F32)],
        compiler_params=_params("arbitrary", "arbitrary"),
        name="moba_prompt",
    )(q, kv, kv)


def _moba_sample_kernel(pt_ref, q_ref, kn_ref, vn_ref, *refs, n_kv, group, steps, pages_per_block,
                        blocks_per_step):
    del pt_ref
    n_pages = pages_per_block * blocks_per_step
    kp = refs[:n_pages]
    vp = refs[n_pages:2 * n_pages]
    o_ref, m_s, l_s, g_s, o_s, bias_s = refs[2 * n_pages:]
    j = pl.program_id(1)
    rows = q_ref.shape[1]
    rq = group * steps
    tok_per_vreg = SUBLANES // n_kv
    n_col = bias_s.shape[1]

    @pl.when((pl.program_id(0) == 0) & (j == 0))
    def _():
        row_head = _div_mod(lax.broadcasted_iota(jnp.int32, (rows, n_col), 0), rq)[0]
        col_head = _div_mod(lax.broadcasted_iota(jnp.int32, (rows, n_col), 1), n_kv)[1]
        bias_s[...] = jnp.where(row_head == col_head, 0.0, NEG)

    qf = q_ref[0]
    q16 = qf.astype(BF16)
    head64 = _div_mod(lax.broadcasted_iota(jnp.int32, (rows, HEAD), 0), rq)[0]
    for t in range(blocks_per_step):
        pages = slice(t * pages_per_block, (t + 1) * pages_per_block)
        kb = jnp.concatenate([r[0] for r in kp[pages]], axis=0)
        vb = jnp.concatenate([r[0] for r in vp[pages]], axis=0)
        s = _dot_nt(q16, kb.astype(BF16)) * SCALE + bias_s[...]
        m = jnp.max(s, axis=1, keepdims=True)
        p = jnp.exp(s - m)

        parts = [kb[SUBLANES * i:SUBLANES * (i + 1), :] for i in range(n_col // SUBLANES)]
        while len(parts) > 1:
            parts = [a + b for a, b in zip(parts[::2], parts[1::2])]
        folded = parts[0]
        kmean = jnp.zeros((rows, HEAD), F32)
        for h in range(n_kv):
            total = folded[h:h + 1, :]
            for i in range(1, tok_per_vreg):
                total = total + folded[h + i * n_kv:h + i * n_kv + 1, :]
            kmean = jnp.where(head64 == h, total * (1.0 / MOBA_BLOCK), kmean)
        gate = jnp.sum(qf * kmean, axis=1, keepdims=True)

        n = j * blocks_per_step + t
        m_s[n] = jnp.broadcast_to(m, (rows, LANES))
        l_s[n] = jnp.broadcast_to(jnp.sum(p, axis=1, keepdims=True), (rows, LANES))
        g_s[n] = jnp.broadcast_to(gate, (rows, LANES))
        o_s[n] = _dot(p.astype(BF16), vb.astype(BF16))

    @pl.when(j == pl.num_programs(1) - 1)
    def _():
        gates = g_s[...]
        n_io = lax.broadcasted_iota(jnp.int32, gates.shape, 0)
        sel = jnp.zeros(gates.shape, jnp.bool_)
        for _ in range(min(MOBA_TOPK, g_s.shape[0])):
            mx = jnp.max(gates, axis=0, keepdims=True)
            idx = jnp.min(jnp.where(gates == mx, n_io, g_s.shape[0]), axis=0, keepdims=True)
            pick = n_io == idx
            sel = sel | pick
            gates = jnp.where(pick, -jnp.inf, gates)

        row = lax.broadcasted_iota(jnp.int32, (rows, LANES), 0)
        lane = lax.broadcasted_iota(jnp.int32, (rows, LANES), 1)
        q_head, q_step = _div_mod(row, rq)[0], _div_mod(row, steps)[1]
        k_step, k_head = _div_mod(lane, n_kv)
        ok = (q_head == k_head) & (k_step <= q_step) & (k_step < steps)
        s_own = jnp.where(ok, _dot_nt(q16, kn_ref[0].astype(BF16)) * SCALE, NEG)
        m_own = jnp.max(s_own, axis=1, keepdims=True)
        p_own = jnp.exp(s_own - m_own)
        l_own = jnp.sum(p_own, axis=1, keepdims=True)
        o_own = _dot(p_own.astype(BF16), vn_ref[0].astype(BF16))

        m_all = m_s[...]
        m_top = jnp.maximum(m_own, jnp.max(jnp.where(sel, m_all, -jnp.inf), axis=0))
        w = jnp.where(sel, jnp.exp(jnp.minimum(m_all - m_top[None], 0.0)), 0.0)
        w_own = jnp.exp(m_own - m_top)
        den = w_own * l_own + jnp.sum(w * l_s[...], axis=0)
        num = w_own * o_own + jnp.sum(w * o_s[...], axis=0)
        o_ref[0] = num / den


def _moba_sample(q, k_new, v_new, cache_k, cache_v, page_table, n_kv, steps):
    n_seq, n_pages = page_table.shape
    n_phys, page, _, _ = cache_k.shape
    ppb = MOBA_BLOCK // page
    n_blocks = n_pages // ppb
    bps = _unroll(n_blocks, SAMPLE_BLOCKS_PER_STEP)
    rows = q.shape[1]
    group = rows // (n_kv * steps)
    assert SUBLANES % n_kv == 0 and steps * n_kv <= LANES
    ck = cache_k.reshape(n_phys, page * n_kv, HEAD)
    cv = cache_v.reshape(n_phys, page * n_kv, HEAD)
    pages_per_step = ppb * bps

    def page_spec(p):
        return pl.BlockSpec((1, page * n_kv, HEAD),
                            lambda b, j, pt: (pt[b, j * pages_per_step + p], 0, 0))

    per_seq = lambda shape: pl.BlockSpec(shape, lambda b, j, pt: (b, 0, 0))
    grid_spec = pltpu.PrefetchScalarGridSpec(
        num_scalar_prefetch=1,
        grid=(n_seq, n_blocks // bps),
        in_specs=[per_seq((1, rows, HEAD)), per_seq((1, LANES, HEAD)), per_seq((1, LANES, HEAD))]
                 + [page_spec(p) for p in range(pages_per_step)] * 2,
        out_specs=per_seq((1, rows, HEAD)),
        scratch_shapes=[pltpu.VMEM((n_blocks, rows, LANES), F32) for _ in range(4)]
                       + [pltpu.VMEM((rows, MOBA_BLOCK * n_kv), F32)],
    )
    return pl.pallas_call(
        functools.partial(_moba_sample_kernel, n_kv=n_kv, group=group, steps=steps,
                          pages_per_block=ppb, blocks_per_step=bps),
        grid_spec=grid_spec,
        out_shape=jax.ShapeDtypeStruct((n_seq, rows, HEAD), F32),
        compiler_params=_params("arbitrary", "arbitrary"),
        name="moba_sample",
    )(page_table, q, k_new, v_new, *([ck] * pages_per_step), *([cv] * pages_per_step))


def kernel(x_prompt, x_sample, state_hgrn, cache_k, cache_v, page_table, norm_mix_a, w_in_a, lb_logits,
           onorm_a, w_out_a, norm_kv, w_kv, k_norm, norm_mix_b, w_q_b, q_norm, w_o_b, norm_ffn,
           w_gate_up, w_down):
    batch, seq, d = x_prompt.shape
    n_seq, steps, _ = x_sample.shape
    n_a = w_in_a.shape[0]
    depth = norm_ffn.shape[0]
    heads = d // HEAD
    n_kv = cache_k.shape[2]
    group = heads // n_kv
    kv_width = n_kv * HEAD
    n_dec = n_seq * steps
    past = page_table.shape[1] * cache_k.shape[1]
    assert batch == 1 and seq % MOBA_BLOCK == 0 and seq // MOBA_BLOCK <= 32
    assert past % MOBA_BLOCK == 0 and MOBA_BLOCK % cache_k.shape[1] == 0
    assert steps <= LANES and seq % n_dec == 0 and seq % GLA_CHUNK == 0

    h = jnp.concatenate([x_prompt.reshape(seq, d), x_sample.reshape(n_dec, d)], axis=0)
    states_p, states_s = [], []
    kv = None
    for layer in range(depth):
        if layer < n_a:
            z = _norm_matmul(h, norm_mix_a[layer], w_in_a, layer, name="hgrn_in")
            o, s_p = _gla_prompt(z, lb_logits, onorm_a[layer], layer, seq)
            o, s_s = _gla_sample(z, o, seq, n_seq, steps, state_hgrn, layer, lb_logits,
                                 onorm_a[layer], layer)
            h = _matmul_residual(o, w_out_a, layer, h, 512, "hgrn_out")
            states_p.append(s_p.reshape(1, heads, HEAD, HEAD).astype(state_hgrn.dtype))
            states_s.append(s_s)
        else:
            if kv is None:
                kv = _norm_matmul(h, norm_kv, w_kv, 0, head_gain=k_norm, norm_blocks=1,
                                  tn_cap=kv_width, name="kv_proj")
            jb = layer - n_a
            q = _norm_matmul(h, norm_mix_b[jb], w_q_b, jb, head_gain=q_norm[jb],
                             norm_blocks=heads, name="q_proj")
            att = _moba_prompt(q, kv, seq, n_kv)
            q_s = q[seq:].reshape(n_seq, steps, n_kv, group, HEAD).transpose(0, 2, 3, 1, 4)
            pad = ((0, 0), (0, LANES - steps * n_kv), (0, 0))
            k_s = jnp.pad(kv[seq:, :kv_width].reshape(n_seq, steps * n_kv, HEAD), pad)
            v_s = jnp.pad(kv[seq:, kv_width:].reshape(n_seq, steps * n_kv, HEAD), pad)
            att_s = _moba_sample(q_s.reshape(n_seq, n_kv * group * steps, HEAD), k_s, v_s,
                                 cache_k, cache_v, page_table, n_kv, steps)
            att_s = att_s.reshape(n_seq, n_kv, group, steps, HEAD).transpose(0, 3, 1, 2, 4)
            att = lax.dynamic_update_slice(att, att_s.reshape(n_dec, d).astype(BF16), (seq, 0))
            h = _matmul_residual(att, w_o_b, jb, h, 512, "attn_out")
        hf = _swiglu_up(h, norm_ffn[layer], w_gate_up, layer, "ffn_up")
        h = _matmul_residual(hf, w_down, layer, h, 256, "ffn_down")

    k_new = kv[:, :kv_width]
    v_new = kv[:, kv_width:]
    return (h[:seq].reshape(batch, seq, d),
            h[seq:].reshape(n_seq, steps, d),
            jnp.stack(states_p),
            jnp.stack(states_s),
            k_new[:seq].reshape(batch, seq, n_kv, HEAD),
            v_new[:seq].reshape(batch, seq, n_kv, HEAD),
            k_new[seq:].reshape(n_seq, steps, n_kv, HEAD),
            v_new[seq:].reshape(n_seq, steps, n_kv, HEAD))
```

```python
import functools

import jax
import jax.numpy as jnp
from jax import lax
from jax.experimental import pallas as pl
from jax.experimental.pallas import tpu as pltpu

F32 = jnp.float32
BF16 = jnp.bfloat16

EPS = 1e-6
HEAD = 128
GLA_CHUNK = 64
MOBA_BLOCK = 256
MOBA_TOPK = 3
SCALE = HEAD ** -0.5
SCALE_LOG2E = SCALE * 1.4426950408889634
NEG = -1e30
LANES = 128
SUBLANES = 8
NORM_ROWS = 16
NORM_UNROLL = 5
ROW_TILE_CAP = 2080
FFN_ROW_TILE_CAP = 1040
PROMPT_ROW_CHUNK = 128
SCORE_LOOKAHEAD = 2
SAMPLE_BLOCKS_PER_STEP = 4
SAMPLE_SEQ_UNROLL = 4
VMEM_LIMIT = 56 * 1024 * 1024


def _dot(a, b):
    return jnp.dot(a, b, preferred_element_type=F32)


def _dot_nt(a, b, precision=None):
    return lax.dot_general(a, b, (((1,), (1,)), ((), ())), precision=precision,
                           preferred_element_type=F32)


def _dot_tn(a, b):
    return lax.dot_general(a, b, (((0,), (0,)), ((), ())), preferred_element_type=F32)


def _div_mod(x, n):
    if n & (n - 1) == 0:
        return jnp.right_shift(x, n.bit_length() - 1), x & (n - 1)
    q = x // n
    return q, x - q * n


def _sigmoid(x):
    return 1.0 / (1.0 + jnp.exp(-x))


def _rms(x, g):
    return x * lax.rsqrt(jnp.mean(x * x, axis=-1, keepdims=True) + EPS) * g


def _row_tile(m, cap=ROW_TILE_CAP):
    for t in range(min(m, cap), 0, -1):
        if m % t == 0 and t % NORM_ROWS == 0:
            return t
    raise ValueError(f"no row tile for {m} rows")


def _col_tile(n, cap):
    for t in range(min(n, cap), 0, -1):
        if n % t == 0 and t % LANES == 0:
            return t
    raise ValueError(f"no column tile for {n} columns")


def _unroll(trips, cap):
    for u in range(min(trips, cap), 0, -1):
        if trips % u == 0:
            return u
    return 1


def _params(*sem):
    return pltpu.CompilerParams(dimension_semantics=sem, vmem_limit_bytes=VMEM_LIMIT)


def _normalise_tile(x_ref, g_ref, xn_ref):
    g = g_ref[...]
    trips = x_ref.shape[0] // NORM_ROWS

    def body(i, carry):
        rows = pl.ds(pl.multiple_of(i * NORM_ROWS, NORM_ROWS), NORM_ROWS)
        xn_ref[rows, :] = _rms(x_ref[rows, :], g).astype(BF16)
        return carry

    lax.fori_loop(0, trips, body, 0, unroll=_unroll(trips, NORM_UNROLL))


def _norm_matmul_kernel(x_ref, g_ref, w_ref, hg_ref, o_ref, xn_ref, *, norm_blocks):
    j = pl.program_id(1)

    @pl.when(j == 0)
    def _():
        _normalise_tile(x_ref, g_ref, xn_ref)

    acc = _dot(xn_ref[...], w_ref[...].astype(BF16))
    if norm_blocks == 0:
        o_ref[...] = acc
        return

    @pl.when(j < norm_blocks)
    def _():
        for h in range(acc.shape[1] // HEAD):
            cols = slice(h * HEAD, (h + 1) * HEAD)
            o_ref[:, cols] = _rms(acc[:, cols], hg_ref[...])

    @pl.when(j >= norm_blocks)
    def _():
        o_ref[...] = acc


def _norm_matmul(x, g, w, layer, head_gain=None, norm_blocks=0, tn_cap=512, name="norm_matmul"):
    m, d = x.shape
    n = w.shape[-1]
    tm, tn = _row_tile(m), _col_tile(n, tn_cap)
    if head_gain is None:
        head_gain = jnp.ones((HEAD,), F32)
    if w.ndim == 2:
        w = w[None]
    return pl.pallas_call(
        functools.partial(_norm_matmul_kernel, norm_blocks=norm_blocks),
        grid=(m // tm, n // tn),
        in_specs=[
            pl.BlockSpec((tm, d), lambda i, j: (i, 0), pipeline_mode=pl.Buffered(1)),
            pl.BlockSpec((1, d), lambda i, j: (0, 0)),
            pl.BlockSpec((None, d, tn), lambda i, j: (layer, 0, j)),
            pl.BlockSpec((1, HEAD), lambda i, j: (0, 0)),
        ],
        out_specs=pl.BlockSpec((tm, tn), lambda i, j: (i, j)),
        out_shape=jax.ShapeDtypeStruct((m, n), F32),
        scratch_shapes=[pltpu.VMEM((tm, d), BF16)],
        compiler_params=_params("parallel", "arbitrary"),
        name=name,
    )(x, g.reshape(1, d), w, head_gain.reshape(1, HEAD))


def _swiglu_up_kernel(x_ref, g_ref, wa_ref, wu_ref, o_ref, xn_ref):
    @pl.when(pl.program_id(1) == 0)
    def _():
        _normalise_tile(x_ref, g_ref, xn_ref)

    xn = xn_ref[...]
    a = _dot(xn, wa_ref[...].astype(BF16))
    u = _dot(xn, wu_ref[...].astype(BF16))
    o_ref[...] = (a * _sigmoid(a) * u).astype(BF16)


def _swiglu_up(x, g, w_gu, layer, name):
    m, d = x.shape
    d_ff = w_gu.shape[-1] // 2
    tm, tn = _row_tile(m, FFN_ROW_TILE_CAP), _col_tile(d_ff, 512)
    nblk = d_ff // tn
    return pl.pallas_call(
        _swiglu_up_kernel,
        grid=(m // tm, nblk),
        in_specs=[
            pl.BlockSpec((tm, d), lambda i, j: (i, 0)),
            pl.BlockSpec((1, d), lambda i, j: (0, 0)),
            pl.BlockSpec((None, d, tn), lambda i, j: (layer, 0, j)),
            pl.BlockSpec((None, d, tn), lambda i, j: (layer, 0, j + nblk)),
        ],
        out_specs=pl.BlockSpec((tm, tn), lambda i, j: (i, j)),
        out_shape=jax.ShapeDtypeStruct((m, d_ff), BF16),
        scratch_shapes=[pltpu.VMEM((tm, d), BF16)],
        compiler_params=_params("parallel", "arbitrary"),
        name=name,
    )(x, g.reshape(1, d), w_gu, w_gu)


def _matmul_residual_kernel(a_ref, w_ref, r_ref, o_ref):
    o_ref[...] = r_ref[...] + _dot(a_ref[...], w_ref[...].astype(BF16))


def _matmul_residual(a, w, layer, res, tn_cap, name, tm_cap=ROW_TILE_CAP):
    m, k = a.shape
    n = w.shape[-1]
    tm, tn = _row_tile(m, tm_cap), _col_tile(n, tn_cap)
    return pl.pallas_call(
        _matmul_residual_kernel,
        grid=(m // tm, n // tn),
        in_specs=[
            pl.BlockSpec((tm, k), lambda i, j: (i, 0)),
            pl.BlockSpec((None, k, tn), lambda i, j: (layer, 0, j)),
            pl.BlockSpec((tm, tn), lambda i, j: (i, j)),
        ],
        out_specs=pl.BlockSpec((tm, tn), lambda i, j: (i, j)),
        out_shape=jax.ShapeDtypeStruct((m, n), F32),
        compiler_params=_params("parallel", "arbitrary"),
        name=name,
    )(a, w, res)


def _lower_bound(logits, layer):
    e = jnp.exp(logits - jnp.max(logits, axis=0, keepdims=True))
    return jnp.sum(e[:layer + 1], axis=0, keepdims=True) / jnp.sum(e, axis=0, keepdims=True)


def _gate_inputs(zq, zf, lb):
    f = lb + (1.0 - lb) * _sigmoid(zf)
    return zq * _sigmoid(zq), 1.0 - f, jnp.log(f)


def _gated_output(o, zg, gain):
    return _rms(o, gain) * (zg * _sigmoid(zg))


def _cumsum_rows(g, seg=None):
    n = g.shape[0]
    row = lax.broadcasted_iota(jnp.int32, g.shape, 0)
    pos = row if seg is None else _div_mod(row, seg)[1]
    span = n if seg is None else seg
    s = 1
    while s < span:
        g = g + jnp.where(pos >= s, pltpu.roll(g, s, 0), 0.0)
        s *= 2
    return g


def _gla_prompt_kernel(zq_ref, zf_ref, zi_ref, zg_ref, lbl_ref, gain_ref, o_ref, s_ref, st_ref,
                       *, layer):
    t = pl.program_id(1)
    c = GLA_CHUNK
    mid = (c - 1) // 2

    @pl.when(t == 0)
    def _():
        st_ref[...] = jnp.zeros_like(st_ref)

    lb = _lower_bound(lbl_ref[...], layer)
    gain = gain_ref[...]
    r2 = lax.broadcasted_iota(jnp.int32, (c, c), 0)
    c2 = lax.broadcasted_iota(jnp.int32, (c, c), 1)
    st = st_ref[...]
    for i in range(o_ref.shape[0] // c):
        rows = slice(i * c, (i + 1) * c)
        q, k, g = _gate_inputs(zq_ref[rows, :], zf_ref[rows, :], lb)
        v16 = zi_ref[rows, :].astype(BF16)
        cum = _cumsum_rows(g)
        ref = cum[mid:mid + 1, :]
        last = cum[c - 1:c, :]
        att = _dot_nt((q * jnp.exp(cum - ref)).astype(BF16), (k * jnp.exp(ref - cum)).astype(BF16))
        att = jnp.where(r2 >= c2, att, 0.0)
        o = _dot(att.astype(BF16), v16) + _dot_nt((q * jnp.exp(cum)).astype(BF16), st.astype(BF16))
        k_out = (k * jnp.exp(last - cum)).astype(BF16)
        st = jnp.exp(last) * st + _dot_tn(v16, k_out)
        o_ref[rows, :] = _gated_output(o, zg_ref[rows, :], gain).astype(BF16)
    st_ref[...] = st

    @pl.when(t == pl.num_programs(1) - 1)
    def _():
        s_ref[...] = st.T


def _gla_prompt(z, lb_logits, gain, layer, seq, tl=512):
    heads = z.shape[1] // (4 * HEAD)
    tl = min(tl, seq)
    slots = lb_logits.shape[0]
    zspec = lambda off: pl.BlockSpec((tl, HEAD), lambda h, t: (t, h + off * heads))
    return pl.pallas_call(
        functools.partial(_gla_prompt_kernel, layer=layer),
        grid=(heads, seq // tl),
        in_specs=[zspec(0), zspec(1), zspec(2), zspec(3),
                  pl.BlockSpec((slots, HEAD), lambda h, t: (0, h)),
                  pl.BlockSpec((1, HEAD), lambda h, t: (0, 0))],
        out_specs=[pl.BlockSpec((tl, HEAD), lambda h, t: (t, h)),
                   pl.BlockSpec((None, HEAD, HEAD), lambda h, t: (h, 0, 0))],
        out_shape=[jax.ShapeDtypeStruct((z.shape[0], heads * HEAD), BF16),
                   jax.ShapeDtypeStruct((heads, HEAD, HEAD), F32)],
        scratch_shapes=[pltpu.VMEM((HEAD, HEAD), F32)],
        compiler_params=_params("parallel", "arbitrary"),
        name="gla_prompt",
    )(z, z, z, z, lb_logits, gain.reshape(1, HEAD))


def _gla_sample_kernel(zq_ref, zf_ref, zi_ref, zg_ref, lbl_ref, gain_ref, s0_ref, o_in_ref,
                       o_ref, s_ref, last_ref, kout_ref, qin_ref, v_ref, *, layer, steps):
    del o_in_ref
    rows = zq_ref.shape[0]
    mid = (steps - 1) // 2
    lb = _lower_bound(lbl_ref[...], layer)
    q, k, g = _gate_inputs(zq_ref[...], zf_ref[...], lb)
    v16 = zi_ref[...].astype(BF16)
    cum = _cumsum_rows(g, seg=steps)

    row = lax.broadcasted_iota(jnp.int32, cum.shape, 0)
    pos = _div_mod(row, steps)[1]

    def spread(src_pos):
        picked = jnp.where(pos == src_pos, cum, 0.0)
        out = picked
        for d in range(steps):
            if d != src_pos:
                out = out + pltpu.roll(picked, (d - src_pos) % rows, 0)
        return out

    ref = spread(mid)
    last = spread(steps - 1)
    r2 = lax.broadcasted_iota(jnp.int32, (rows, rows), 0)
    c2 = lax.broadcasted_iota(jnp.int32, (rows, rows), 1)
    att = _dot_nt((q * jnp.exp(cum - ref)).astype(BF16), (k * jnp.exp(ref - cum)).astype(BF16))
    att = jnp.where((r2 >= c2) & (_div_mod(r2, steps)[0] == _div_mod(c2, steps)[0]), att, 0.0)
    o_intra = _dot(att.astype(BF16), v16)
    qin_ref[...] = (q * jnp.exp(cum)).astype(BF16)
    v_ref[...] = v16
    last_ref[...] = last
    kout_ref[...] = k * jnp.exp(last - cum)
    n_seq = s0_ref.shape[0]

    def body(b, o_inter):
        lo = b * steps
        mine = (row >= lo) & (row < lo + steps)
        st = s0_ref[b].T
        o_inter = o_inter + jnp.where(mine, _dot_nt(qin_ref[...], st.astype(BF16)), 0.0)
        k_out = jnp.where(mine, kout_ref[...], 0.0).astype(BF16)
        decay = jnp.exp(last_ref[pl.ds(lo, 1), :])
        s_ref[b] = (decay * st + _dot_tn(v_ref[...], k_out)).T
        return o_inter

    o_inter = lax.fori_loop(0, n_seq, body, jnp.zeros(cum.shape, F32),
                            unroll=_unroll(n_seq, SAMPLE_SEQ_UNROLL))
    o_ref[...] = _gated_output(o_intra + o_inter, zg_ref[...], gain_ref[...]).astype(BF16)


def _gla_sample(z, o_all, row0, n_seq, steps, s0, layer_s0, lb_logits, gain, layer):
    heads = z.shape[1] // (4 * HEAD)
    rows = n_seq * steps
    slots = lb_logits.shape[0]
    rb = row0 // rows
    zspec = lambda off: pl.BlockSpec((rows, HEAD), lambda h: (rb, h + off * heads))
    s_in = pl.BlockSpec((None, n_seq, None, HEAD, HEAD), lambda h: (layer_s0, 0, h, 0, 0))
    s_out = pl.BlockSpec((n_seq, None, HEAD, HEAD), lambda h: (0, h, 0, 0))
    return pl.pallas_call(
        functools.partial(_gla_sample_kernel, layer=layer, steps=steps),
        grid=(heads,),
        in_specs=[zspec(0), zspec(1), zspec(2), zspec(3),
                  pl.BlockSpec((slots, HEAD), lambda h: (0, h)),
                  pl.BlockSpec((1, HEAD), lambda h: (0, 0)),
                  s_in,
                  pl.BlockSpec(memory_space=pl.ANY)],
        out_specs=[pl.BlockSpec((rows, HEAD), lambda h: (rb, h)), s_out],
        out_shape=[jax.ShapeDtypeStruct(o_all.shape, o_all.dtype),
                   jax.ShapeDtypeStruct(s0.shape[1:], s0.dtype)],
        scratch_shapes=[pltpu.VMEM((rows, HEAD), F32), pltpu.VMEM((rows, HEAD), F32),
                        pltpu.VMEM((rows, HEAD), BF16), pltpu.VMEM((rows, HEAD), BF16)],
        input_output_aliases={7: 0},
        compiler_params=_params("parallel"),
        name="gla_sample",
    )(z, z, z, z, lb_logits, gain.reshape(1, HEAD), s0, o_all)


def _top_blocks_bits(gate_t, blk_f, n_valid):
    gate_t = jnp.where(blk_f < n_valid, gate_t, -jnp.inf)
    bits = jnp.zeros((1, gate_t.shape[1]), jnp.int32)
    for _ in range(MOBA_TOPK):
        mx = jnp.max(gate_t, axis=0, keepdims=True)
        idx = jnp.min(jnp.where(gate_t == mx, blk_f, float(LANES)), axis=0, keepdims=True)
        live = mx > -jnp.inf
        gate_t = jnp.where(blk_f == idx, -jnp.inf, gate_t)
        shift = jnp.minimum(idx, 31.0).astype(jnp.int32)
        bits = bits | jnp.where(live, jnp.left_shift(1, shift), 0)
    return bits


def _moba_prompt_kernel(q_ref, k_ref, v_ref, o_ref, kmean_ref, qb_ref, sel_ref, m_ref, l_ref,
                        acc_ref, *, n_blocks, group, chunk):
    qi = pl.program_id(1)
    blk = MOBA_BLOCK
    rows = group * blk
    n_chunks = rows // chunk

    @pl.when(qi == 0)
    def _():
        kmean_ref[...] = jnp.zeros_like(kmean_ref)
        for n in range(n_blocks):
            kmean_ref[n:n + 1, :] = jnp.mean(k_ref[n * blk:(n + 1) * blk, :], axis=0, keepdims=True)

    blk_f = lax.broadcasted_iota(jnp.int32, (kmean_ref.shape[0], blk), 0).astype(F32)
    qi_f = qi.astype(F32)
    for g in range(group):
        qg = q_ref[:, g * HEAD:(g + 1) * HEAD]
        qb_ref[g * blk:(g + 1) * blk, :] = (qg * SCALE_LOG2E).astype(BF16)
        gate_t = _dot_nt(kmean_ref[...], qg, precision=lax.Precision.HIGHEST)
        bits = _top_blocks_bits(gate_t, blk_f, qi_f)
        per_row = jnp.broadcast_to(bits, (SUBLANES, blk)).T
        sel_ref[g * blk:(g + 1) * blk, :] = jnp.broadcast_to(per_row[:, :1], (blk, LANES))

    own = pl.ds(pl.multiple_of(qi * blk, blk), blk)
    kd = k_ref[own, :].astype(BF16)
    vd = v_ref[own, :].astype(BF16)
    for c in range(n_chunks):
        r = slice(c * chunk, (c + 1) * chunk)
        q_pos = lax.broadcasted_iota(jnp.int32, (chunk, blk), 0) + (c * chunk) % blk
        k_pos = lax.broadcasted_iota(jnp.int32, (chunk, blk), 1)
        s = jnp.where(k_pos <= q_pos, _dot_nt(qb_ref[r, :], kd), NEG)
        m = jnp.max(s, axis=1, keepdims=True)
        p = jnp.exp2(s - m)
        m_ref[r, :] = jnp.broadcast_to(m, (chunk, LANES))
        l_ref[r, :] = jnp.broadcast_to(jnp.sum(p, axis=1, keepdims=True), (chunk, LANES))
        acc_ref[r, :] = _dot(p.astype(BF16), vd)

    def body(jp, carry):
        ja = 2 * jp
        jb = jnp.minimum(ja + 1, qi - 1)
        bit_a = jnp.left_shift(jnp.int32(1), ja)
        bit_b = jnp.where(ja + 1 < qi, jnp.left_shift(jnp.int32(1), jb), 0)
        rows_a = pl.ds(pl.multiple_of(ja * blk, blk), blk)
        rows_b = pl.ds(pl.multiple_of(jb * blk, blk), blk)
        ka = k_ref[rows_a, :].astype(BF16)
        kb = k_ref[rows_b, :].astype(BF16)
        va = v_ref[rows_a, :].astype(BF16)
        vb = v_ref[rows_b, :].astype(BF16)

        def scores(c):
            qc = qb_ref[c * chunk:(c + 1) * chunk, :]
            return _dot_nt(qc, ka), _dot_nt(qc, kb)

        pending = [scores(c) for c in range(min(SCORE_LOOKAHEAD, n_chunks))]
        for c in range(n_chunks):
            r = slice(c * chunk, (c + 1) * chunk)
            sel = sel_ref[r, :]
            on_a = (sel & bit_a) != 0
            on_b = (sel & bit_b) != 0
            sa, sb = pending.pop(0)
            if c + SCORE_LOOKAHEAD < n_chunks:
                pending.append(scores(c + SCORE_LOOKAHEAD))
            parts = [jnp.where(on_a, sa[:, :LANES], NEG), jnp.where(on_a, sa[:, LANES:], NEG),
                     jnp.where(on_b, sb[:, :LANES], NEG), jnp.where(on_b, sb[:, LANES:], NEG)]
            m_prev = m_ref[r, :]
            top = jnp.maximum(jnp.maximum(parts[0], parts[1]), jnp.maximum(parts[2], parts[3]))
            m_new = jnp.maximum(m_prev, jnp.max(top, axis=1, keepdims=True))
            ps = [jnp.exp2(x - m_new) for x in parts]
            alpha = jnp.exp2(m_prev - m_new)
            l_ref[r, :] = alpha * l_ref[r, :] + jnp.sum((ps[0] + ps[1]) + (ps[2] + ps[3]), axis=1,
                                                        keepdims=True)
            pa = jnp.concatenate(ps[:2], axis=1).astype(BF16)
            pb = jnp.concatenate(ps[2:], axis=1).astype(BF16)
            acc_ref[r, :] = alpha * acc_ref[r, :] + (_dot(pa, va) + _dot(pb, vb))
            m_ref[r, :] = m_new
        return carry

    lax.fori_loop(0, (qi + 1) // 2, body, 0)
    for g in range(group):
        r = slice(g * blk, (g + 1) * blk)
        o_ref[:, g * HEAD:(g + 1) * HEAD] = (acc_ref[r, :] / l_ref[r, :]).astype(BF16)


def _moba_prompt(q, kv, seq, n_kv, chunk=PROMPT_ROW_CHUNK):
    heads = q.shape[1] // HEAD
    group = heads // n_kv
    n_blocks = seq // MOBA_BLOCK
    blk = MOBA_BLOCK
    rows = group * blk
    return pl.pallas_call(
        functools.partial(_moba_prompt_kernel, n_blocks=n_blocks, group=group, chunk=chunk),
        grid=(n_kv, n_blocks),
        in_specs=[pl.BlockSpec((blk, group * HEAD), lambda h, i: (i, h)),
                  pl.BlockSpec((seq, HEAD), lambda h, i: (0, h)),
                  pl.BlockSpec((seq, HEAD), lambda h, i: (0, n_kv + h))],
        out_specs=pl.BlockSpec((blk, group * HEAD), lambda h, i: (i, h)),
        out_shape=jax.ShapeDtypeStruct((q.shape[0], heads * HEAD), BF16),
        scratch_shapes=[pltpu.VMEM((-(-n_blocks // SUBLANES) * SUBLANES, HEAD), F32),
                        pltpu.VMEM((rows, HEAD), BF16),
                        pltpu.VMEM((rows, LANES), jnp.int32),
                        pltpu.VMEM((rows, LANES), F32),
                        pltpu.VMEM((rows, LANES), F32),
                        pltpu.VMEM((rows, HEAD), F32)],
        compiler_params=_params("arbitrary", "arbitrary"),
        name="moba_prompt",
    )(q, kv, kv)


def _moba_sample_kernel(pt_ref, q_ref, kn_ref, vn_ref, *refs, n_kv, group, steps, pages_per_block,
                        blocks_per_step):
    del pt_ref
    n_pages = pages_per_block * blocks_per_step
    kp = refs[:n_pages]
    vp = refs[n_pages:2 * n_pages]
    o_ref, m_s, l_s, g_s, o_s, bias_s = refs[2 * n_pages:]
    j = pl.program_id(1)
    rows = q_ref.shape[1]
    rq = group * steps
    tok_per_vreg = SUBLANES // n_kv
    n_col = bias_s.shape[1]

    @pl.when((pl.program_id(0) == 0) & (j == 0))
    def _():
        row_head = _div_mod(lax.broadcasted_iota(jnp.int32, (rows, n_col), 0), rq)[0]
        col_head = _div_mod(lax.broadcasted_iota(jnp.int32, (rows, n_col), 1), n_kv)[1]
        bias_s[...] = jnp.where(row_head == col_head, 0.0, NEG)

    qf = q_ref[0]
    q16 = qf.astype(BF16)
    head64 = _div_mod(lax.broadcasted_iota(jnp.int32, (rows, HEAD), 0), rq)[0]
    kbs, vbs, scores = [], [], []
    for t in range(blocks_per_step):
        pages = slice(t * pages_per_block, (t + 1) * pages_per_block)
        kbs.append(jnp.concatenate([r[0] for r in kp[pages]], axis=0))
        vbs.append(jnp.concatenate([r[0] for r in vp[pages]], axis=0))
        scores.append(_dot_nt(q16, kbs[t].astype(BF16)))
    for t in range(blocks_per_step):
        kb = kbs[t]
        s = scores[t] * SCALE + bias_s[...]
        m = jnp.max(s, axis=1, keepdims=True)
        p = jnp.exp(s - m)

        parts = [kb[SUBLANES * i:SUBLANES * (i + 1), :] for i in range(n_col // SUBLANES)]
        while len(parts) > 1:
            parts = [a + b for a, b in zip(parts[::2], parts[1::2])]
        folded = parts[0]
        kmean = jnp.zeros((rows, HEAD), F32)
        for h in range(n_kv):
            total = folded[h:h + 1, :]
            for i in range(1, tok_per_vreg):
                total = total + folded[h + i * n_kv:h + i * n_kv + 1, :]
            kmean = jnp.where(head64 == h, total * (1.0 / MOBA_BLOCK), kmean)
        gate = jnp.sum(qf * kmean, axis=1, keepdims=True)

        n = j * blocks_per_step + t
        m_s[n] = jnp.broadcast_to(m, (rows, LANES))
        l_s[n] = jnp.broadcast_to(jnp.sum(p, axis=1, keepdims=True), (rows, LANES))
        g_s[n] = jnp.broadcast_to(gate, (rows, LANES))
        o_s[n] = _dot(p.astype(BF16), vbs[t].astype(BF16))

    @pl.when(j == pl.num_programs(1) - 1)
    def _():
        gates = g_s[...]
        n_io = lax.broadcasted_iota(jnp.int32, gates.shape, 0)
        sel = jnp.zeros(gates.shape, jnp.bool_)
        for _ in range(min(MOBA_TOPK, g_s.shape[0])):
            mx = jnp.max(gates, axis=0, keepdims=True)
            idx = jnp.min(jnp.where(gates == mx, n_io, g_s.shape[0]), axis=0, keepdims=True)
            pick = n_io == idx
            sel = sel | pick
            gates = jnp.where(pick, -jnp.inf, gates)

        row = lax.broadcasted_iota(jnp.int32, (rows, LANES), 0)
        lane = lax.broadcasted_iota(jnp.int32, (rows, LANES), 1)
        q_head, q_step = _div_mod(row, rq)[0], _div_mod(row, steps)[1]
        k_step, k_head = _div_mod(lane, n_kv)
        ok = (q_head == k_head) & (k_step <= q_step) & (k_step < steps)
        s_own = jnp.where(ok, _dot_nt(q16, kn_ref[0].astype(BF16)) * SCALE, NEG)
        m_own = jnp.max(s_own, axis=1, keepdims=True)
        p_own = jnp.exp(s_own - m_own)
        l_own = jnp.sum(p_own, axis=1, keepdims=True)
        o_own = _dot(p_own.astype(BF16), vn_ref[0].astype(BF16))

        m_all = m_s[...]
        m_top = jnp.maximum(m_own, jnp.max(jnp.where(sel, m_all, -jnp.inf), axis=0))
        w = jnp.where(sel, jnp.exp(jnp.minimum(m_all - m_top[None], 0.0)), 0.0)
        w_own = jnp.exp(m_own - m_top)
        den = w_own * l_own + jnp.sum(w * l_s[...], axis=0)
        num = w_own * o_own + jnp.sum(w * o_s[...], axis=0)
        o_ref[0] = num / den


def _moba_sample(q, k_new, v_new, cache_k, cache_v, page_table, n_kv, steps):
    n_seq, n_pages = page_table.shape
    n_phys, page, _, _ = cache_k.shape
    ppb = MOBA_BLOCK // page
    n_blocks = n_pages // ppb
    bps = _unroll(n_blocks, SAMPLE_BLOCKS_PER_STEP)
    rows = q.shape[1]
    group = rows // (n_kv * steps)
    assert SUBLANES % n_kv == 0 and steps * n_kv <= LANES
    ck = cache_k.reshape(n_phys, page * n_kv, HEAD)
    cv = cache_v.reshape(n_phys, page * n_kv, HEAD)
    pages_per_step = ppb * bps

    def page_spec(p):
        return pl.BlockSpec((1, page * n_kv, HEAD),
                            lambda b, j, pt: (pt[b, j * pages_per_step + p], 0, 0))

    per_seq = lambda shape: pl.BlockSpec(shape, lambda b, j, pt: (b, 0, 0))
    grid_spec = pltpu.PrefetchScalarGridSpec(
        num_scalar_prefetch=1,
        grid=(n_seq, n_blocks // bps),
        in_specs=[per_seq((1, rows, HEAD)), per_seq((1, LANES, HEAD)), per_seq((1, LANES, HEAD))]
                 + [page_spec(p) for p in range(pages_per_step)] * 2,
        out_specs=per_seq((1, rows, HEAD)),
        scratch_shapes=[pltpu.VMEM((n_blocks, rows, LANES), F32) for _ in range(4)]
                       + [pltpu.VMEM((rows, MOBA_BLOCK * n_kv), F32)],
    )
    return pl.pallas_call(
        functools.partial(_moba_sample_kernel, n_kv=n_kv, group=group, steps=steps,
                          pages_per_block=ppb, blocks_per_step=bps),
        grid_spec=grid_spec,
        out_shape=jax.ShapeDtypeStruct((n_seq, rows, HEAD), F32),
        compiler_params=_params("arbitrary", "arbitrary"),
        name="moba_sample",
    )(page_table, q, k_new, v_new, *([ck] * pages_per_step), *([cv] * pages_per_step))


def kernel(x_prompt, x_sample, state_hgrn, cache_k, cache_v, page_table, norm_mix_a, w_in_a, lb_logits,
           onorm_a, w_out_a, norm_kv, w_kv, k_norm, norm_mix_b, w_q_b, q_norm, w_o_b, norm_ffn,
           w_gate_up, w_down):
    batch, seq, d = x_prompt.shape
    n_seq, steps, _ = x_sample.shape
    n_a = w_in_a.shape[0]
    depth = norm_ffn.shape[0]
    heads = d // HEAD
    n_kv = cache_k.shape[2]
    group = heads // n_kv
    kv_width = n_kv * HEAD
    n_dec = n_seq * steps
    past = page_table.shape[1] * cache_k.shape[1]
    assert batch == 1 and seq % MOBA_BLOCK == 0 and seq // MOBA_BLOCK <= 32
    assert past % MOBA_BLOCK == 0 and MOBA_BLOCK % cache_k.shape[1] == 0
    assert steps <= LANES and seq % n_dec == 0 and seq % GLA_CHUNK == 0

    h = jnp.concatenate([x_prompt.reshape(seq, d), x_sample.reshape(n_dec, d)], axis=0)
    states_p, states_s = [], []
    kv = None
    for layer in range(depth):
        if layer < n_a:
            z = _norm_matmul(h, norm_mix_a[layer], w_in_a, layer, name="hgrn_in")
            o, s_p = _gla_prompt(z, lb_logits, onorm_a[layer], layer, seq)
            o, s_s = _gla_sample(z, o, seq, n_seq, steps, state_hgrn, layer, lb_logits,
                                 onorm_a[layer], layer)
            h = _matmul_residual(o, w_out_a, layer, h, 512, "hgrn_out")
            states_p.append(s_p.reshape(1, heads, HEAD, HEAD).astype(state_hgrn.dtype))
            states_s.append(s_s)
        else:
            if kv is None:
                kv = _norm_matmul(h, norm_kv, w_kv, 0, head_gain=k_norm, norm_blocks=1,
                                  tn_cap=kv_width, name="kv_proj")
            jb = layer - n_a
            q = _norm_matmul(h, norm_mix_b[jb], w_q_b, jb, head_gain=q_norm[jb],
                             norm_blocks=heads, name="q_proj")
            att = _moba_prompt(q, kv, seq, n_kv)
            q_s = q[seq:].reshape(n_seq, steps, n_kv, group, HEAD).transpose(0, 2, 3, 1, 4)
            pad = ((0, 0), (0, LANES - steps * n_kv), (0, 0))
            k_s = jnp.pad(kv[seq:, :kv_width].reshape(n_seq, steps * n_kv, HEAD), pad)
            v_s = jnp.pad(kv[seq:, kv_width:].reshape(n_seq, steps * n_kv, HEAD), pad)
            att_s = _moba_sample(q_s.reshape(n_seq, n_kv * group * steps, HEAD), k_s, v_s,
                                 cache_k, cache_v, page_table, n_kv, steps)
            att_s = att_s.reshape(n_seq, n_kv, group, steps, HEAD).transpose(0, 3, 1, 2, 4)
            att = lax.dynamic_update_slice(att, att_s.reshape(n_dec, d).astype(BF16), (seq, 0))
            h = _matmul_residual(att, w_o_b, jb, h, 512, "attn_out")
        hf = _swiglu_up(h, norm_ffn[layer], w_gate_up, layer, "ffn_up")
        h = _matmul_residual(hf, w_down, layer, h, 256, "ffn_down", tm_cap=FFN_ROW_TILE_CAP)

    k_new = kv[:, :kv_width]
    v_new = kv[:, kv_width:]
    return (h[:seq].reshape(batch, seq, d),
            h[seq:].reshape(n_seq, steps, d),
            jnp.stack(states_p),
            jnp.stack(states_s),
            k_new[:seq].reshape(batch, seq, n_kv, HEAD),
            v_new[:seq].reshape(batch, seq, n_kv, HEAD),
            k_new[seq:].reshape(n_seq, steps, n_kv, HEAD),
            v_new[seq:].reshape(n_seq, steps, n_kv, HEAD))
```

```python
import functools

import jax
import jax.numpy as jnp
from jax import lax
from jax.experimental import pallas as pl
from jax.experimental.pallas import tpu as pltpu

F32 = jnp.float32
BF16 = jnp.bfloat16

EPS = 1e-6
HEAD = 128
GLA_CHUNK = 64
MOBA_BLOCK = 256
MOBA_TOPK = 3
SCALE = HEAD ** -0.5
SCALE_LOG2E = SCALE * 1.4426950408889634
NEG = -1e30
LANES = 128
SUBLANES = 8
GLA_HEADS_PER_STEP = 4
NORM_ROWS = 16
NORM_UNROLL = 5
ROW_TILE_CAP = 2080
FFN_ROW_TILE_CAP = 1040
PROMPT_ROW_CHUNK = 256
PROMPT_OWN_CHUNK = 128
SCORE_LOOKAHEAD = 2
SAMPLE_BLOCKS_PER_STEP = 8
SAMPLE_SEQ_UNROLL = 4
VMEM_LIMIT = 56 * 1024 * 1024


def _dot(a, b):
    return jnp.dot(a, b, preferred_element_type=F32)


def _dot_nt(a, b, precision=None):
    return lax.dot_general(a, b, (((1,), (1,)), ((), ())), precision=precision,
                           preferred_element_type=F32)


def _dot_tn(a, b):
    return lax.dot_general(a, b, (((0,), (0,)), ((), ())), preferred_element_type=F32)


def _div_mod(x, n):
    if n & (n - 1) == 0:
        return jnp.right_shift(x, n.bit_length() - 1), x & (n - 1)
    q = x // n
    return q, x - q * n


def _sigmoid(x):
    return 1.0 / (1.0 + jnp.exp(-x))


def _rms(x, g):
    return x * lax.rsqrt(jnp.mean(x * x, axis=-1, keepdims=True) + EPS) * g


def _row_tile(m, cap=ROW_TILE_CAP):
    for t in range(min(m, cap), 0, -1):
        if m % t == 0 and t % NORM_ROWS == 0:
            return t
    raise ValueError(f"no row tile for {m} rows")


def _col_tile(n, cap):
    for t in range(min(n, cap), 0, -1):
        if n % t == 0 and t % LANES == 0:
            return t
    raise ValueError(f"no column tile for {n} columns")


def _unroll(trips, cap):
    for u in range(min(trips, cap), 0, -1):
        if trips % u == 0:
            return u
    return 1


def _params(*sem):
    return pltpu.CompilerParams(dimension_semantics=sem, vmem_limit_bytes=VMEM_LIMIT)


def _normalise_tile(x_ref, g_ref, xn_ref):
    g = g_ref[...]
    trips = x_ref.shape[0] // NORM_ROWS

    def body(i, carry):
        rows = pl.ds(pl.multiple_of(i * NORM_ROWS, NORM_ROWS), NORM_ROWS)
        xn_ref[rows, :] = _rms(x_ref[rows, :], g).astype(BF16)
        return carry

    lax.fori_loop(0, trips, body, 0, unroll=_unroll(trips, NORM_UNROLL))


def _norm_matmul_kernel(x_ref, g_ref, w_ref, hg_ref, o_ref, xn_ref, *, norm_blocks):
    j = pl.program_id(1)

    @pl.when(j == 0)
    def _():
        _normalise_tile(x_ref, g_ref, xn_ref)

    acc = _dot(xn_ref[...], w_ref[...].astype(BF16))
    if norm_blocks == 0:
        o_ref[...] = acc
        return

    @pl.when(j < norm_blocks)
    def _():
        for h in range(acc.shape[1] // HEAD):
            cols = slice(h * HEAD, (h + 1) * HEAD)
            o_ref[:, cols] = _rms(acc[:, cols], hg_ref[...])

    @pl.when(j >= norm_blocks)
    def _():
        o_ref[...] = acc


def _norm_matmul(x, g, w, layer, head_gain=None, norm_blocks=0, tn_cap=512, name="norm_matmul"):
    m, d = x.shape
    n = w.shape[-1]
    tm, tn = _row_tile(m), _col_tile(n, tn_cap)
    if head_gain is None:
        head_gain = jnp.ones((HEAD,), F32)
    if w.ndim == 2:
        w = w[None]
    return pl.pallas_call(
        functools.partial(_norm_matmul_kernel, norm_blocks=norm_blocks),
        grid=(m // tm, n // tn),
        in_specs=[
            pl.BlockSpec((tm, d), lambda i, j: (i, 0), pipeline_mode=pl.Buffered(1)),
            pl.BlockSpec((1, d), lambda i, j: (0, 0)),
            pl.BlockSpec((None, d, tn), lambda i, j: (layer, 0, j)),
            pl.BlockSpec((1, HEAD), lambda i, j: (0, 0)),
        ],
        out_specs=pl.BlockSpec((tm, tn), lambda i, j: (i, j)),
        out_shape=jax.ShapeDtypeStruct((m, n), F32),
        scratch_shapes=[pltpu.VMEM((tm, d), BF16)],
        compiler_params=_params("parallel", "arbitrary"),
        name=name,
    )(x, g.reshape(1, d), w, head_gain.reshape(1, HEAD))


def _swiglu_up_kernel(x_ref, g_ref, wa_ref, wu_ref, o_ref, xn_ref):
    @pl.when(pl.program_id(1) == 0)
    def _():
        _normalise_tile(x_ref, g_ref, xn_ref)

    xn = xn_ref[...]
    a = _dot(xn, wa_ref[...].astype(BF16))
    u = _dot(xn, wu_ref[...].astype(BF16))
    o_ref[...] = (a * _sigmoid(a) * u).astype(BF16)


def _swiglu_up(x, g, w_gu, layer, name):
    m, d = x.shape
    d_ff = w_gu.shape[-1] // 2
    tm, tn = _row_tile(m, FFN_ROW_TILE_CAP), _col_tile(d_ff, 512)
    nblk = d_ff // tn
    return pl.pallas_call(
        _swiglu_up_kernel,
        grid=(m // tm, nblk),
        in_specs=[
            pl.BlockSpec((tm, d), lambda i, j: (i, 0)),
            pl.BlockSpec((1, d), lambda i, j: (0, 0)),
            pl.BlockSpec((None, d, tn), lambda i, j: (layer, 0, j)),
            pl.BlockSpec((None, d, tn), lambda i, j: (layer, 0, j + nblk)),
        ],
        out_specs=pl.BlockSpec((tm, tn), lambda i, j: (i, j)),
        out_shape=jax.ShapeDtypeStruct((m, d_ff), BF16),
        scratch_shapes=[pltpu.VMEM((tm, d), BF16)],
        compiler_params=_params("parallel", "arbitrary"),
        name=name,
    )(x, g.reshape(1, d), w_gu, w_gu)


def _matmul_residual_kernel(a_ref, w_ref, r_ref, o_ref):
    o_ref[...] = r_ref[...] + _dot(a_ref[...], w_ref[...].astype(BF16))


def _matmul_residual(a, w, layer, res, tn_cap, name, tm_cap=ROW_TILE_CAP):
    m, k = a.shape
    n = w.shape[-1]
    tm, tn = _row_tile(m, tm_cap), _col_tile(n, tn_cap)
    return pl.pallas_call(
        _matmul_residual_kernel,
        grid=(m // tm, n // tn),
        in_specs=[
            pl.BlockSpec((tm, k), lambda i, j: (i, 0)),
            pl.BlockSpec((None, k, tn), lambda i, j: (layer, 0, j)),
            pl.BlockSpec((tm, tn), lambda i, j: (i, j)),
        ],
        out_specs=pl.BlockSpec((tm, tn), lambda i, j: (i, j)),
        out_shape=jax.ShapeDtypeStruct((m, n), F32),
        compiler_params=_params("parallel", "arbitrary"),
        name=name,
    )(a, w, res)


def _lower_bound(logits, layer):
    e = jnp.exp(logits - jnp.max(logits, axis=0, keepdims=True))
    return jnp.sum(e[:layer + 1], axis=0, keepdims=True) / jnp.sum(e, axis=0, keepdims=True)


def _gate_inputs(zq, zf, lb):
    f = lb + (1.0 - lb) * _sigmoid(zf)
    return zq * _sigmoid(zq), 1.0 - f, jnp.log(f)


def _gated_output(o, zg, gain):
    return _rms(o, gain) * (zg * _sigmoid(zg))


def _cumsum_rows(g, seg=None):
    n = g.shape[0]
    row = lax.broadcasted_iota(jnp.int32, g.shape, 0)
    pos = row if seg is None else _div_mod(row, seg)[1]
    span = n if seg is None else seg
    s = 1
    while s < span:
        g = g + jnp.where(pos >= s, pltpu.roll(g, s, 0), 0.0)
        s *= 2
    return g


def _gla_prompt_kernel(zq_ref, zf_ref, zi_ref, zg_ref, lbl_ref, gain_ref, o_init_ref, o_ref, s_ref,
                       st_ref, *, layer, hp):
    del o_init_ref
    t = pl.program_id(1)
    c = GLA_CHUNK
    mid = (c - 1) // 2

    @pl.when(t == 0)
    def _():
        st_ref[...] = jnp.zeros_like(st_ref)

    lb_all = _lower_bound(lbl_ref[...], layer)
    gain = gain_ref[...]
    r2 = lax.broadcasted_iota(jnp.int32, (c, c), 0)
    c2 = lax.broadcasted_iota(jnp.int32, (c, c), 1)
    st = [st_ref[j] for j in range(hp)]
    for i in range(o_ref.shape[0] // c):
        rows = slice(i * c, (i + 1) * c)
        for j in range(hp):
            cols = slice(j * HEAD, (j + 1) * HEAD)
            q, k, g = _gate_inputs(zq_ref[rows, cols], zf_ref[rows, cols], lb_all[:, cols])
            v16 = zi_ref[rows, cols].astype(BF16)
            cum = _cumsum_rows(g)
            ref = cum[mid:mid + 1, :]
            last = cum[c - 1:c, :]
            att = _dot_nt((q * jnp.exp(cum - ref)).astype(BF16),
                          (k * jnp.exp(ref - cum)).astype(BF16))
            att = jnp.where(r2 >= c2, att, 0.0)
            o = _dot(att.astype(BF16), v16) + _dot_nt((q * jnp.exp(cum)).astype(BF16),
                                                       st[j].astype(BF16))
            k_out = (k * jnp.exp(last - cum)).astype(BF16)
            st[j] = jnp.exp(last) * st[j] + _dot_tn(v16, k_out)
            o_ref[rows, cols] = _gated_output(o, zg_ref[rows, cols], gain).astype(BF16)
    for j in range(hp):
        st_ref[j] = st[j]

    @pl.when(t == pl.num_programs(1) - 1)
    def _():
        for j in range(hp):
            s_ref[j] = st[j].T


def _gla_prompt(z, lb_logits, gain, layer, seq, tl=512, hp=GLA_HEADS_PER_STEP):
    heads = z.shape[1] // (4 * HEAD)
    hp = _unroll(heads, hp)
    tl = min(tl, seq)
    slots = lb_logits.shape[0]
    groups = heads // hp
    zspec = lambda off: pl.BlockSpec((tl, hp * HEAD), lambda h, t: (t, h + off * groups))
    return pl.pallas_call(
        functools.partial(_gla_prompt_kernel, layer=layer, hp=hp),
        grid=(groups, seq // tl),
        in_specs=[zspec(0), zspec(1), zspec(2), zspec(3),
                  pl.BlockSpec((slots, hp * HEAD), lambda h, t: (0, h)),
                  pl.BlockSpec((1, HEAD), lambda h, t: (0, 0)),
                  pl.BlockSpec(memory_space=pl.ANY)],
        out_specs=[pl.BlockSpec((tl, hp * HEAD), lambda h, t: (t, h)),
                   pl.BlockSpec((hp, HEAD, HEAD), lambda h, t: (h, 0, 0))],
        out_shape=[jax.ShapeDtypeStruct((z.shape[0], heads * HEAD), BF16),
                   jax.ShapeDtypeStruct((heads, HEAD, HEAD), F32)],
        scratch_shapes=[pltpu.VMEM((hp, HEAD, HEAD), F32)],
        input_output_aliases={6: 0},
        compiler_params=_params("parallel", "arbitrary"),
        name="gla_prompt",
    )(z, z, z, z, lb_logits, gain.reshape(1, HEAD), jnp.zeros((z.shape[0], heads * HEAD), BF16))


def _gla_sample_kernel(zq_ref, zf_ref, zi_ref, zg_ref, lbl_ref, gain_ref, s0_ref, o_in_ref,
                       o_ref, s_ref, last_ref, kout_ref, qin_ref, v_ref, *, layer, steps):
    del o_in_ref
    rows = zq_ref.shape[0]
    mid = (steps - 1) // 2
    lb = _lower_bound(lbl_ref[...], layer)
    q, k, g = _gate_inputs(zq_ref[...], zf_ref[...], lb)
    v16 = zi_ref[...].astype(BF16)
    cum = _cumsum_rows(g, seg=steps)

    row = lax.broadcasted_iota(jnp.int32, cum.shape, 0)
    pos = _div_mod(row, steps)[1]

    def spread(src_pos):
        picked = jnp.where(pos == src_pos, cum, 0.0)
        out = picked
        for d in range(steps):
            if d != src_pos:
                out = out + pltpu.roll(picked, (d - src_pos) % rows, 0)
        return out

    ref = spread(mid)
    last = spread(steps - 1)
    r2 = lax.broadcasted_iota(jnp.int32, (rows, rows), 0)
    c2 = lax.broadcasted_iota(jnp.int32, (rows, rows), 1)
    att = _dot_nt((q * jnp.exp(cum - ref)).astype(BF16), (k * jnp.exp(ref - cum)).astype(BF16))
    att = jnp.where((r2 >= c2) & (_div_mod(r2, steps)[0] == _div_mod(c2, steps)[0]), att, 0.0)
    o_intra = _dot(att.astype(BF16), v16)
    qin_ref[...] = (q * jnp.exp(cum)).astype(BF16)
    v_ref[...] = v16
    last_ref[...] = last
    kout_ref[...] = k * jnp.exp(last - cum)
    n_seq = s0_ref.shape[0]

    def body(b, o_inter):
        lo = b * steps
        mine = (row >= lo) & (row < lo + steps)
        st = s0_ref[b].T
        o_inter = o_inter + jnp.where(mine, _dot_nt(qin_ref[...], st.astype(BF16)), 0.0)
        k_out = jnp.where(mine, kout_ref[...], 0.0).astype(BF16)
        decay = jnp.exp(last_ref[pl.ds(lo, 1), :])
        s_ref[b] = (decay * st + _dot_tn(v_ref[...], k_out)).T
        return o_inter

    o_inter = lax.fori_loop(0, n_seq, body, jnp.zeros(cum.shape, F32),
                            unroll=_unroll(n_seq, SAMPLE_SEQ_UNROLL))
    o_ref[...] = _gated_output(o_intra + o_inter, zg_ref[...], gain_ref[...]).astype(BF16)


def _gla_sample(z, o_all, row0, n_seq, steps, s0, layer_s0, lb_logits, gain, layer):
    heads = z.shape[1] // (4 * HEAD)
    rows = n_seq * steps
    slots = lb_logits.shape[0]
    rb = row0 // rows
    zspec = lambda off: pl.BlockSpec((rows, HEAD), lambda h: (rb, h + off * heads))
    s_in = pl.BlockSpec((None, n_seq, None, HEAD, HEAD), lambda h: (layer_s0, 0, h, 0, 0))
    s_out = pl.BlockSpec((n_seq, None, HEAD, HEAD), lambda h: (0, h, 0, 0))
    return pl.pallas_call(
        functools.partial(_gla_sample_kernel, layer=layer, steps=steps),
        grid=(heads,),
        in_specs=[zspec(0), zspec(1), zspec(2), zspec(3),
                  pl.BlockSpec((slots, HEAD), lambda h: (0, h)),
                  pl.BlockSpec((1, HEAD), lambda h: (0, 0)),
                  s_in,
                  pl.BlockSpec(memory_space=pl.ANY)],
        out_specs=[pl.BlockSpec((rows, HEAD), lambda h: (rb, h)), s_out],
        out_shape=[jax.ShapeDtypeStruct(o_all.shape, o_all.dtype),
                   jax.ShapeDtypeStruct(s0.shape[1:], s0.dtype)],
        scratch_shapes=[pltpu.VMEM((rows, HEAD), F32), pltpu.VMEM((rows, HEAD), F32),
                        pltpu.VMEM((rows, HEAD), BF16), pltpu.VMEM((rows, HEAD), BF16)],
        input_output_aliases={7: 0},
        compiler_params=_params("parallel"),
        name="gla_sample",
    )(z, z, z, z, lb_logits, gain.reshape(1, HEAD), s0, o_all)


def _top_blocks_bits(gate_t, blk_f, n_valid):
    gate_t = jnp.where(blk_f < n_valid, gate_t, -jnp.inf)
    bits = jnp.zeros((1, gate_t.shape[1]), jnp.int32)
    for _ in range(MOBA_TOPK):
        mx = jnp.max(gate_t, axis=0, keepdims=True)
        idx = jnp.min(jnp.where(gate_t == mx, blk_f, float(LANES)), axis=0, keepdims=True)
        live = mx > -jnp.inf
        gate_t = jnp.where(blk_f == idx, -jnp.inf, gate_t)
        shift = jnp.minimum(idx, 31.0).astype(jnp.int32)
        bits = bits | jnp.where(live, jnp.left_shift(1, shift), 0)
    return bits


def _moba_prompt_kernel(q_ref, k_ref, v_ref, o_init_ref, o_ref, kmean_ref, qb_ref, sel_ref, m_ref,
                        l_ref, acc_ref, *, n_blocks, group, chunk):
    del o_init_ref
    qi = pl.program_id(1)
    blk = MOBA_BLOCK
    rows = group * blk
    n_chunks = rows // chunk

    @pl.when(qi == 0)
    def _():
        kmean_ref[...] = jnp.zeros_like(kmean_ref)
        for n in range(n_blocks):
            kmean_ref[n:n + 1, :] = jnp.mean(k_ref[n * blk:(n + 1) * blk, :], axis=0, keepdims=True)

    blk_f = lax.broadcasted_iota(jnp.int32, (kmean_ref.shape[0], blk), 0).astype(F32)
    qi_f = qi.astype(F32)
    for g in range(group):
        qg = q_ref[:, g * HEAD:(g + 1) * HEAD]
        qb_ref[g * blk:(g + 1) * blk, :] = (qg * SCALE_LOG2E).astype(BF16)
        gate_t = _dot_nt(kmean_ref[...], qg, precision=lax.Precision.HIGHEST)
        bits = _top_blocks_bits(gate_t, blk_f, qi_f)
        per_row = jnp.broadcast_to(bits, (SUBLANES, blk)).T
        sel_ref[g * blk:(g + 1) * blk, :] = jnp.broadcast_to(per_row[:, :1], (blk, LANES))

    own = pl.ds(pl.multiple_of(qi * blk, blk), blk)
    kd = k_ref[own, :].astype(BF16)
    vd = v_ref[own, :].astype(BF16)
    own_chunk = min(PROMPT_OWN_CHUNK, blk)
    for c in range(rows // own_chunk):
        r = slice(c * own_chunk, (c + 1) * own_chunk)
        q_pos = lax.broadcasted_iota(jnp.int32, (own_chunk, blk), 0) + (c * own_chunk) % blk
        k_pos = lax.broadcasted_iota(jnp.int32, (own_chunk, blk), 1)
        s = jnp.where(k_pos <= q_pos, _dot_nt(qb_ref[r, :], kd), NEG)
        m = jnp.max(s, axis=1, keepdims=True)
        p = jnp.exp2(s - m)
        m_ref[r, :] = jnp.broadcast_to(m, (own_chunk, LANES))
        l_ref[r, :] = jnp.broadcast_to(jnp.sum(p, axis=1, keepdims=True), (own_chunk, LANES))
        acc_ref[r, :] = _dot(p.astype(BF16), vd)

    def body(jp, carry):
        ja = 2 * jp
        jb = jnp.minimum(ja + 1, qi - 1)
        bit_a = jnp.left_shift(jnp.int32(1), ja)
        bit_b = jnp.where(ja + 1 < qi, jnp.left_shift(jnp.int32(1), jb), 0)
        rows_a = pl.ds(pl.multiple_of(ja * blk, blk), blk)
        rows_b = pl.ds(pl.multiple_of(jb * blk, blk), blk)
        ka = k_ref[rows_a, :].astype(BF16)
        kb = k_ref[rows_b, :].astype(BF16)
        va = v_ref[rows_a, :].astype(BF16)
        vb = v_ref[rows_b, :].astype(BF16)

        def scores(c):
            qc = qb_ref[c * chunk:(c + 1) * chunk, :]
            return _dot_nt(qc, ka), _dot_nt(qc, kb)

        pending = [scores(c) for c in range(min(SCORE_LOOKAHEAD, n_chunks))]
        for c in range(n_chunks):
            r = slice(c * chunk, (c + 1) * chunk)
            sel = sel_ref[r, :]
            on_a = (sel & bit_a) != 0
            on_b = (sel & bit_b) != 0
            sa, sb = pending.pop(0)
            if c + SCORE_LOOKAHEAD < n_chunks:
                pending.append(scores(c + SCORE_LOOKAHEAD))
            parts = [jnp.where(on_a, sa[:, :LANES], NEG), jnp.where(on_a, sa[:, LANES:], NEG),
                     jnp.where(on_b, sb[:, :LANES], NEG), jnp.where(on_b, sb[:, LANES:], NEG)]
            m_prev = m_ref[r, :]
            top = jnp.maximum(jnp.maximum(parts[0], parts[1]), jnp.maximum(parts[2], parts[3]))
            m_new = jnp.maximum(m_prev, jnp.max(top, axis=1, keepdims=True))
            ps = [jnp.exp2(x - m_new) for x in parts]
            alpha = jnp.exp2(m_prev - m_new)
            l_ref[r, :] = alpha * l_ref[r, :] + jnp.sum((ps[0] + ps[1]) + (ps[2] + ps[3]), axis=1,
                                                        keepdims=True)
            pa = jnp.concatenate(ps[:2], axis=1).astype(BF16)
            pb = jnp.concatenate(ps[2:], axis=1).astype(BF16)
            acc_ref[r, :] = alpha * acc_ref[r, :] + (_dot(pa, va) + _dot(pb, vb))
            m_ref[r, :] = m_new
        return carry

    lax.fori_loop(0, (qi + 1) // 2, body, 0)
    for g in range(group):
        r = slice(g * blk, (g + 1) * blk)
        o_ref[:, g * HEAD:(g + 1) * HEAD] = (acc_ref[r, :] / l_ref[r, :]).astype(BF16)


def _moba_prompt(q, kv, seq, n_kv, chunk=PROMPT_ROW_CHUNK):
    heads = q.shape[1] // HEAD
    group = heads // n_kv
    n_blocks = seq // MOBA_BLOCK
    blk = MOBA_BLOCK
    rows = group * blk
    return pl.pallas_call(
        functools.partial(_moba_prompt_kernel, n_blocks=n_blocks, group=group, chunk=chunk),
        grid=(n_kv, n_blocks),
        in_specs=[pl.BlockSpec((blk, group * HEAD), lambda h, i: (i, h)),
                  pl.BlockSpec((seq, HEAD), lambda h, i: (0, h)),
                  pl.BlockSpec((seq, HEAD), lambda h, i: (0, n_kv + h)),
                  pl.BlockSpec(memory_space=pl.ANY)],
        out_specs=pl.BlockSpec((blk, group * HEAD), lambda h, i: (i, h)),
        out_shape=jax.ShapeDtypeStruct((q.shape[0], heads * HEAD), BF16),
        scratch_shapes=[pltpu.VMEM((-(-n_blocks // SUBLANES) * SUBLANES, HEAD), F32),
                        pltpu.VMEM((rows, HEAD), BF16),
                        pltpu.VMEM((rows, LANES), jnp.int32),
                        pltpu.VMEM((rows, LANES), F32),
                        pltpu.VMEM((rows, LANES), F32),
                        pltpu.VMEM((rows, HEAD), F32)],
        input_output_aliases={3: 0},
        compiler_params=_params("arbitrary", "arbitrary"),
        name="moba_prompt",
    )(q, kv, kv, jnp.zeros((q.shape[0], heads * HEAD), BF16))


def _moba_sample_kernel(pt_ref, q_ref, kn_ref, vn_ref, *refs, n_kv, group, steps, pages_per_block,
                        blocks_per_step):
    del pt_ref
    n_pages = pages_per_block * blocks_per_step
    kp = refs[:n_pages]
    vp = refs[n_pages:2 * n_pages]
    o_ref, m_s, l_s, g_s, o_s, bias_s = refs[2 * n_pages:]
    j = pl.program_id(1)
    rows = q_ref.shape[1]
    rq = group * steps
    tok_per_vreg = SUBLANES // n_kv
    n_col = bias_s.shape[1]

    @pl.when((pl.program_id(0) == 0) & (j == 0))
    def _():
        row_head = _div_mod(lax.broadcasted_iota(jnp.int32, (rows, n_col), 0), rq)[0]
        col_head = _div_mod(lax.broadcasted_iota(jnp.int32, (rows, n_col), 1), n_kv)[1]
        bias_s[...] = jnp.where(row_head == col_head, 0.0, NEG)

    qf = q_ref[0]
    q16 = qf.astype(BF16)
    head64 = _div_mod(lax.broadcasted_iota(jnp.int32, (rows, HEAD), 0), rq)[0]
    kbs, vbs, scores = [], [], []
    for t in range(blocks_per_step):
        pages = slice(t * pages_per_block, (t + 1) * pages_per_block)
        kbs.append(jnp.concatenate([r[0] for r in kp[pages]], axis=0))
        vbs.append(jnp.concatenate([r[0] for r in vp[pages]], axis=0))
        scores.append(_dot_nt(q16, kbs[t].astype(BF16)))
    for t in range(blocks_per_step):
        kb = kbs[t]
        s = scores[t] * SCALE + bias_s[...]
        m = jnp.max(s, axis=1, keepdims=True)
        p = jnp.exp(s - m)

        parts = [kb[SUBLANES * i:SUBLANES * (i + 1), :] for i in range(n_col // SUBLANES)]
        while len(parts) > 1:
            parts = [a + b for a, b in zip(parts[::2], parts[1::2])]
        folded = parts[0]
        kmean = jnp.zeros((rows, HEAD), F32)
        for h in range(n_kv):
            total = folded[h:h + 1, :]
            for i in range(1, tok_per_vreg):
                total = total + folded[h + i * n_kv:h + i * n_kv + 1, :]
            kmean = jnp.where(head64 == h, total * (1.0 / MOBA_BLOCK), kmean)
        gate = jnp.sum(qf * kmean, axis=1, keepdims=True)

        n = j * blocks_per_step + t
        m_s[n] = jnp.broadcast_to(m, (rows, LANES))
        l_s[n] = jnp.broadcast_to(jnp.sum(p, axis=1, keepdims=True), (rows, LANES))
        g_s[n] = jnp.broadcast_to(gate, (rows, LANES))
        o_s[n] = _dot(p.astype(BF16), vbs[t].astype(BF16))

    @pl.when(j == pl.num_programs(1) - 1)
    def _():
        gates = g_s[...]
        n_io = lax.broadcasted_iota(jnp.int32, gates.shape, 0)
        sel = jnp.zeros(gates.shape, jnp.bool_)
        for _ in range(min(MOBA_TOPK, g_s.shape[0])):
            mx = jnp.max(gates, axis=0, keepdims=True)
            idx = jnp.min(jnp.where(gates == mx, n_io, g_s.shape[0]), axis=0, keepdims=True)
            pick = n_io == idx
            sel = sel | pick
            gates = jnp.where(pick, -jnp.inf, gates)

        row = lax.broadcasted_iota(jnp.int32, (rows, LANES), 0)
        lane = lax.broadcasted_iota(jnp.int32, (rows, LANES), 1)
        q_head, q_step = _div_mod(row, rq)[0], _div_mod(row, steps)[1]
        k_step, k_head = _div_mod(lane, n_kv)
        ok = (q_head == k_head) & (k_step <= q_step) & (k_step < steps)
        s_own = jnp.where(ok, _dot_nt(q16, kn_ref[0].astype(BF16)) * SCALE, NEG)
        m_own = jnp.max(s_own, axis=1, keepdims=True)
        p_own = jnp.exp(s_own - m_own)
        l_own = jnp.sum(p_own, axis=1, keepdims=True)
        o_own = _dot(p_own.astype(BF16), vn_ref[0].astype(BF16))

        m_all = m_s[...]
        m_top = jnp.maximum(m_own, jnp.max(jnp.where(sel, m_all, -jnp.inf), axis=0))
        w = jnp.where(sel, jnp.exp(jnp.minimum(m_all - m_top[None], 0.0)), 0.0)
        w_own = jnp.exp(m_own - m_top)
        den = w_own * l_own + jnp.sum(w * l_s[...], axis=0)
        num = w_own * o_own + jnp.sum(w * o_s[...], axis=0)
        o_ref[0] = num / den


def _moba_sample(q, k_new, v_new, cache_k, cache_v, page_table, n_kv, steps):
    n_seq, n_pages = page_table.shape
    n_phys, page, _, _ = cache_k.shape
    ppb = MOBA_BLOCK // page
    n_blocks = n_pages // ppb
    bps = _unroll(n_blocks, SAMPLE_BLOCKS_PER_STEP)
    rows = q.shape[1]
    group = rows // (n_kv * steps)
    assert SUBLANES % n_kv == 0 and steps * n_kv <= LANES
    ck = cache_k.reshape(n_phys, page * n_kv, HEAD)
    cv = cache_v.reshape(n_phys, page * n_kv, HEAD)
    pages_per_step = ppb * bps

    def page_spec(p):
        return pl.BlockSpec((1, page * n_kv, HEAD),
                            lambda b, j, pt: (pt[b, j * pages_per_step + p], 0, 0))

    per_seq = lambda shape: pl.BlockSpec(shape, lambda b, j, pt: (b, 0, 0))
    grid_spec = pltpu.PrefetchScalarGridSpec(
        num_scalar_prefetch=1,
        grid=(n_seq, n_blocks // bps),
        in_specs=[per_seq((1, rows, HEAD)), per_seq((1, LANES, HEAD)), per_seq((1, LANES, HEAD))]
                 + [page_spec(p) for p in range(pages_per_step)] * 2,
        out_specs=per_seq((1, rows, HEAD)),
        scratch_shapes=[pltpu.VMEM((n_blocks, rows, LANES), F32) for _ in range(4)]
                       + [pltpu.VMEM((rows, MOBA_BLOCK * n_kv), F32)],
    )
    return pl.pallas_call(
        functools.partial(_moba_sample_kernel, n_kv=n_kv, group=group, steps=steps,
                          pages_per_block=ppb, blocks_per_step=bps),
        grid_spec=grid_spec,
        out_shape=jax.ShapeDtypeStruct((n_seq, rows, HEAD), F32),
        compiler_params=_params("arbitrary", "arbitrary"),
        name="moba_sample",
    )(page_table, q, k_new, v_new, *([ck] * pages_per_step), *([cv] * pages_per_step))


def kernel(x_prompt, x_sample, state_hgrn, cache_k, cache_v, page_table, norm_mix_a, w_in_a, lb_logits,
           onorm_a, w_out_a, norm_kv, w_kv, k_norm, norm_mix_b, w_q_b, q_norm, w_o_b, norm_ffn,
           w_gate_up, w_down):
    batch, seq, d = x_prompt.shape
    n_seq, steps, _ = x_sample.shape
    n_a = w_in_a.shape[0]
    depth = norm_ffn.shape[0]
    heads = d // HEAD
    n_kv = cache_k.shape[2]
    group = heads // n_kv
    kv_width = n_kv * HEAD
    n_dec = n_seq * steps
    past = page_table.shape[1] * cache_k.shape[1]
    assert batch == 1 and seq % MOBA_BLOCK == 0 and seq // MOBA_BLOCK <= 32
    assert past % MOBA_BLOCK == 0 and MOBA_BLOCK % cache_k.shape[1] == 0
    assert steps <= LANES and seq % n_dec == 0 and seq % GLA_CHUNK == 0

    h = jnp.concatenate([x_prompt.reshape(seq, d), x_sample.reshape(n_dec, d)], axis=0)
    states_p, states_s = [], []
    kv = None
    for layer in range(depth):
        if layer < n_a:
            z = _norm_matmul(h, norm_mix_a[layer], w_in_a, layer, name="hgrn_in")
            o, s_p = _gla_prompt(z, lb_logits, onorm_a[layer], layer, seq)
            o, s_s = _gla_sample(z, o, seq, n_seq, steps, state_hgrn, layer, lb_logits,
                                 onorm_a[layer], layer)
            h = _matmul_residual(o, w_out_a, layer, h, 512, "hgrn_out")
            states_p.append(s_p.reshape(1, heads, HEAD, HEAD).astype(state_hgrn.dtype))
            states_s.append(s_s)
        else:
            if kv is None:
                kv = _norm_matmul(h, norm_kv, w_kv, 0, head_gain=k_norm, norm_blocks=1,
                                  tn_cap=kv_width, name="kv_proj")
            jb = layer - n_a
            q = _norm_matmul(h, norm_mix_b[jb], w_q_b, jb, head_gain=q_norm[jb],
                             norm_blocks=heads, name="q_proj")
            att = _moba_prompt(q, kv, seq, n_kv)
            q_s = q[seq:].reshape(n_seq, steps, n_kv, group, HEAD).transpose(0, 2, 3, 1, 4)
            pad = ((0, 0), (0, LANES - steps * n_kv), (0, 0))
            k_s = jnp.pad(kv[seq:, :kv_width].reshape(n_seq, steps * n_kv, HEAD), pad)
            v_s = jnp.pad(kv[seq:, kv_width:].reshape(n_seq, steps * n_kv, HEAD), pad)
            att_s = _moba_sample(q_s.reshape(n_seq, n_kv * group * steps, HEAD), k_s, v_s,
                                 cache_k, cache_v, page_table, n_kv, steps)
            att_s = att_s.reshape(n_seq, n_kv, group, steps, HEAD).transpose(0, 3, 1, 2, 4)
            att = lax.dynamic_update_slice(att, att_s.reshape(n_dec, d).astype(BF16), (seq, 0))
            h = _matmul_residual(att, w_o_b, jb, h, 512, "attn_out")
        hf = _swiglu_up(h, norm_ffn[layer], w_gate_up, layer, "ffn_up")
        h = _matmul_residual(hf, w_down, layer, h, 256, "ffn_down", tm_cap=FFN_ROW_TILE_CAP)

    k_new = kv[:, :kv_width]
    v_new = kv[:, kv_width:]
    return (h[:seq].reshape(batch, seq, d),
            h[seq:].reshape(n_seq, steps, d),
            jnp.stack(states_p),
            jnp.stack(states_s),
            k_new[:seq].reshape(batch, seq, n_kv, HEAD),
            v_new[:seq].reshape(batch, seq, n_kv, HEAD),
            k_new[seq:].reshape(n_seq, steps, n_kv, HEAD),
            v_new[seq:].reshape(n_seq, steps, n_kv, HEAD))
```

```python
import functools

import jax
import jax.numpy as jnp
from jax import lax
from jax.experimental import pallas as pl
from jax.experimental.pallas import tpu as pltpu

F32 = jnp.float32
BF16 = jnp.bfloat16

EPS = 1e-6
HEAD = 128
GLA_CHUNK = 64
MOBA_BLOCK = 256
MOBA_TOPK = 3
SCALE = HEAD ** -0.5
SCALE_LOG2E = SCALE * 1.4426950408889634
NEG = -1e30
LANES = 128
SUBLANES = 8
GLA_HEADS_PER_STEP = 4
NORM_ROWS = 16
NORM_UNROLL = 5
ROW_TILE_CAP = 2080
FFN_ROW_TILE_CAP = 1040
PROMPT_ROW_CHUNK = 256
PROMPT_OWN_CHUNK = 128
SCORE_LOOKAHEAD = 2
PAIRS_PER_TRIP = 2
SAMPLE_BLOCKS_PER_STEP = 8
SAMPLE_SEQ_UNROLL = 4
VMEM_LIMIT = 56 * 1024 * 1024


def _dot(a, b):
    return jnp.dot(a, b, preferred_element_type=F32)


def _dot_nt(a, b, precision=None):
    return lax.dot_general(a, b, (((1,), (1,)), ((), ())), precision=precision,
                           preferred_element_type=F32)


def _dot_tn(a, b):
    return lax.dot_general(a, b, (((0,), (0,)), ((), ())), preferred_element_type=F32)


def _div_mod(x, n):
    if n & (n - 1) == 0:
        return jnp.right_shift(x, n.bit_length() - 1), x & (n - 1)
    q = x // n
    return q, x - q * n


def _sigmoid(x):
    return 1.0 / (1.0 + jnp.exp(-x))


def _rms(x, g):
    return x * lax.rsqrt(jnp.mean(x * x, axis=-1, keepdims=True) + EPS) * g


def _row_tile(m, cap=ROW_TILE_CAP):
    for t in range(min(m, cap), 0, -1):
        if m % t == 0 and t % NORM_ROWS == 0:
            return t
    raise ValueError(f"no row tile for {m} rows")


def _col_tile(n, cap):
    for t in range(min(n, cap), 0, -1):
        if n % t == 0 and t % LANES == 0:
            return t
    raise ValueError(f"no column tile for {n} columns")


def _unroll(trips, cap):
    for u in range(min(trips, cap), 0, -1):
        if trips % u == 0:
            return u
    return 1


def _params(*sem):
    return pltpu.CompilerParams(dimension_semantics=sem, vmem_limit_bytes=VMEM_LIMIT)


def _normalise_tile(x_ref, g_ref, xn_ref):
    g = g_ref[...]
    trips = x_ref.shape[0] // NORM_ROWS

    def body(i, carry):
        rows = pl.ds(pl.multiple_of(i * NORM_ROWS, NORM_ROWS), NORM_ROWS)
        xn_ref[rows, :] = _rms(x_ref[rows, :], g).astype(BF16)
        return carry

    lax.fori_loop(0, trips, body, 0, unroll=_unroll(trips, NORM_UNROLL))


def _norm_matmul_kernel(x_ref, g_ref, w_ref, hg_ref, o_ref, xn_ref, *, norm_blocks):
    j = pl.program_id(1)

    @pl.when(j == 0)
    def _():
        _normalise_tile(x_ref, g_ref, xn_ref)

    acc = _dot(xn_ref[...], w_ref[...].astype(BF16))
    if norm_blocks == 0:
        o_ref[...] = acc
        return

    @pl.when(j < norm_blocks)
    def _():
        for h in range(acc.shape[1] // HEAD):
            cols = slice(h * HEAD, (h + 1) * HEAD)
            o_ref[:, cols] = _rms(acc[:, cols], hg_ref[...])

    @pl.when(j >= norm_blocks)
    def _():
        o_ref[...] = acc


def _norm_matmul(x, g, w, layer, head_gain=None, norm_blocks=0, tn_cap=512, name="norm_matmul"):
    m, d = x.shape
    n = w.shape[-1]
    tm, tn = _row_tile(m), _col_tile(n, tn_cap)
    if head_gain is None:
        head_gain = jnp.ones((HEAD,), F32)
    if w.ndim == 2:
        w = w[None]
    return pl.pallas_call(
        functools.partial(_norm_matmul_kernel, norm_blocks=norm_blocks),
        grid=(m // tm, n // tn),
        in_specs=[
            pl.BlockSpec((tm, d), lambda i, j: (i, 0), pipeline_mode=pl.Buffered(1)),
            pl.BlockSpec((1, d), lambda i, j: (0, 0)),
            pl.BlockSpec((None, d, tn), lambda i, j: (layer, 0, j)),
            pl.BlockSpec((1, HEAD), lambda i, j: (0, 0)),
        ],
        out_specs=pl.BlockSpec((tm, tn), lambda i, j: (i, j)),
        out_shape=jax.ShapeDtypeStruct((m, n), F32),
        scratch_shapes=[pltpu.VMEM((tm, d), BF16)],
        compiler_params=_params("parallel", "arbitrary"),
        name=name,
    )(x, g.reshape(1, d), w, head_gain.reshape(1, HEAD))


def _swiglu_up_kernel(x_ref, g_ref, wa_ref, wu_ref, o_ref, xn_ref):
    @pl.when(pl.program_id(1) == 0)
    def _():
        _normalise_tile(x_ref, g_ref, xn_ref)

    xn = xn_ref[...]
    a = _dot(xn, wa_ref[...].astype(BF16))
    u = _dot(xn, wu_ref[...].astype(BF16))
    o_ref[...] = (a * _sigmoid(a) * u).astype(BF16)


def _swiglu_up(x, g, w_gu, layer, name):
    m, d = x.shape
    d_ff = w_gu.shape[-1] // 2
    tm, tn = _row_tile(m, FFN_ROW_TILE_CAP), _col_tile(d_ff, 512)
    nblk = d_ff // tn
    return pl.pallas_call(
        _swiglu_up_kernel,
        grid=(m // tm, nblk),
        in_specs=[
            pl.BlockSpec((tm, d), lambda i, j: (i, 0)),
            pl.BlockSpec((1, d), lambda i, j: (0, 0)),
            pl.BlockSpec((None, d, tn), lambda i, j: (layer, 0, j)),
            pl.BlockSpec((None, d, tn), lambda i, j: (layer, 0, j + nblk)),
        ],
        out_specs=pl.BlockSpec((tm, tn), lambda i, j: (i, j)),
        out_shape=jax.ShapeDtypeStruct((m, d_ff), BF16),
        scratch_shapes=[pltpu.VMEM((tm, d), BF16)],
        compiler_params=_params("parallel", "arbitrary"),
        name=name,
    )(x, g.reshape(1, d), w_gu, w_gu)


def _matmul_residual_kernel(a_ref, w_ref, r_ref, o_ref):
    o_ref[...] = r_ref[...] + _dot(a_ref[...], w_ref[...].astype(BF16))


def _matmul_residual_split_kernel(a_ref, w_ref, r_ref, head_ref, tail_ref, *, tail_start):
    out = r_ref[...] + _dot(a_ref[...], w_ref[...].astype(BF16))
    head_ref[...] = out
    last = pl.program_id(0) == pl.num_programs(0) - 1

    @pl.when(last)
    def _():
        tail_ref[...] = out[tail_start:tail_start + tail_ref.shape[0], :]

    @pl.when(jnp.logical_not(last))
    def _():
        tail_ref[...] = jnp.zeros_like(tail_ref)


def _matmul_residual(a, w, layer, res, tn_cap, name, tm_cap=ROW_TILE_CAP, head_rows=None):
    m, k = a.shape
    n = w.shape[-1]
    tm, tn = _row_tile(m, tm_cap), _col_tile(n, tn_cap)
    in_specs = [
        pl.BlockSpec((tm, k), lambda i, j: (i, 0)),
        pl.BlockSpec((None, k, tn), lambda i, j: (layer, 0, j)),
        pl.BlockSpec((tm, tn), lambda i, j: (i, j)),
    ]
    if head_rows is None:
        return pl.pallas_call(
            _matmul_residual_kernel,
            grid=(m // tm, n // tn),
            in_specs=in_specs,
            out_specs=pl.BlockSpec((tm, tn), lambda i, j: (i, j)),
            out_shape=jax.ShapeDtypeStruct((m, n), F32),
            compiler_params=_params("parallel", "arbitrary"),
            name=name,
        )(a, w, res)
    tail_rows = m - head_rows
    tail_start = head_rows - (m // tm - 1) * tm
    assert 0 < tail_start and tail_start + tail_rows == tm and tail_start % SUBLANES == 0
    head, tail = pl.pallas_call(
        functools.partial(_matmul_residual_split_kernel, tail_start=tail_start),
        grid=(m // tm, n // tn),
        in_specs=in_specs,
        out_specs=[pl.BlockSpec((tm, tn), lambda i, j: (i, j)),
                   pl.BlockSpec((tail_rows, tn), lambda i, j: (i, j))],
        out_shape=[jax.ShapeDtypeStruct((head_rows, n), F32),
                   jax.ShapeDtypeStruct((m // tm * tail_rows, n), F32)],
        compiler_params=_params("parallel", "arbitrary"),
        name=name,
    )(a, w, res)
    return head, tail[(m // tm - 1) * tail_rows:]


def _lower_bound(logits, layer):
    e = jnp.exp(logits - jnp.max(logits, axis=0, keepdims=True))
    return jnp.sum(e[:layer + 1], axis=0, keepdims=True) / jnp.sum(e, axis=0, keepdims=True)


def _gate_inputs(zq, zf, lb):
    f = lb + (1.0 - lb) * _sigmoid(zf)
    return zq * _sigmoid(zq), 1.0 - f, jnp.log(f)


def _gated_output(o, zg, gain):
    return _rms(o, gain) * (zg * _sigmoid(zg))


def _cumsum_rows(g, seg=None):
    n = g.shape[0]
    row = lax.broadcasted_iota(jnp.int32, g.shape, 0)
    pos = row if seg is None else _div_mod(row, seg)[1]
    span = n if seg is None else seg
    s = 1
    while s < span:
        g = g + jnp.where(pos >= s, pltpu.roll(g, s, 0), 0.0)
        s *= 2
    return g


def _gla_prompt_kernel(zq_ref, zf_ref, zi_ref, zg_ref, lbl_ref, gain_ref, o_init_ref, o_ref, s_ref,
                       st_ref, *, layer, hp):
    del o_init_ref
    t = pl.program_id(1)
    c = GLA_CHUNK
    mid = (c - 1) // 2

    @pl.when(t == 0)
    def _():
        st_ref[...] = jnp.zeros_like(st_ref)

    lb_all = _lower_bound(lbl_ref[...], layer)
    gain = gain_ref[...]
    r2 = lax.broadcasted_iota(jnp.int32, (c, c), 0)
    c2 = lax.broadcasted_iota(jnp.int32, (c, c), 1)
    st = [st_ref[j] for j in range(hp)]
    for i in range(o_ref.shape[0] // c):
        rows = slice(i * c, (i + 1) * c)
        for j in range(hp):
            cols = slice(j * HEAD, (j + 1) * HEAD)
            q, k, g = _gate_inputs(zq_ref[rows, cols], zf_ref[rows, cols], lb_all[:, cols])
            v16 = zi_ref[rows, cols].astype(BF16)
            cum = _cumsum_rows(g)
            ref = cum[mid:mid + 1, :]
            last = cum[c - 1:c, :]
            att = _dot_nt((q * jnp.exp(cum - ref)).astype(BF16),
                          (k * jnp.exp(ref - cum)).astype(BF16))
            att = jnp.where(r2 >= c2, att, 0.0)
            o = _dot(att.astype(BF16), v16) + _dot_nt((q * jnp.exp(cum)).astype(BF16),
                                                       st[j].astype(BF16))
            k_out = (k * jnp.exp(last - cum)).astype(BF16)
            st[j] = jnp.exp(last) * st[j] + _dot_tn(v16, k_out)
            o_ref[rows, cols] = _gated_output(o, zg_ref[rows, cols], gain).astype(BF16)
    for j in range(hp):
        st_ref[j] = st[j]

    @pl.when(t == pl.num_programs(1) - 1)
    def _():
        for j in range(hp):
            s_ref[j] = st[j].T


def _gla_prompt(z, lb_logits, gain, layer, seq, tl=512, hp=GLA_HEADS_PER_STEP):
    heads = z.shape[1] // (4 * HEAD)
    hp = _unroll(heads, hp)
    tl = min(tl, seq)
    slots = lb_logits.shape[0]
    groups = heads // hp
    zspec = lambda off: pl.BlockSpec((tl, hp * HEAD), lambda h, t: (t, h + off * groups))
    return pl.pallas_call(
        functools.partial(_gla_prompt_kernel, layer=layer, hp=hp),
        grid=(groups, seq // tl),
        in_specs=[zspec(0), zspec(1), zspec(2), zspec(3),
                  pl.BlockSpec((slots, hp * HEAD), lambda h, t: (0, h)),
                  pl.BlockSpec((1, HEAD), lambda h, t: (0, 0)),
                  pl.BlockSpec(memory_space=pl.ANY)],
        out_specs=[pl.BlockSpec((tl, hp * HEAD), lambda h, t: (t, h)),
                   pl.BlockSpec((hp, HEAD, HEAD), lambda h, t: (h, 0, 0))],
        out_shape=[jax.ShapeDtypeStruct((z.shape[0], heads * HEAD), BF16),
                   jax.ShapeDtypeStruct((heads, HEAD, HEAD), F32)],
        scratch_shapes=[pltpu.VMEM((hp, HEAD, HEAD), F32)],
        input_output_aliases={6: 0},
        compiler_params=_params("parallel", "arbitrary"),
        name="gla_prompt",
    )(z, z, z, z, lb_logits, gain.reshape(1, HEAD), jnp.zeros((z.shape[0], heads * HEAD), BF16))


def _gla_sample_kernel(zq_ref, zf_ref, zi_ref, zg_ref, lbl_ref, gain_ref, s0_ref, o_in_ref,
                       o_ref, s_ref, last_ref, kout_ref, qin_ref, v_ref, *, layer, steps):
    del o_in_ref
    rows = zq_ref.shape[0]
    mid = (steps - 1) // 2
    lb = _lower_bound(lbl_ref[...], layer)
    q, k, g = _gate_inputs(zq_ref[...], zf_ref[...], lb)
    v16 = zi_ref[...].astype(BF16)
    cum = _cumsum_rows(g, seg=steps)

    row = lax.broadcasted_iota(jnp.int32, cum.shape, 0)
    pos = _div_mod(row, steps)[1]

    def spread(src_pos):
        picked = jnp.where(pos == src_pos, cum, 0.0)
        out = picked
        for d in range(steps):
            if d != src_pos:
                out = out + pltpu.roll(picked, (d - src_pos) % rows, 0)
        return out

    ref = spread(mid)
    last = spread(steps - 1)
    r2 = lax.broadcasted_iota(jnp.int32, (rows, rows), 0)
    c2 = lax.broadcasted_iota(jnp.int32, (rows, rows), 1)
    att = _dot_nt((q * jnp.exp(cum - ref)).astype(BF16), (k * jnp.exp(ref - cum)).astype(BF16))
    att = jnp.where((r2 >= c2) & (_div_mod(r2, steps)[0] == _div_mod(c2, steps)[0]), att, 0.0)
    o_intra = _dot(att.astype(BF16), v16)
    qin_ref[...] = (q * jnp.exp(cum)).astype(BF16)
    v_ref[...] = v16
    last_ref[...] = last
    kout_ref[...] = k * jnp.exp(last - cum)
    n_seq = s0_ref.shape[0]

    def body(b, o_inter):
        lo = b * steps
        mine = (row >= lo) & (row < lo + steps)
        st = s0_ref[b].T
        o_inter = o_inter + jnp.where(mine, _dot_nt(qin_ref[...], st.astype(BF16)), 0.0)
        k_out = jnp.where(mine, kout_ref[...], 0.0).astype(BF16)
        decay = jnp.exp(last_ref[pl.ds(lo, 1), :])
        s_ref[b] = (decay * st + _dot_tn(v_ref[...], k_out)).T
        return o_inter

    o_inter = lax.fori_loop(0, n_seq, body, jnp.zeros(cum.shape, F32),
                            unroll=_unroll(n_seq, SAMPLE_SEQ_UNROLL))
    o_ref[...] = _gated_output(o_intra + o_inter, zg_ref[...], gain_ref[...]).astype(BF16)


def _gla_sample(z, o_all, row0, n_seq, steps, s0, layer_s0, lb_logits, gain, layer):
    heads = z.shape[1] // (4 * HEAD)
    rows = n_seq * steps
    slots = lb_logits.shape[0]
    rb = row0 // rows
    zspec = lambda off: pl.BlockSpec((rows, HEAD), lambda h: (rb, h + off * heads))
    s_in = pl.BlockSpec((None, n_seq, None, HEAD, HEAD), lambda h: (layer_s0, 0, h, 0, 0))
    s_out = pl.BlockSpec((n_seq, None, HEAD, HEAD), lambda h: (0, h, 0, 0))
    return pl.pallas_call(
        functools.partial(_gla_sample_kernel, layer=layer, steps=steps),
        grid=(heads,),
        in_specs=[zspec(0), zspec(1), zspec(2), zspec(3),
                  pl.BlockSpec((slots, HEAD), lambda h: (0, h)),
                  pl.BlockSpec((1, HEAD), lambda h: (0, 0)),
                  s_in,
                  pl.BlockSpec(memory_space=pl.ANY)],
        out_specs=[pl.BlockSpec((rows, HEAD), lambda h: (rb, h)), s_out],
        out_shape=[jax.ShapeDtypeStruct(o_all.shape, o_all.dtype),
                   jax.ShapeDtypeStruct(s0.shape[1:], s0.dtype)],
        scratch_shapes=[pltpu.VMEM((rows, HEAD), F32), pltpu.VMEM((rows, HEAD), F32),
                        pltpu.VMEM((rows, HEAD), BF16), pltpu.VMEM((rows, HEAD), BF16)],
        input_output_aliases={7: 0},
        compiler_params=_params("parallel"),
        name="gla_sample",
    )(z, z, z, z, lb_logits, gain.reshape(1, HEAD), s0, o_all)


def _top_blocks_bits(gate_t, blk_f, n_valid):
    gate_t = jnp.where(blk_f < n_valid, gate_t, -jnp.inf)
    bits = jnp.zeros((1, gate_t.shape[1]), jnp.int32)
    for _ in range(MOBA_TOPK):
        mx = jnp.max(gate_t, axis=0, keepdims=True)
        idx = jnp.min(jnp.where(gate_t == mx, blk_f, float(LANES)), axis=0, keepdims=True)
        live = mx > -jnp.inf
        gate_t = jnp.where(blk_f == idx, -jnp.inf, gate_t)
        shift = jnp.minimum(idx, 31.0).astype(jnp.int32)
        bits = bits | jnp.where(live, jnp.left_shift(1, shift), 0)
    return bits


def _moba_prompt_kernel(q_ref, k_ref, v_ref, o_init_ref, o_ref, kmean_ref, qb_ref, sel_ref, m_ref,
                        l_ref, acc_ref, *, n_blocks, group, chunk):
    del o_init_ref
    qi = pl.program_id(1)
    blk = MOBA_BLOCK
    rows = group * blk
    n_chunks = rows // chunk

    @pl.when(qi == 0)
    def _():
        kmean_ref[...] = jnp.zeros_like(kmean_ref)
        for n in range(n_blocks):
            kmean_ref[n:n + 1, :] = jnp.mean(k_ref[n * blk:(n + 1) * blk, :], axis=0, keepdims=True)

    blk_f = lax.broadcasted_iota(jnp.int32, (kmean_ref.shape[0], blk), 0).astype(F32)
    qi_f = qi.astype(F32)
    for g in range(group):
        qg = q_ref[:, g * HEAD:(g + 1) * HEAD]
        qb_ref[g * blk:(g + 1) * blk, :] = (qg * SCALE_LOG2E).astype(BF16)
        gate_t = _dot_nt(kmean_ref[...], qg, precision=lax.Precision.HIGHEST)
        bits = _top_blocks_bits(gate_t, blk_f, qi_f)
        per_row = jnp.broadcast_to(bits, (SUBLANES, blk)).T
        sel_ref[g * blk:(g + 1) * blk, :] = jnp.broadcast_to(per_row[:, :1], (blk, LANES))

    own = pl.ds(pl.multiple_of(qi * blk, blk), blk)
    kd = k_ref[own, :].astype(BF16)
    vd = v_ref[own, :].astype(BF16)
    own_chunk = min(PROMPT_OWN_CHUNK, blk)
    for c in range(rows // own_chunk):
        r = slice(c * own_chunk, (c + 1) * own_chunk)
        q_pos = lax.broadcasted_iota(jnp.int32, (own_chunk, blk), 0) + (c * own_chunk) % blk
        k_pos = lax.broadcasted_iota(jnp.int32, (own_chunk, blk), 1)
        s = jnp.where(k_pos <= q_pos, _dot_nt(qb_ref[r, :], kd), NEG)
        m = jnp.max(s, axis=1, keepdims=True)
        p = jnp.exp2(s - m)
        m_ref[r, :] = jnp.broadcast_to(m, (own_chunk, LANES))
        l_ref[r, :] = jnp.broadcast_to(jnp.sum(p, axis=1, keepdims=True), (own_chunk, LANES))
        acc_ref[r, :] = _dot(p.astype(BF16), vd)

    def attend(first_pair, n_pairs):
        blocks = []
        for t in range(n_pairs):
            ja = 2 * (first_pair + t)
            jb = jnp.minimum(ja + 1, qi - 1)
            bit_a = jnp.left_shift(jnp.int32(1), ja)
            bit_b = jnp.where(ja + 1 < qi, jnp.left_shift(jnp.int32(1), jb), 0)
            rows_a = pl.ds(pl.multiple_of(ja * blk, blk), blk)
            rows_b = pl.ds(pl.multiple_of(jb * blk, blk), blk)
            blocks.append((bit_a, bit_b,
                           k_ref[rows_a, :].astype(BF16), k_ref[rows_b, :].astype(BF16),
                           v_ref[rows_a, :].astype(BF16), v_ref[rows_b, :].astype(BF16)))
        work = [(t, c) for t in range(n_pairs) for c in range(n_chunks)]

        def scores(item):
            t, c = item
            qc = qb_ref[c * chunk:(c + 1) * chunk, :]
            return _dot_nt(qc, blocks[t][2]), _dot_nt(qc, blocks[t][3])

        pending = [scores(w) for w in work[:SCORE_LOOKAHEAD]]
        for n, (t, c) in enumerate(work):
            bit_a, bit_b, _, _, va, vb = blocks[t]
            r = slice(c * chunk, (c + 1) * chunk)
            sel = sel_ref[r, :]
            on_a = (sel & bit_a) != 0
            on_b = (sel & bit_b) != 0
            sa, sb = pending.pop(0)
            if n + SCORE_LOOKAHEAD < len(work):
                pending.append(scores(work[n + SCORE_LOOKAHEAD]))
            parts = [jnp.where(on_a, sa[:, :LANES], NEG), jnp.where(on_a, sa[:, LANES:], NEG),
                     jnp.where(on_b, sb[:, :LANES], NEG), jnp.where(on_b, sb[:, LANES:], NEG)]
            m_prev = m_ref[r, :]
            top = jnp.maximum(jnp.maximum(parts[0], parts[1]), jnp.maximum(parts[2], parts[3]))
            m_new = jnp.maximum(m_prev, jnp.max(top, axis=1, keepdims=True))
            ps = [jnp.exp2(x - m_new) for x in parts]
            alpha = jnp.exp2(m_prev - m_new)
            l_ref[r, :] = alpha * l_ref[r, :] + jnp.sum((ps[0] + ps[1]) + (ps[2] + ps[3]), axis=1,
                                                        keepdims=True)
            pa = jnp.concatenate(ps[:2], axis=1).astype(BF16)
            pb = jnp.concatenate(ps[2:], axis=1).astype(BF16)
            acc_ref[r, :] = alpha * acc_ref[r, :] + (_dot(pa, va) + _dot(pb, vb))
            m_ref[r, :] = m_new

    n_pairs = (qi + 1) // 2
    trips = n_pairs // PAIRS_PER_TRIP

    def body(i, carry):
        attend(i * PAIRS_PER_TRIP, PAIRS_PER_TRIP)
        return carry

    lax.fori_loop(0, trips, body, 0)
    for left in range(1, PAIRS_PER_TRIP):
        @pl.when(n_pairs - trips * PAIRS_PER_TRIP == left)
        def _():
            attend(trips * PAIRS_PER_TRIP, left)

    for g in range(group):
        r = slice(g * blk, (g + 1) * blk)
        o_ref[:, g * HEAD:(g + 1) * HEAD] = (acc_ref[r, :] / l_ref[r, :]).astype(BF16)


def _moba_prompt(q, kv, seq, n_kv, chunk=PROMPT_ROW_CHUNK):
    heads = q.shape[1] // HEAD
    group = heads // n_kv
    n_blocks = seq // MOBA_BLOCK
    blk = MOBA_BLOCK
    rows = group * blk
    return pl.pallas_call(
        functools.partial(_moba_prompt_kernel, n_blocks=n_blocks, group=group, chunk=chunk),
        grid=(n_kv, n_blocks),
        in_specs=[pl.BlockSpec((blk, group * HEAD), lambda h, i: (i, h)),
                  pl.BlockSpec((seq, HEAD), lambda h, i: (0, h)),
                  pl.BlockSpec((seq, HEAD), lambda h, i: (0, n_kv + h)),
                  pl.BlockSpec(memory_space=pl.ANY)],
        out_specs=pl.BlockSpec((blk, group * HEAD), lambda h, i: (i, h)),
        out_shape=jax.ShapeDtypeStruct((q.shape[0], heads * HEAD), BF16),
        scratch_shapes=[pltpu.VMEM((-(-n_blocks // SUBLANES) * SUBLANES, HEAD), F32),
                        pltpu.VMEM((rows, HEAD), BF16),
                        pltpu.VMEM((rows, LANES), jnp.int32),
                        pltpu.VMEM((rows, LANES), F32),
                        pltpu.VMEM((rows, LANES), F32),
                        pltpu.VMEM((rows, HEAD), F32)],
        input_output_aliases={3: 0},
        compiler_params=_params("arbitrary", "arbitrary"),
        name="moba_prompt",
    )(q, kv, kv, jnp.zeros((q.shape[0], heads * HEAD), BF16))


def _moba_sample_kernel(pt_ref, q_ref, kn_ref, vn_ref, *refs, n_kv, group, steps, pages_per_block,
                        blocks_per_step):
    del pt_ref
    n_pages = pages_per_block * blocks_per_step
    kp = refs[:n_pages]
    vp = refs[n_pages:2 * n_pages]
    o_ref, m_s, l_s, g_s, o_s, bias_s = refs[2 * n_pages:]
    j = pl.program_id(1)
    rows = q_ref.shape[1]
    rq = group * steps
    tok_per_vreg = SUBLANES // n_kv
    n_col = bias_s.shape[1]

    @pl.when((pl.program_id(0) == 0) & (j == 0))
    def _():
        row_head = _div_mod(lax.broadcasted_iota(jnp.int32, (rows, n_col), 0), rq)[0]
        col_head = _div_mod(lax.broadcasted_iota(jnp.int32, (rows, n_col), 1), n_kv)[1]
        bias_s[...] = jnp.where(row_head == col_head, 0.0, NEG)

    qf = q_ref[0]
    q16 = qf.astype(BF16)
    head64 = _div_mod(lax.broadcasted_iota(jnp.int32, (rows, HEAD), 0), rq)[0]
    kbs, vbs, scores = [], [], []
    for t in range(blocks_per_step):
        pages = slice(t * pages_per_block, (t + 1) * pages_per_block)
        kbs.append(jnp.concatenate([r[0] for r in kp[pages]], axis=0))
        vbs.append(jnp.concatenate([r[0] for r in vp[pages]], axis=0))
        scores.append(_dot_nt(q16, kbs[t].astype(BF16)))
    for t in range(blocks_per_step):
        kb = kbs[t]
        s = scores[t] * SCALE + bias_s[...]
        m = jnp.max(s, axis=1, keepdims=True)
        p = jnp.exp(s - m)

        parts = [kb[SUBLANES * i:SUBLANES * (i + 1), :] for i in range(n_col // SUBLANES)]
        while len(parts) > 1:
            parts = [a + b for a, b in zip(parts[::2], parts[1::2])]
        folded = parts[0]
        kmean = jnp.zeros((rows, HEAD), F32)
        for h in range(n_kv):
            total = folded[h:h + 1, :]
            for i in range(1, tok_per_vreg):
                total = total + folded[h + i * n_kv:h + i * n_kv + 1, :]
            kmean = jnp.where(head64 == h, total * (1.0 / MOBA_BLOCK), kmean)
        gate = jnp.sum(qf * kmean, axis=1, keepdims=True)

        n = j * blocks_per_step + t
        m_s[n] = jnp.broadcast_to(m, (rows, LANES))
        l_s[n] = jnp.broadcast_to(jnp.sum(p, axis=1, keepdims=True), (rows, LANES))
        g_s[n] = jnp.broadcast_to(gate, (rows, LANES))
        o_s[n] = _dot(p.astype(BF16), vbs[t].astype(BF16))

    @pl.when(j == pl.num_programs(1) - 1)
    def _():
        gates = g_s[...]
        n_io = lax.broadcasted_iota(jnp.int32, gates.shape, 0)
        sel = jnp.zeros(gates.shape, jnp.bool_)
        for _ in range(min(MOBA_TOPK, g_s.shape[0])):
            mx = jnp.max(gates, axis=0, keepdims=True)
            idx = jnp.min(jnp.where(gates == mx, n_io, g_s.shape[0]), axis=0, keepdims=True)
            pick = n_io == idx
            sel = sel | pick
            gates = jnp.where(pick, -jnp.inf, gates)

        row = lax.broadcasted_iota(jnp.int32, (rows, LANES), 0)
        lane = lax.broadcasted_iota(jnp.int32, (rows, LANES), 1)
        q_head, q_step = _div_mod(row, rq)[0], _div_mod(row, steps)[1]
        k_step, k_head = _div_mod(lane, n_kv)
        ok = (q_head == k_head) & (k_step <= q_step) & (k_step < steps)
        s_own = jnp.where(ok, _dot_nt(q16, kn_ref[0].astype(BF16)) * SCALE, NEG)
        m_own = jnp.max(s_own, axis=1, keepdims=True)
        p_own = jnp.exp(s_own - m_own)
        l_own = jnp.sum(p_own, axis=1, keepdims=True)
        o_own = _dot(p_own.astype(BF16), vn_ref[0].astype(BF16))

        m_all = m_s[...]
        m_top = jnp.maximum(m_own, jnp.max(jnp.where(sel, m_all, -jnp.inf), axis=0))
        w = jnp.where(sel, jnp.exp(jnp.minimum(m_all - m_top[None], 0.0)), 0.0)
        w_own = jnp.exp(m_own - m_top)
        den = w_own * l_own + jnp.sum(w * l_s[...], axis=0)
        num = w_own * o_own + jnp.sum(w * o_s[...], axis=0)
        o_ref[0] = num / den


def _moba_sample(q, k_new, v_new, cache_k, cache_v, page_table, n_kv, steps):
    n_seq, n_pages = page_table.shape
    n_phys, page, _, _ = cache_k.shape
    ppb = MOBA_BLOCK // page
    n_blocks = n_pages // ppb
    bps = _unroll(n_blocks, SAMPLE_BLOCKS_PER_STEP)
    rows = q.shape[1]
    group = rows // (n_kv * steps)
    assert SUBLANES % n_kv == 0 and steps * n_kv <= LANES
    ck = cache_k.reshape(n_phys, page * n_kv, HEAD)
    cv = cache_v.reshape(n_phys, page * n_kv, HEAD)
    pages_per_step = ppb * bps

    def page_spec(p):
        return pl.BlockSpec((1, page * n_kv, HEAD),
                            lambda b, j, pt: (pt[b, j * pages_per_step + p], 0, 0))

    per_seq = lambda shape: pl.BlockSpec(shape, lambda b, j, pt: (b, 0, 0))
    grid_spec = pltpu.PrefetchScalarGridSpec(
        num_scalar_prefetch=1,
        grid=(n_seq, n_blocks // bps),
        in_specs=[per_seq((1, rows, HEAD)), per_seq((1, LANES, HEAD)), per_seq((1, LANES, HEAD))]
                 + [page_spec(p) for p in range(pages_per_step)] * 2,
        out_specs=per_seq((1, rows, HEAD)),
        scratch_shapes=[pltpu.VMEM((n_blocks, rows, LANES), F32) for _ in range(4)]
                       + [pltpu.VMEM((rows, MOBA_BLOCK * n_kv), F32)],
    )
    return pl.pallas_call(
        functools.partial(_moba_sample_kernel, n_kv=n_kv, group=group, steps=steps,
                          pages_per_block=ppb, blocks_per_step=bps),
        grid_spec=grid_spec,
        out_shape=jax.ShapeDtypeStruct((n_seq, rows, HEAD), F32),
        compiler_params=_params("arbitrary", "arbitrary"),
        name="moba_sample",
    )(page_table, q, k_new, v_new, *([ck] * pages_per_step), *([cv] * pages_per_step))


def kernel(x_prompt, x_sample, state_hgrn, cache_k, cache_v, page_table, norm_mix_a, w_in_a, lb_logits,
           onorm_a, w_out_a, norm_kv, w_kv, k_norm, norm_mix_b, w_q_b, q_norm, w_o_b, norm_ffn,
           w_gate_up, w_down):
    batch, seq, d = x_prompt.shape
    n_seq, steps, _ = x_sample.shape
    n_a = w_in_a.shape[0]
    depth = norm_ffn.shape[0]
    heads = d // HEAD
    n_kv = cache_k.shape[2]
    group = heads // n_kv
    kv_width = n_kv * HEAD
    n_dec = n_seq * steps
    past = page_table.shape[1] * cache_k.shape[1]
    assert batch == 1 and seq % MOBA_BLOCK == 0 and seq // MOBA_BLOCK <= 32
    assert past % MOBA_BLOCK == 0 and MOBA_BLOCK % cache_k.shape[1] == 0
    assert steps <= LANES and seq % n_dec == 0 and seq % GLA_CHUNK == 0

    h = jnp.concatenate([x_prompt.reshape(seq, d), x_sample.reshape(n_dec, d)], axis=0)
    states_p, states_s = [], []
    kv = None
    for layer in range(depth):
        if layer < n_a:
            z = _norm_matmul(h, norm_mix_a[layer], w_in_a, layer, name="hgrn_in")
            o, s_p = _gla_prompt(z, lb_logits, onorm_a[layer], layer, seq)
            o, s_s = _gla_sample(z, o, seq, n_seq, steps, state_hgrn, layer, lb_logits,
                                 onorm_a[layer], layer)
            h = _matmul_residual(o, w_out_a, layer, h, 512, "hgrn_out")
            states_p.append(s_p.reshape(1, heads, HEAD, HEAD).astype(state_hgrn.dtype))
            states_s.append(s_s)
        else:
            if kv is None:
                kv = _norm_matmul(h, norm_kv, w_kv, 0, head_gain=k_norm, norm_blocks=1,
                                  tn_cap=kv_width, name="kv_proj")
            jb = layer - n_a
            q = _norm_matmul(h, norm_mix_b[jb], w_q_b, jb, head_gain=q_norm[jb],
                             norm_blocks=heads, name="q_proj")
            att = _moba_prompt(q, kv, seq, n_kv)
            q_s = q[seq:].reshape(n_seq, steps, n_kv, group, HEAD).transpose(0, 2, 3, 1, 4)
            pad = ((0, 0), (0, LANES - steps * n_kv), (0, 0))
            k_s = jnp.pad(kv[seq:, :kv_width].reshape(n_seq, steps * n_kv, HEAD), pad)
            v_s = jnp.pad(kv[seq:, kv_width:].reshape(n_seq, steps * n_kv, HEAD), pad)
            att_s = _moba_sample(q_s.reshape(n_seq, n_kv * group * steps, HEAD), k_s, v_s,
                                 cache_k, cache_v, page_table, n_kv, steps)
            att_s = att_s.reshape(n_seq, n_kv, group, steps, HEAD).transpose(0, 3, 1, 2, 4)
            att = lax.dynamic_update_slice(att, att_s.reshape(n_dec, d).astype(BF16), (seq, 0))
            h = _matmul_residual(att, w_o_b, jb, h, 512, "attn_out")
        hf = _swiglu_up(h, norm_ffn[layer], w_gate_up, layer, "ffn_up")
        if layer + 1 < depth:
            h = _matmul_residual(hf, w_down, layer, h, 256, "ffn_down", tm_cap=FFN_ROW_TILE_CAP)
        else:
            y_p, y_s = _matmul_residual(hf, w_down, layer, h, 256, "ffn_down",
                                        tm_cap=FFN_ROW_TILE_CAP, head_rows=seq)

    k_new = kv[:, :kv_width]
    v_new = kv[:, kv_width:]
    return (y_p.reshape(batch, seq, d),
            y_s.reshape(n_seq, steps, d),
            jnp.stack(states_p),
            jnp.stack(states_s),
            k_new[:seq].reshape(batch, seq, n_kv, HEAD),
            v_new[:seq].reshape(batch, seq, n_kv, HEAD),
            k_new[seq:].reshape(n_seq, steps, n_kv, HEAD),
            v_new[seq:].reshape(n_seq, steps, n_kv, HEAD))
```

```python
import functools

import jax
import jax.numpy as jnp
from jax import lax
from jax.experimental import pallas as pl
from jax.experimental.pallas import tpu as pltpu

F32 = jnp.float32
BF16 = jnp.bfloat16

EPS = 1e-6
HEAD = 128
GLA_CHUNK = 64
MOBA_BLOCK = 256
MOBA_TOPK = 3
SCALE = HEAD ** -0.5
SCALE_LOG2E = SCALE * 1.4426950408889634
NEG = -1e30
LANES = 128
SUBLANES = 8
GLA_HEADS_PER_STEP = 4
NORM_ROWS = 16
NORM_UNROLL = 5
ROW_TILE_CAP = 2080
FFN_ROW_TILE_CAP = 1040
PROMPT_ROW_CHUNK = 256
PROMPT_OWN_CHUNK = 128
SCORE_LOOKAHEAD = 2
PAIRS_PER_TRIP = 4
SAMPLE_BLOCKS_PER_STEP = 8
SAMPLE_SEQ_UNROLL = 4
VMEM_LIMIT = 56 * 1024 * 1024


def _dot(a, b):
    return jnp.dot(a, b, preferred_element_type=F32)


def _dot_nt(a, b, precision=None):
    return lax.dot_general(a, b, (((1,), (1,)), ((), ())), precision=precision,
                           preferred_element_type=F32)


def _dot_tn(a, b):
    return lax.dot_general(a, b, (((0,), (0,)), ((), ())), preferred_element_type=F32)


def _div_mod(x, n):
    if n & (n - 1) == 0:
        return jnp.right_shift(x, n.bit_length() - 1), x & (n - 1)
    q = x // n
    return q, x - q * n


def _sigmoid(x):
    return 1.0 / (1.0 + jnp.exp(-x))


def _rms(x, g):
    return x * lax.rsqrt(jnp.mean(x * x, axis=-1, keepdims=True) + EPS) * g


def _row_tile(m, cap=ROW_TILE_CAP):
    for t in range(min(m, cap), 0, -1):
        if m % t == 0 and t % NORM_ROWS == 0:
            return t
    raise ValueError(f"no row tile for {m} rows")


def _col_tile(n, cap):
    for t in range(min(n, cap), 0, -1):
        if n % t == 0 and t % LANES == 0:
            return t
    raise ValueError(f"no column tile for {n} columns")


def _unroll(trips, cap):
    for u in range(min(trips, cap), 0, -1):
        if trips % u == 0:
            return u
    return 1


def _params(*sem):
    return pltpu.CompilerParams(dimension_semantics=sem, vmem_limit_bytes=VMEM_LIMIT)


def _normalise_rows(x_ref, g_ref, xn_ref, dst_row, n_rows):
    g = g_ref[...]
    trips = n_rows // NORM_ROWS

    def body(i, carry):
        src = pl.ds(pl.multiple_of(i * NORM_ROWS, NORM_ROWS), NORM_ROWS)
        dst = pl.ds(pl.multiple_of(dst_row + i * NORM_ROWS, NORM_ROWS), NORM_ROWS)
        xn_ref[dst, :] = _rms(x_ref[src, :], g).astype(BF16)
        return carry

    lax.fori_loop(0, trips, body, 0, unroll=_unroll(trips, NORM_UNROLL))


def _norm_matmul_kernel(x_ref, tail_ref, g_ref, w_ref, hg_ref, o_ref, xn_ref, *, norm_blocks,
                        tail_start):
    j = pl.program_id(1)

    if tail_start is None:
        @pl.when(j == 0)
        def _():
            _normalise_rows(x_ref, g_ref, xn_ref, 0, x_ref.shape[0])
    else:
        last = pl.program_id(0) == pl.num_programs(0) - 1

        @pl.when((j == 0) & jnp.logical_not(last))
        def _():
            _normalise_rows(x_ref, g_ref, xn_ref, 0, x_ref.shape[0])

        @pl.when((j == 0) & last)
        def _():
            _normalise_rows(x_ref, g_ref, xn_ref, 0, tail_start)
            _normalise_rows(tail_ref, g_ref, xn_ref, tail_start, tail_ref.shape[0])

    acc = _dot(xn_ref[...], w_ref[...].astype(BF16))
    if norm_blocks == 0:
        o_ref[...] = acc
        return

    @pl.when(j < norm_blocks)
    def _():
        for h in range(acc.shape[1] // HEAD):
            cols = slice(h * HEAD, (h + 1) * HEAD)
            o_ref[:, cols] = _rms(acc[:, cols], hg_ref[...])

    @pl.when(j >= norm_blocks)
    def _():
        o_ref[...] = acc


def _norm_matmul(x, g, w, layer, head_gain=None, norm_blocks=0, tn_cap=512, name="norm_matmul",
                 tail=None):
    d = x.shape[1]
    m = x.shape[0] + (0 if tail is None else tail.shape[0])
    n = w.shape[-1]
    tm, tn = _row_tile(m), _col_tile(n, tn_cap)
    if head_gain is None:
        head_gain = jnp.ones((HEAD,), F32)
    if w.ndim == 2:
        w = w[None]
    tail_start = None
    if tail is not None:
        tail_start = x.shape[0] - (m // tm - 1) * tm
        assert tail_start > 0 and tail_start + tail.shape[0] == tm
        assert tail_start % NORM_ROWS == 0 and tail.shape[0] % NORM_ROWS == 0
    else:
        tail = jnp.zeros((NORM_ROWS, d), x.dtype)
    return pl.pallas_call(
        functools.partial(_norm_matmul_kernel, norm_blocks=norm_blocks, tail_start=tail_start),
        grid=(m // tm, n // tn),
        in_specs=[
            pl.BlockSpec((tm, d), lambda i, j: (i, 0), pipeline_mode=pl.Buffered(1)),
            pl.BlockSpec(tail.shape, lambda i, j: (0, 0)),
            pl.BlockSpec((1, d), lambda i, j: (0, 0)),
            pl.BlockSpec((None, d, tn), lambda i, j: (layer, 0, j)),
            pl.BlockSpec((1, HEAD), lambda i, j: (0, 0)),
        ],
        out_specs=pl.BlockSpec((tm, tn), lambda i, j: (i, j)),
        out_shape=jax.ShapeDtypeStruct((m, n), F32),
        scratch_shapes=[pltpu.VMEM((tm, d), BF16)],
        compiler_params=_params("parallel", "arbitrary"),
        name=name,
    )(x, tail, g.reshape(1, d), w, head_gain.reshape(1, HEAD))


def _swiglu_up_kernel(x_ref, g_ref, wa_ref, wu_ref, o_ref, xn_ref):
    @pl.when(pl.program_id(1) == 0)
    def _():
        _normalise_rows(x_ref, g_ref, xn_ref, 0, x_ref.shape[0])

    xn = xn_ref[...]
    a = _dot(xn, wa_ref[...].astype(BF16))
    u = _dot(xn, wu_ref[...].astype(BF16))
    o_ref[...] = (a * _sigmoid(a) * u).astype(BF16)


def _swiglu_up(x, g, w_gu, layer, name):
    m, d = x.shape
    d_ff = w_gu.shape[-1] // 2
    tm, tn = _row_tile(m, FFN_ROW_TILE_CAP), _col_tile(d_ff, 512)
    nblk = d_ff // tn
    return pl.pallas_call(
        _swiglu_up_kernel,
        grid=(m // tm, nblk),
        in_specs=[
            pl.BlockSpec((tm, d), lambda i, j: (i, 0)),
            pl.BlockSpec((1, d), lambda i, j: (0, 0)),
            pl.BlockSpec((None, d, tn), lambda i, j: (layer, 0, j)),
            pl.BlockSpec((None, d, tn), lambda i, j: (layer, 0, j + nblk)),
        ],
        out_specs=pl.BlockSpec((tm, tn), lambda i, j: (i, j)),
        out_shape=jax.ShapeDtypeStruct((m, d_ff), BF16),
        scratch_shapes=[pltpu.VMEM((tm, d), BF16)],
        compiler_params=_params("parallel", "arbitrary"),
        name=name,
    )(x, g.reshape(1, d), w_gu, w_gu)


def _matmul_residual_kernel(a_ref, w_ref, r_ref, o_ref):
    o_ref[...] = r_ref[...] + _dot(a_ref[...], w_ref[...].astype(BF16))


def _matmul_residual_joined_kernel(a_ref, w_ref, r_ref, rt_ref, o_ref, *, tail_start):
    acc = _dot(a_ref[...], w_ref[...].astype(BF16))
    last = pl.program_id(0) == pl.num_programs(0) - 1

    @pl.when(jnp.logical_not(last))
    def _():
        o_ref[...] = r_ref[...] + acc

    @pl.when(last)
    def _():
        o_ref[:tail_start, :] = r_ref[:tail_start, :] + acc[:tail_start, :]
        o_ref[tail_start:, :] = rt_ref[...] + acc[tail_start:, :]


def _matmul_residual_split_kernel(a_ref, w_ref, r_ref, head_ref, tail_ref, *, tail_start):
    out = r_ref[...] + _dot(a_ref[...], w_ref[...].astype(BF16))
    head_ref[...] = out
    last = pl.program_id(0) == pl.num_programs(0) - 1

    @pl.when(last)
    def _():
        tail_ref[...] = out[tail_start:tail_start + tail_ref.shape[0], :]

    @pl.when(jnp.logical_not(last))
    def _():
        tail_ref[...] = jnp.zeros_like(tail_ref)


def _matmul_residual(a, w, layer, res, tn_cap, name, tm_cap=ROW_TILE_CAP, head_rows=None,
                     res_tail=None):
    m, k = a.shape
    n = w.shape[-1]
    tm, tn = _row_tile(m, tm_cap), _col_tile(n, tn_cap)
    if res_tail is not None:
        tail_start = res.shape[0] - (m // tm - 1) * tm
        assert tail_start > 0 and tail_start + res_tail.shape[0] == tm
        assert tail_start % SUBLANES == 0 and head_rows is None
        return pl.pallas_call(
            functools.partial(_matmul_residual_joined_kernel, tail_start=tail_start),
            grid=(m // tm, n // tn),
            in_specs=[
                pl.BlockSpec((tm, k), lambda i, j: (i, 0)),
                pl.BlockSpec((None, k, tn), lambda i, j: (layer, 0, j)),
                pl.BlockSpec((tm, tn), lambda i, j: (i, j)),
                pl.BlockSpec((res_tail.shape[0], tn), lambda i, j: (0, j)),
            ],
            out_specs=pl.BlockSpec((tm, tn), lambda i, j: (i, j)),
            out_shape=jax.ShapeDtypeStruct((m, n), F32),
            compiler_params=_params("parallel", "arbitrary"),
            name=name,
        )(a, w, res, res_tail)
    in_specs = [
        pl.BlockSpec((tm, k), lambda i, j: (i, 0)),
        pl.BlockSpec((None, k, tn), lambda i, j: (layer, 0, j)),
        pl.BlockSpec((tm, tn), lambda i, j: (i, j)),
    ]
    if head_rows is None:
        return pl.pallas_call(
            _matmul_residual_kernel,
            grid=(m // tm, n // tn),
            in_specs=in_specs,
            out_specs=pl.BlockSpec((tm, tn), lambda i, j: (i, j)),
            out_shape=jax.ShapeDtypeStruct((m, n), F32),
            compiler_params=_params("parallel", "arbitrary"),
            name=name,
        )(a, w, res)
    tail_rows = m - head_rows
    tail_start = head_rows - (m // tm - 1) * tm
    assert 0 < tail_start and tail_start + tail_rows == tm and tail_start % SUBLANES == 0
    head, tail = pl.pallas_call(
        functools.partial(_matmul_residual_split_kernel, tail_start=tail_start),
        grid=(m // tm, n // tn),
        in_specs=in_specs,
        out_specs=[pl.BlockSpec((tm, tn), lambda i, j: (i, j)),
                   pl.BlockSpec((tail_rows, tn), lambda i, j: (i, j))],
        out_shape=[jax.ShapeDtypeStruct((head_rows, n), F32),
                   jax.ShapeDtypeStruct((m // tm * tail_rows, n), F32)],
        compiler_params=_params("parallel", "arbitrary"),
        name=name,
    )(a, w, res)
    return head, tail[(m // tm - 1) * tail_rows:]


def _lower_bound(logits, layer):
    e = jnp.exp(logits - jnp.max(logits, axis=0, keepdims=True))
    return jnp.sum(e[:layer + 1], axis=0, keepdims=True) / jnp.sum(e, axis=0, keepdims=True)


def _gate_inputs(zq, zf, lb):
    f = lb + (1.0 - lb) * _sigmoid(zf)
    return zq * _sigmoid(zq), 1.0 - f, jnp.log(f)


def _gated_output(o, zg, gain):
    return _rms(o, gain) * (zg * _sigmoid(zg))


def _cumsum_rows(g, seg=None):
    n = g.shape[0]
    row = lax.broadcasted_iota(jnp.int32, g.shape, 0)
    pos = row if seg is None else _div_mod(row, seg)[1]
    span = n if seg is None else seg
    s = 1
    while s < span:
        g = g + jnp.where(pos >= s, pltpu.roll(g, s, 0), 0.0)
        s *= 2
    return g


def _gla_prompt_kernel(zq_ref, zf_ref, zi_ref, zg_ref, lbl_ref, gain_ref, o_init_ref, o_ref, s_ref,
                       st_ref, *, layer, hp):
    del o_init_ref
    t = pl.program_id(1)
    c = GLA_CHUNK
    mid = (c - 1) // 2

    @pl.when(t == 0)
    def _():
        st_ref[...] = jnp.zeros_like(st_ref)

    lb_all = _lower_bound(lbl_ref[...], layer)
    gain = gain_ref[...]
    r2 = lax.broadcasted_iota(jnp.int32, (c, c), 0)
    c2 = lax.broadcasted_iota(jnp.int32, (c, c), 1)
    st = [st_ref[j] for j in range(hp)]
    for i in range(o_ref.shape[0] // c):
        rows = slice(i * c, (i + 1) * c)
        for j in range(hp):
            cols = slice(j * HEAD, (j + 1) * HEAD)
            q, k, g = _gate_inputs(zq_ref[rows, cols], zf_ref[rows, cols], lb_all[:, cols])
            v16 = zi_ref[rows, cols].astype(BF16)
            cum = _cumsum_rows(g)
            ref = cum[mid:mid + 1, :]
            last = cum[c - 1:c, :]
            att = _dot_nt((q * jnp.exp(cum - ref)).astype(BF16),
                          (k * jnp.exp(ref - cum)).astype(BF16))
            att = jnp.where(r2 >= c2, att, 0.0)
            o = _dot(att.astype(BF16), v16) + _dot_nt((q * jnp.exp(cum)).astype(BF16),
                                                       st[j].astype(BF16))
            k_out = (k * jnp.exp(last - cum)).astype(BF16)
            st[j] = jnp.exp(last) * st[j] + _dot_tn(v16, k_out)
            o_ref[rows, cols] = _gated_output(o, zg_ref[rows, cols], gain).astype(BF16)
    for j in range(hp):
        st_ref[j] = st[j]

    @pl.when(t == pl.num_programs(1) - 1)
    def _():
        for j in range(hp):
            s_ref[j] = st[j].T


def _gla_prompt(z, lb_logits, gain, layer, seq, tl=512, hp=GLA_HEADS_PER_STEP):
    heads = z.shape[1] // (4 * HEAD)
    hp = _unroll(heads, hp)
    tl = min(tl, seq)
    slots = lb_logits.shape[0]
    groups = heads // hp
    zspec = lambda off: pl.BlockSpec((tl, hp * HEAD), lambda h, t: (t, h + off * groups))
    return pl.pallas_call(
        functools.partial(_gla_prompt_kernel, layer=layer, hp=hp),
        grid=(groups, seq // tl),
        in_specs=[zspec(0), zspec(1), zspec(2), zspec(3),
                  pl.BlockSpec((slots, hp * HEAD), lambda h, t: (0, h)),
                  pl.BlockSpec((1, HEAD), lambda h, t: (0, 0)),
                  pl.BlockSpec(memory_space=pl.ANY)],
        out_specs=[pl.BlockSpec((tl, hp * HEAD), lambda h, t: (t, h)),
                   pl.BlockSpec((hp, HEAD, HEAD), lambda h, t: (h, 0, 0))],
        out_shape=[jax.ShapeDtypeStruct((z.shape[0], heads * HEAD), BF16),
                   jax.ShapeDtypeStruct((heads, HEAD, HEAD), F32)],
        scratch_shapes=[pltpu.VMEM((hp, HEAD, HEAD), F32)],
        input_output_aliases={6: 0},
        compiler_params=_params("parallel", "arbitrary"),
        name="gla_prompt",
    )(z, z, z, z, lb_logits, gain.reshape(1, HEAD), jnp.zeros((z.shape[0], heads * HEAD), BF16))


def _gla_sample_kernel(zq_ref, zf_ref, zi_ref, zg_ref, lbl_ref, gain_ref, s0_ref, o_in_ref,
                       o_ref, s_ref, last_ref, kout_ref, qin_ref, v_ref, *, layer, steps):
    del o_in_ref
    rows = zq_ref.shape[0]
    mid = (steps - 1) // 2
    lb = _lower_bound(lbl_ref[...], layer)
    q, k, g = _gate_inputs(zq_ref[...], zf_ref[...], lb)
    v16 = zi_ref[...].astype(BF16)
    cum = _cumsum_rows(g, seg=steps)

    row = lax.broadcasted_iota(jnp.int32, cum.shape, 0)
    pos = _div_mod(row, steps)[1]

    def spread(src_pos):
        picked = jnp.where(pos == src_pos, cum, 0.0)
        out = picked
        for d in range(steps):
            if d != src_pos:
                out = out + pltpu.roll(picked, (d - src_pos) % rows, 0)
        return out

    ref = spread(mid)
    last = spread(steps - 1)
    r2 = lax.broadcasted_iota(jnp.int32, (rows, rows), 0)
    c2 = lax.broadcasted_iota(jnp.int32, (rows, rows), 1)
    att = _dot_nt((q * jnp.exp(cum - ref)).astype(BF16), (k * jnp.exp(ref - cum)).astype(BF16))
    att = jnp.where((r2 >= c2) & (_div_mod(r2, steps)[0] == _div_mod(c2, steps)[0]), att, 0.0)
    o_intra = _dot(att.astype(BF16), v16)
    qin_ref[...] = (q * jnp.exp(cum)).astype(BF16)
    v_ref[...] = v16
    last_ref[...] = last
    kout_ref[...] = k * jnp.exp(last - cum)
    n_seq = s0_ref.shape[0]

    def body(b, o_inter):
        lo = b * steps
        mine = (row >= lo) & (row < lo + steps)
        st = s0_ref[b].T
        o_inter = o_inter + jnp.where(mine, _dot_nt(qin_ref[...], st.astype(BF16)), 0.0)
        k_out = jnp.where(mine, kout_ref[...], 0.0).astype(BF16)
        decay = jnp.exp(last_ref[pl.ds(lo, 1), :])
        s_ref[b] = (decay * st + _dot_tn(v_ref[...], k_out)).T
        return o_inter

    o_inter = lax.fori_loop(0, n_seq, body, jnp.zeros(cum.shape, F32),
                            unroll=_unroll(n_seq, SAMPLE_SEQ_UNROLL))
    o_ref[...] = _gated_output(o_intra + o_inter, zg_ref[...], gain_ref[...]).astype(BF16)


def _gla_sample(z, o_all, row0, n_seq, steps, s0, layer_s0, lb_logits, gain, layer):
    heads = z.shape[1] // (4 * HEAD)
    rows = n_seq * steps
    slots = lb_logits.shape[0]
    rb = row0 // rows
    zspec = lambda off: pl.BlockSpec((rows, HEAD), lambda h: (rb, h + off * heads))
    s_in = pl.BlockSpec((None, n_seq, None, HEAD, HEAD), lambda h: (layer_s0, 0, h, 0, 0))
    s_out = pl.BlockSpec((n_seq, None, HEAD, HEAD), lambda h: (0, h, 0, 0))
    return pl.pallas_call(
        functools.partial(_gla_sample_kernel, layer=layer, steps=steps),
        grid=(heads,),
        in_specs=[zspec(0), zspec(1), zspec(2), zspec(3),
                  pl.BlockSpec((slots, HEAD), lambda h: (0, h)),
                  pl.BlockSpec((1, HEAD), lambda h: (0, 0)),
                  s_in,
                  pl.BlockSpec(memory_space=pl.ANY)],
        out_specs=[pl.BlockSpec((rows, HEAD), lambda h: (rb, h)), s_out],
        out_shape=[jax.ShapeDtypeStruct(o_all.shape, o_all.dtype),
                   jax.ShapeDtypeStruct(s0.shape[1:], s0.dtype)],
        scratch_shapes=[pltpu.VMEM((rows, HEAD), F32), pltpu.VMEM((rows, HEAD), F32),
                        pltpu.VMEM((rows, HEAD), BF16), pltpu.VMEM((rows, HEAD), BF16)],
        input_output_aliases={7: 0},
        compiler_params=_params("parallel"),
        name="gla_sample",
    )(z, z, z, z, lb_logits, gain.reshape(1, HEAD), s0, o_all)


def _top_blocks_bits(gate_t, blk_f, n_valid):
    gate_t = jnp.where(blk_f < n_valid, gate_t, -jnp.inf)
    bits = jnp.zeros((1, gate_t.shape[1]), jnp.int32)
    for _ in range(MOBA_TOPK):
        mx = jnp.max(gate_t, axis=0, keepdims=True)
        idx = jnp.min(jnp.where(gate_t == mx, blk_f, float(LANES)), axis=0, keepdims=True)
        live = mx > -jnp.inf
        gate_t = jnp.where(blk_f == idx, -jnp.inf, gate_t)
        shift = jnp.minimum(idx, 31.0).astype(jnp.int32)
        bits = bits | jnp.where(live, jnp.left_shift(1, shift), 0)
    return bits


def _moba_prompt_kernel(q_ref, k_ref, v_ref, o_init_ref, o_ref, kmean_ref, qb_ref, sel_ref, m_ref,
                        l_ref, acc_ref, *, n_blocks, group, chunk):
    del o_init_ref
    qi = pl.program_id(1)
    blk = MOBA_BLOCK
    rows = group * blk
    n_chunks = rows // chunk

    @pl.when(qi == 0)
    def _():
        kmean_ref[...] = jnp.zeros_like(kmean_ref)
        for n in range(n_blocks):
            kmean_ref[n:n + 1, :] = jnp.mean(k_ref[n * blk:(n + 1) * blk, :], axis=0, keepdims=True)

    blk_f = lax.broadcasted_iota(jnp.int32, (kmean_ref.shape[0], blk), 0).astype(F32)
    qi_f = qi.astype(F32)
    for g in range(group):
        qg = q_ref[:, g * HEAD:(g + 1) * HEAD]
        qb_ref[g * blk:(g + 1) * blk, :] = (qg * SCALE_LOG2E).astype(BF16)
        gate_t = _dot_nt(kmean_ref[...], qg, precision=lax.Precision.HIGHEST)
        bits = _top_blocks_bits(gate_t, blk_f, qi_f)
        per_row = jnp.broadcast_to(bits, (SUBLANES, blk)).T
        sel_ref[g * blk:(g + 1) * blk, :] = jnp.broadcast_to(per_row[:, :1], (blk, LANES))

    own = pl.ds(pl.multiple_of(qi * blk, blk), blk)
    kd = k_ref[own, :].astype(BF16)
    vd = v_ref[own, :].astype(BF16)
    own_chunk = min(PROMPT_OWN_CHUNK, blk)
    for c in range(rows // own_chunk):
        r = slice(c * own_chunk, (c + 1) * own_chunk)
        q_pos = lax.broadcasted_iota(jnp.int32, (own_chunk, blk), 0) + (c * own_chunk) % blk
        k_pos = lax.broadcasted_iota(jnp.int32, (own_chunk, blk), 1)
        s = jnp.where(k_pos <= q_pos, _dot_nt(qb_ref[r, :], kd), NEG)
        m = jnp.max(s, axis=1, keepdims=True)
        p = jnp.exp2(s - m)
        m_ref[r, :] = jnp.broadcast_to(m, (own_chunk, LANES))
        l_ref[r, :] = jnp.broadcast_to(jnp.sum(p, axis=1, keepdims=True), (own_chunk, LANES))
        acc_ref[r, :] = _dot(p.astype(BF16), vd)

    def attend(first_pair, n_pairs):
        blocks = []
        for t in range(n_pairs):
            ja = 2 * (first_pair + t)
            jb = jnp.minimum(ja + 1, qi - 1)
            bit_a = jnp.left_shift(jnp.int32(1), ja)
            bit_b = jnp.where(ja + 1 < qi, jnp.left_shift(jnp.int32(1), jb), 0)
            rows_a = pl.ds(pl.multiple_of(ja * blk, blk), blk)
            rows_b = pl.ds(pl.multiple_of(jb * blk, blk), blk)
            blocks.append((bit_a, bit_b,
                           k_ref[rows_a, :].astype(BF16), k_ref[rows_b, :].astype(BF16),
                           v_ref[rows_a, :].astype(BF16), v_ref[rows_b, :].astype(BF16)))
        work = [(t, c) for t in range(n_pairs) for c in range(n_chunks)]

        def scores(item):
            t, c = item
            qc = qb_ref[c * chunk:(c + 1) * chunk, :]
            return _dot_nt(qc, blocks[t][2]), _dot_nt(qc, blocks[t][3])

        pending = [scores(w) for w in work[:SCORE_LOOKAHEAD]]
        for n, (t, c) in enumerate(work):
            bit_a, bit_b, _, _, va, vb = blocks[t]
            r = slice(c * chunk, (c + 1) * chunk)
            sel = sel_ref[r, :]
            on_a = (sel & bit_a) != 0
            on_b = (sel & bit_b) != 0
            sa, sb = pending.pop(0)
            if n + SCORE_LOOKAHEAD < len(work):
                pending.append(scores(work[n + SCORE_LOOKAHEAD]))
            parts = [jnp.where(on_a, sa[:, :LANES], NEG), jnp.where(on_a, sa[:, LANES:], NEG),
                     jnp.where(on_b, sb[:, :LANES], NEG), jnp.where(on_b, sb[:, LANES:], NEG)]
            m_prev = m_ref[r, :]
            top = jnp.maximum(jnp.maximum(parts[0], parts[1]), jnp.maximum(parts[2], parts[3]))
            m_new = jnp.maximum(m_prev, jnp.max(top, axis=1, keepdims=True))
            ps = [jnp.exp2(x - m_new) for x in parts]
            alpha = jnp.exp2(m_prev - m_new)
            l_ref[r, :] = alpha * l_ref[r, :] + jnp.sum((ps[0] + ps[1]) + (ps[2] + ps[3]), axis=1,
                                                        keepdims=True)
            pa = jnp.concatenate(ps[:2], axis=1).astype(BF16)
            pb = jnp.concatenate(ps[2:], axis=1).astype(BF16)
            acc_ref[r, :] = alpha * acc_ref[r, :] + (_dot(pa, va) + _dot(pb, vb))
            m_ref[r, :] = m_new

    n_pairs = (qi + 1) // 2
    trips = n_pairs // PAIRS_PER_TRIP

    def body(i, carry):
        attend(i * PAIRS_PER_TRIP, PAIRS_PER_TRIP)
        return carry

    lax.fori_loop(0, trips, body, 0)
    for left in range(1, PAIRS_PER_TRIP):
        @pl.when(n_pairs - trips * PAIRS_PER_TRIP == left)
        def _():
            attend(trips * PAIRS_PER_TRIP, left)

    for g in range(group):
        r = slice(g * blk, (g + 1) * blk)
        o_ref[:, g * HEAD:(g + 1) * HEAD] = (acc_ref[r, :] / l_ref[r, :]).astype(BF16)


def _moba_prompt(q, kv, seq, n_kv, chunk=PROMPT_ROW_CHUNK):
    heads = q.shape[1] // HEAD
    group = heads // n_kv
    n_blocks = seq // MOBA_BLOCK
    blk = MOBA_BLOCK
    rows = group * blk
    return pl.pallas_call(
        functools.partial(_moba_prompt_kernel, n_blocks=n_blocks, group=group, chunk=chunk),
        grid=(n_kv, n_blocks),
        in_specs=[pl.BlockSpec((blk, group * HEAD), lambda h, i: (i, h)),
                  pl.BlockSpec((seq, HEAD), lambda h, i: (0, h)),
                  pl.BlockSpec((seq, HEAD), lambda h, i: (0, n_kv + h)),
                  pl.BlockSpec(memory_space=pl.ANY)],
        out_specs=pl.BlockSpec((blk, group * HEAD), lambda h, i: (i, h)),
        out_shape=jax.ShapeDtypeStruct((q.shape[0], heads * HEAD), BF16),
        scratch_shapes=[pltpu.VMEM((-(-n_blocks // SUBLANES) * SUBLANES, HEAD), F32),
                        pltpu.VMEM((rows, HEAD), BF16),
                        pltpu.VMEM((rows, LANES), jnp.int32),
                        pltpu.VMEM((rows, LANES), F32),
                        pltpu.VMEM((rows, LANES), F32),
                        pltpu.VMEM((rows, HEAD), F32)],
        input_output_aliases={3: 0},
        compiler_params=_params("arbitrary", "arbitrary"),
        name="moba_prompt",
    )(q, kv, kv, jnp.zeros((q.shape[0], heads * HEAD), BF16))


def _moba_sample_kernel(pt_ref, q_ref, kn_ref, vn_ref, *refs, n_kv, group, steps, pages_per_block,
                        blocks_per_step):
    del pt_ref
    n_pages = pages_per_block * blocks_per_step
    kp = refs[:n_pages]
    vp = refs[n_pages:2 * n_pages]
    o_ref, m_s, l_s, g_s, o_s, bias_s = refs[2 * n_pages:]
    j = pl.program_id(1)
    rows = q_ref.shape[1]
    rq = group * steps
    tok_per_vreg = SUBLANES // n_kv
    n_col = bias_s.shape[1]

    @pl.when((pl.program_id(0) == 0) & (j == 0))
    def _():
        row_head = _div_mod(lax.broadcasted_iota(jnp.int32, (rows, n_col), 0), rq)[0]
        col_head = _div_mod(lax.broadcasted_iota(jnp.int32, (rows, n_col), 1), n_kv)[1]
        bias_s[...] = jnp.where(row_head == col_head, 0.0, NEG)

    qf = q_ref[0]
    q16 = qf.astype(BF16)
    head64 = _div_mod(lax.broadcasted_iota(jnp.int32, (rows, HEAD), 0), rq)[0]
    kbs, vbs, scores = [], [], []
    for t in range(blocks_per_step):
        pages = slice(t * pages_per_block, (t + 1) * pages_per_block)
        kbs.append(jnp.concatenate([r[0] for r in kp[pages]], axis=0))
        vbs.append(jnp.concatenate([r[0] for r in vp[pages]], axis=0))
        scores.append(_dot_nt(q16, kbs[t].astype(BF16)))
    for t in range(blocks_per_step):
        kb = kbs[t]
        s = scores[t] * SCALE + bias_s[...]
        m = jnp.max(s, axis=1, keepdims=True)
        p = jnp.exp(s - m)

        parts = [kb[SUBLANES * i:SUBLANES * (i + 1), :] for i in range(n_col // SUBLANES)]
        while len(parts) > 1:
            parts = [a + b for a, b in zip(parts[::2], parts[1::2])]
        folded = parts[0]
        kmean = jnp.zeros((rows, HEAD), F32)
        for h in range(n_kv):
            total = folded[h:h + 1, :]
            for i in range(1, tok_per_vreg):
                total = total + folded[h + i * n_kv:h + i * n_kv + 1, :]
            kmean = jnp.where(head64 == h, total * (1.0 / MOBA_BLOCK), kmean)
        gate = jnp.sum(qf * kmean, axis=1, keepdims=True)

        n = j * blocks_per_step + t
        m_s[n] = jnp.broadcast_to(m, (rows, LANES))
        l_s[n] = jnp.broadcast_to(jnp.sum(p, axis=1, keepdims=True), (rows, LANES))
        g_s[n] = jnp.broadcast_to(gate, (rows, LANES))
        o_s[n] = _dot(p.astype(BF16), vbs[t].astype(BF16))

    @pl.when(j == pl.num_programs(1) - 1)
    def _():
        gates = g_s[...]
        n_io = lax.broadcasted_iota(jnp.int32, gates.shape, 0)
        sel = jnp.zeros(gates.shape, jnp.bool_)
        for _ in range(min(MOBA_TOPK, g_s.shape[0])):
            mx = jnp.max(gates, axis=0, keepdims=True)
            idx = jnp.min(jnp.where(gates == mx, n_io, g_s.shape[0]), axis=0, keepdims=True)
            pick = n_io == idx
            sel = sel | pick
            gates = jnp.where(pick, -jnp.inf, gates)

        row = lax.broadcasted_iota(jnp.int32, (rows, LANES), 0)
        lane = lax.broadcasted_iota(jnp.int32, (rows, LANES), 1)
        q_head, q_step = _div_mod(row, rq)[0], _div_mod(row, steps)[1]
        k_step, k_head = _div_mod(lane, n_kv)
        ok = (q_head == k_head) & (k_step <= q_step) & (k_step < steps)
        s_own = jnp.where(ok, _dot_nt(q16, kn_ref[0].astype(BF16)) * SCALE, NEG)
        m_own = jnp.max(s_own, axis=1, keepdims=True)
        p_own = jnp.exp(s_own - m_own)
        l_own = jnp.sum(p_own, axis=1, keepdims=True)
        o_own = _dot(p_own.astype(BF16), vn_ref[0].astype(BF16))

        m_all = m_s[...]
        m_top = jnp.maximum(m_own, jnp.max(jnp.where(sel, m_all, -jnp.inf), axis=0))
        w = jnp.where(sel, jnp.exp(jnp.minimum(m_all - m_top[None], 0.0)), 0.0)
        w_own = jnp.exp(m_own - m_top)
        den = w_own * l_own + jnp.sum(w * l_s[...], axis=0)
        num = w_own * o_own + jnp.sum(w * o_s[...], axis=0)
        o_ref[0] = num / den


def _moba_sample(q, k_new, v_new, cache_k, cache_v, page_table, n_kv, steps):
    n_seq, n_pages = page_table.shape
    n_phys, page, _, _ = cache_k.shape
    ppb = MOBA_BLOCK // page
    n_blocks = n_pages // ppb
    bps = _unroll(n_blocks, SAMPLE_BLOCKS_PER_STEP)
    rows = q.shape[1]
    group = rows // (n_kv * steps)
    assert SUBLANES % n_kv == 0 and steps * n_kv <= LANES
    ck = cache_k.reshape(n_phys, page * n_kv, HEAD)
    cv = cache_v.reshape(n_phys, page * n_kv, HEAD)
    pages_per_step = ppb * bps

    def page_spec(p):
        return pl.BlockSpec((1, page * n_kv, HEAD),
                            lambda b, j, pt: (pt[b, j * pages_per_step + p], 0, 0))

    per_seq = lambda shape: pl.BlockSpec(shape, lambda b, j, pt: (b, 0, 0))
    grid_spec = pltpu.PrefetchScalarGridSpec(
        num_scalar_prefetch=1,
        grid=(n_seq, n_blocks // bps),
        in_specs=[per_seq((1, rows, HEAD)), per_seq((1, LANES, HEAD)), per_seq((1, LANES, HEAD))]
                 + [page_spec(p) for p in range(pages_per_step)] * 2,
        out_specs=per_seq((1, rows, HEAD)),
        scratch_shapes=[pltpu.VMEM((n_blocks, rows, LANES), F32) for _ in range(4)]
                       + [pltpu.VMEM((rows, MOBA_BLOCK * n_kv), F32)],
    )
    return pl.pallas_call(
        functools.partial(_moba_sample_kernel, n_kv=n_kv, group=group, steps=steps,
                          pages_per_block=ppb, blocks_per_step=bps),
        grid_spec=grid_spec,
        out_shape=jax.ShapeDtypeStruct((n_seq, rows, HEAD), F32),
        compiler_params=_params("arbitrary", "arbitrary"),
        name="moba_sample",
    )(page_table, q, k_new, v_new, *([ck] * pages_per_step), *([cv] * pages_per_step))


def kernel(x_prompt, x_sample, state_hgrn, cache_k, cache_v, page_table, norm_mix_a, w_in_a, lb_logits,
           onorm_a, w_out_a, norm_kv, w_kv, k_norm, norm_mix_b, w_q_b, q_norm, w_o_b, norm_ffn,
           w_gate_up, w_down):
    batch, seq, d = x_prompt.shape
    n_seq, steps, _ = x_sample.shape
    n_a = w_in_a.shape[0]
    depth = norm_ffn.shape[0]
    heads = d // HEAD
    n_kv = cache_k.shape[2]
    group = heads // n_kv
    kv_width = n_kv * HEAD
    n_dec = n_seq * steps
    past = page_table.shape[1] * cache_k.shape[1]
    assert batch == 1 and seq % MOBA_BLOCK == 0 and seq // MOBA_BLOCK <= 32
    assert past % MOBA_BLOCK == 0 and MOBA_BLOCK % cache_k.shape[1] == 0
    assert steps <= LANES and seq % n_dec == 0 and seq % GLA_CHUNK == 0

    x_p, x_s = x_prompt.reshape(seq, d), x_sample.reshape(n_dec, d)
    h = None
    states_p, states_s = [], []
    kv = None
    for layer in range(depth):
        if layer < n_a:
            if h is None:
                z = _norm_matmul(x_p, norm_mix_a[layer], w_in_a, layer, name="hgrn_in", tail=x_s)
            else:
                z = _norm_matmul(h, norm_mix_a[layer], w_in_a, layer, name="hgrn_in")
            o, s_p = _gla_prompt(z, lb_logits, onorm_a[layer], layer, seq)
            o, s_s = _gla_sample(z, o, seq, n_seq, steps, state_hgrn, layer, lb_logits,
                                 onorm_a[layer], layer)
            if h is None:
                h = _matmul_residual(o, w_out_a, layer, x_p, 512, "hgrn_out", res_tail=x_s)
            else:
                h = _matmul_residual(o, w_out_a, layer, h, 512, "hgrn_out")
            states_p.append(s_p.reshape(1, heads, HEAD, HEAD).astype(state_hgrn.dtype))
            states_s.append(s_s)
        else:
            if h is None:
                h = jnp.concatenate([x_p, x_s], axis=0)
            if kv is None:
                kv = _norm_matmul(h, norm_kv, w_kv, 0, head_gain=k_norm, norm_blocks=1,
                                  tn_cap=kv_width, name="kv_proj")
            jb = layer - n_a
            q = _norm_matmul(h, norm_mix_b[jb], w_q_b, jb, head_gain=q_norm[jb],
                             norm_blocks=heads, name="q_proj")
            att = _moba_prompt(q, kv, seq, n_kv)
            q_s = q[seq:].reshape(n_seq, steps, n_kv, group, HEAD).transpose(0, 2, 3, 1, 4)
            pad = ((0, 0), (0, LANES - steps * n_kv), (0, 0))
            k_s = jnp.pad(kv[seq:, :kv_width].reshape(n_seq, steps * n_kv, HEAD), pad)
            v_s = jnp.pad(kv[seq:, kv_width:].reshape(n_seq, steps * n_kv, HEAD), pad)
            att_s = _moba_sample(q_s.reshape(n_seq, n_kv * group * steps, HEAD), k_s, v_s,
                                 cache_k, cache_v, page_table, n_kv, steps)
            att_s = att_s.reshape(n_seq, n_kv, group, steps, HEAD).transpose(0, 3, 1, 2, 4)
            att = lax.dynamic_update_slice(att, att_s.reshape(n_dec, d).astype(BF16), (seq, 0))
            h = _matmul_residual(att, w_o_b, jb, h, 512, "attn_out")
        hf = _swiglu_up(h, norm_ffn[layer], w_gate_up, layer, "ffn_up")
        if layer + 1 < depth:
            h = _matmul_residual(hf, w_down, layer, h, 256, "ffn_down", tm_cap=FFN_ROW_TILE_CAP)
        else:
            y_p, y_s = _matmul_residual(hf, w_down, layer, h, 256, "ffn_down",
                                        tm_cap=FFN_ROW_TILE_CAP, head_rows=seq)

    k_new = kv[:, :kv_width]
    v_new = kv[:, kv_width:]
    return (y_p.reshape(batch, seq, d),
            y_s.reshape(n_seq, steps, d),
            jnp.stack(states_p),
            jnp.stack(states_s),
            k_new[:seq].reshape(batch, seq, n_kv, HEAD),
            v_new[:seq].reshape(batch, seq, n_kv, HEAD),
            k_new[seq:].reshape(n_seq, steps, n_kv, HEAD),
            v_new[seq:].reshape(n_seq, steps, n_kv, HEAD))
```

```python
import functools

import jax
import jax.numpy as jnp
from jax import lax
from jax.experimental import pallas as pl
from jax.experimental.pallas import tpu as pltpu

F32 = jnp.float32
BF16 = jnp.bfloat16

EPS = 1e-6
HEAD = 128
GLA_CHUNK = 64
MOBA_BLOCK = 256
MOBA_TOPK = 3
SCALE = HEAD ** -0.5
SCALE_LOG2E = SCALE * 1.4426950408889634
NEG = -1e30
LANES = 128
SUBLANES = 8
GLA_HEADS_PER_STEP = 4
NORM_ROWS = 16
NORM_UNROLL = 5
ROW_TILE_CAP = 2080
FFN_ROW_TILE_CAP = 1040
PROMPT_ROW_CHUNK = 256
PROMPT_OWN_CHUNK = 128
SCORE_LOOKAHEAD = 2
PAIRS_PER_TRIP = 4
SAMPLE_BLOCKS_PER_STEP = 8
SAMPLE_SEQ_UNROLL = 4
VMEM_LIMIT = 56 * 1024 * 1024


def _dot(a, b):
    return jnp.dot(a, b, preferred_element_type=F32)


def _dot_nt(a, b, precision=None):
    return lax.dot_general(a, b, (((1,), (1,)), ((), ())), precision=precision,
                           preferred_element_type=F32)


def _dot_tn(a, b):
    return lax.dot_general(a, b, (((0,), (0,)), ((), ())), preferred_element_type=F32)


def _div_mod(x, n):
    if n & (n - 1) == 0:
        return jnp.right_shift(x, n.bit_length() - 1), x & (n - 1)
    q = x // n
    return q, x - q * n


def _sigmoid(x):
    return 1.0 / (1.0 + jnp.exp(-x))


def _rms(x, g):
    return x * lax.rsqrt(jnp.mean(x * x, axis=-1, keepdims=True) + EPS) * g


def _row_tile(m, cap=ROW_TILE_CAP):
    for t in range(min(m, cap), 0, -1):
        if m % t == 0 and t % NORM_ROWS == 0:
            return t
    raise ValueError(f"no row tile for {m} rows")


def _col_tile(n, cap):
    for t in range(min(n, cap), 0, -1):
        if n % t == 0 and t % LANES == 0:
            return t
    raise ValueError(f"no column tile for {n} columns")


def _unroll(trips, cap):
    for u in range(min(trips, cap), 0, -1):
        if trips % u == 0:
            return u
    return 1


def _params(*sem):
    return pltpu.CompilerParams(dimension_semantics=sem, vmem_limit_bytes=VMEM_LIMIT)


def _normalise_rows(x_ref, g_ref, xn_ref, dst_row, n_rows):
    g = g_ref[...]
    trips = n_rows // NORM_ROWS

    def body(i, carry):
        src = pl.ds(pl.multiple_of(i * NORM_ROWS, NORM_ROWS), NORM_ROWS)
        dst = pl.ds(pl.multiple_of(dst_row + i * NORM_ROWS, NORM_ROWS), NORM_ROWS)
        xn_ref[dst, :] = _rms(x_ref[src, :], g).astype(BF16)
        return carry

    lax.fori_loop(0, trips, body, 0, unroll=_unroll(trips, NORM_UNROLL))


def _norm_matmul_kernel(x_ref, tail_ref, g_ref, w_ref, hg_ref, o_ref, xn_ref, *, norm_blocks,
                        tail_start):
    j = pl.program_id(1)

    if tail_start is None:
        @pl.when(j == 0)
        def _():
            _normalise_rows(x_ref, g_ref, xn_ref, 0, x_ref.shape[0])
    else:
        last = pl.program_id(0) == pl.num_programs(0) - 1

        @pl.when((j == 0) & jnp.logical_not(last))
        def _():
            _normalise_rows(x_ref, g_ref, xn_ref, 0, x_ref.shape[0])

        @pl.when((j == 0) & last)
        def _():
            _normalise_rows(x_ref, g_ref, xn_ref, 0, tail_start)
            _normalise_rows(tail_ref, g_ref, xn_ref, tail_start, tail_ref.shape[0])

    acc = _dot(xn_ref[...], w_ref[...].astype(BF16))
    if norm_blocks == 0:
        o_ref[...] = acc
        return

    @pl.when(j < norm_blocks)
    def _():
        for h in range(acc.shape[1] // HEAD):
            cols = slice(h * HEAD, (h + 1) * HEAD)
            o_ref[:, cols] = _rms(acc[:, cols], hg_ref[...])

    @pl.when(j >= norm_blocks)
    def _():
        o_ref[...] = acc


def _norm_matmul(x, g, w, layer, head_gain=None, norm_blocks=0, tn_cap=512, name="norm_matmul",
                 tail=None):
    d = x.shape[1]
    m = x.shape[0] + (0 if tail is None else tail.shape[0])
    n = w.shape[-1]
    tm, tn = _row_tile(m), _col_tile(n, tn_cap)
    if head_gain is None:
        head_gain = jnp.ones((HEAD,), F32)
    if w.ndim == 2:
        w = w[None]
    tail_start = None
    if tail is not None:
        tail_start = x.shape[0] - (m // tm - 1) * tm
        assert tail_start > 0 and tail_start + tail.shape[0] == tm
        assert tail_start % NORM_ROWS == 0 and tail.shape[0] % NORM_ROWS == 0
    else:
        tail = jnp.zeros((NORM_ROWS, d), x.dtype)
    return pl.pallas_call(
        functools.partial(_norm_matmul_kernel, norm_blocks=norm_blocks, tail_start=tail_start),
        grid=(m // tm, n // tn),
        in_specs=[
            pl.BlockSpec((tm, d), lambda i, j: (i, 0), pipeline_mode=pl.Buffered(1)),
            pl.BlockSpec(tail.shape, lambda i, j: (0, 0)),
            pl.BlockSpec((1, d), lambda i, j: (0, 0)),
            pl.BlockSpec((None, d, tn), lambda i, j: (layer, 0, j)),
            pl.BlockSpec((1, HEAD), lambda i, j: (0, 0)),
        ],
        out_specs=pl.BlockSpec((tm, tn), lambda i, j: (i, j)),
        out_shape=jax.ShapeDtypeStruct((m, n), F32),
        scratch_shapes=[pltpu.VMEM((tm, d), BF16)],
        compiler_params=_params("parallel", "arbitrary"),
        name=name,
    )(x, tail, g.reshape(1, d), w, head_gain.reshape(1, HEAD))


def _kv_q_proj_kernel(x_ref, gkv_ref, gq_ref, wkv_ref, wq_ref, kn_ref, qn_ref, kv_ref, q_ref,
                      xkv_ref, xq_ref, *, kv_blocks, key_blocks):
    j = pl.program_id(1)

    @pl.when(j == 0)
    def _():
        _normalise_rows(x_ref, gkv_ref, xkv_ref, 0, x_ref.shape[0])
        _normalise_rows(x_ref, gq_ref, xq_ref, 0, x_ref.shape[0])

    def head_norm(acc, gain_ref, o_ref):
        for h in range(acc.shape[1] // HEAD):
            cols = slice(h * HEAD, (h + 1) * HEAD)
            o_ref[:, cols] = _rms(acc[:, cols], gain_ref[...])

    @pl.when(j < key_blocks)
    def _():
        head_norm(_dot(xkv_ref[...], wkv_ref[...].astype(BF16)), kn_ref, kv_ref)

    @pl.when((j >= key_blocks) & (j < kv_blocks))
    def _():
        kv_ref[...] = _dot(xkv_ref[...], wkv_ref[...].astype(BF16))

    @pl.when(j >= kv_blocks)
    def _():
        head_norm(_dot(xq_ref[...], wq_ref[...].astype(BF16)), qn_ref, q_ref)


def _kv_q_proj(x, g_kv, w_kv, k_norm, g_q, w_q, layer_q, q_norm, tn=512):
    m, d = x.shape
    n_kv, n_q = w_kv.shape[-1], w_q.shape[-1]
    tm = _row_tile(m, FFN_ROW_TILE_CAP)
    assert n_kv % (2 * tn) == 0 and n_q % tn == 0
    kv_blocks, q_blocks = n_kv // tn, n_q // tn
    key_blocks = kv_blocks // 2
    return pl.pallas_call(
        functools.partial(_kv_q_proj_kernel, kv_blocks=kv_blocks, key_blocks=key_blocks),
        grid=(m // tm, kv_blocks + q_blocks),
        in_specs=[
            pl.BlockSpec((tm, d), lambda i, j: (i, 0), pipeline_mode=pl.Buffered(1)),
            pl.BlockSpec((1, d), lambda i, j: (0, 0)),
            pl.BlockSpec((1, d), lambda i, j: (0, 0)),
            pl.BlockSpec((d, tn), lambda i, j: (0, jnp.minimum(j, kv_blocks - 1))),
            pl.BlockSpec((None, d, tn), lambda i, j: (layer_q, 0, jnp.maximum(j - kv_blocks, 0))),
            pl.BlockSpec((1, HEAD), lambda i, j: (0, 0)),
            pl.BlockSpec((1, HEAD), lambda i, j: (0, 0)),
        ],
        out_specs=[pl.BlockSpec((tm, tn), lambda i, j: (i, jnp.minimum(j, kv_blocks - 1))),
                   pl.BlockSpec((tm, tn), lambda i, j: (i, jnp.maximum(j - kv_blocks, 0)))],
        out_shape=[jax.ShapeDtypeStruct((m, n_kv), F32), jax.ShapeDtypeStruct((m, n_q), F32)],
        scratch_shapes=[pltpu.VMEM((tm, d), BF16), pltpu.VMEM((tm, d), BF16)],
        compiler_params=_params("parallel", "arbitrary"),
        name="kv_q_proj",
    )(x, g_kv.reshape(1, d), g_q.reshape(1, d), w_kv, w_q, k_norm.reshape(1, HEAD),
      q_norm.reshape(1, HEAD))


def _swiglu_up_kernel(x_ref, g_ref, wa_ref, wu_ref, o_ref, xn_ref):
    @pl.when(pl.program_id(1) == 0)
    def _():
        _normalise_rows(x_ref, g_ref, xn_ref, 0, x_ref.shape[0])

    xn = xn_ref[...]
    a = _dot(xn, wa_ref[...].astype(BF16))
    u = _dot(xn, wu_ref[...].astype(BF16))
    o_ref[...] = (a * _sigmoid(a) * u).astype(BF16)


def _swiglu_up(x, g, w_gu, layer, name):
    m, d = x.shape
    d_ff = w_gu.shape[-1] // 2
    tm, tn = _row_tile(m, FFN_ROW_TILE_CAP), _col_tile(d_ff, 512)
    nblk = d_ff // tn
    return pl.pallas_call(
        _swiglu_up_kernel,
        grid=(m // tm, nblk),
        in_specs=[
            pl.BlockSpec((tm, d), lambda i, j: (i, 0)),
            pl.BlockSpec((1, d), lambda i, j: (0, 0)),
            pl.BlockSpec((None, d, tn), lambda i, j: (layer, 0, j)),
            pl.BlockSpec((None, d, tn), lambda i, j: (layer, 0, j + nblk)),
        ],
        out_specs=pl.BlockSpec((tm, tn), lambda i, j: (i, j)),
        out_shape=jax.ShapeDtypeStruct((m, d_ff), BF16),
        scratch_shapes=[pltpu.VMEM((tm, d), BF16)],
        compiler_params=_params("parallel", "arbitrary"),
        name=name,
    )(x, g.reshape(1, d), w_gu, w_gu)


def _matmul_residual_kernel(a_ref, w_ref, r_ref, o_ref):
    o_ref[...] = r_ref[...] + _dot(a_ref[...], w_ref[...].astype(BF16))


def _matmul_residual_joined_kernel(a_ref, w_ref, r_ref, rt_ref, o_ref, *, tail_start):
    acc = _dot(a_ref[...], w_ref[...].astype(BF16))
    last = pl.program_id(0) == pl.num_programs(0) - 1

    @pl.when(jnp.logical_not(last))
    def _():
        o_ref[...] = r_ref[...] + acc

    @pl.when(last)
    def _():
        o_ref[:tail_start, :] = r_ref[:tail_start, :] + acc[:tail_start, :]
        o_ref[tail_start:, :] = rt_ref[...] + acc[tail_start:, :]


def _matmul_residual_split_kernel(a_ref, w_ref, r_ref, head_ref, tail_ref, *, tail_start):
    out = r_ref[...] + _dot(a_ref[...], w_ref[...].astype(BF16))
    head_ref[...] = out
    last = pl.program_id(0) == pl.num_programs(0) - 1

    @pl.when(last)
    def _():
        tail_ref[...] = out[tail_start:tail_start + tail_ref.shape[0], :]

    @pl.when(jnp.logical_not(last))
    def _():
        tail_ref[...] = jnp.zeros_like(tail_ref)


def _matmul_residual(a, w, layer, res, tn_cap, name, tm_cap=ROW_TILE_CAP, head_rows=None,
                     res_tail=None):
    m, k = a.shape
    n = w.shape[-1]
    tm, tn = _row_tile(m, tm_cap), _col_tile(n, tn_cap)
    if res_tail is not None:
        tail_start = res.shape[0] - (m // tm - 1) * tm
        assert tail_start > 0 and tail_start + res_tail.shape[0] == tm
        assert tail_start % SUBLANES == 0 and head_rows is None
        return pl.pallas_call(
            functools.partial(_matmul_residual_joined_kernel, tail_start=tail_start),
            grid=(m // tm, n // tn),
            in_specs=[
                pl.BlockSpec((tm, k), lambda i, j: (i, 0)),
                pl.BlockSpec((None, k, tn), lambda i, j: (layer, 0, j)),
                pl.BlockSpec((tm, tn), lambda i, j: (i, j)),
                pl.BlockSpec((res_tail.shape[0], tn), lambda i, j: (0, j)),
            ],
            out_specs=pl.BlockSpec((tm, tn), lambda i, j: (i, j)),
            out_shape=jax.ShapeDtypeStruct((m, n), F32),
            compiler_params=_params("parallel", "arbitrary"),
            name=name,
        )(a, w, res, res_tail)
    in_specs = [
        pl.BlockSpec((tm, k), lambda i, j: (i, 0)),
        pl.BlockSpec((None, k, tn), lambda i, j: (layer, 0, j)),
        pl.BlockSpec((tm, tn), lambda i, j: (i, j)),
    ]
    if head_rows is None:
        return pl.pallas_call(
            _matmul_residual_kernel,
            grid=(m // tm, n // tn),
            in_specs=in_specs,
            out_specs=pl.BlockSpec((tm, tn), lambda i, j: (i, j)),
            out_shape=jax.ShapeDtypeStruct((m, n), F32),
            compiler_params=_params("parallel", "arbitrary"),
            name=name,
        )(a, w, res)
    tail_rows = m - head_rows
    tail_start = head_rows - (m // tm - 1) * tm
    assert 0 < tail_start and tail_start + tail_rows == tm and tail_start % SUBLANES == 0
    head, tail = pl.pallas_call(
        functools.partial(_matmul_residual_split_kernel, tail_start=tail_start),
        grid=(m // tm, n // tn),
        in_specs=in_specs,
        out_specs=[pl.BlockSpec((tm, tn), lambda i, j: (i, j)),
                   pl.BlockSpec((tail_rows, tn), lambda i, j: (i, j))],
        out_shape=[jax.ShapeDtypeStruct((head_rows, n), F32),
                   jax.ShapeDtypeStruct((m // tm * tail_rows, n), F32)],
        compiler_params=_params("parallel", "arbitrary"),
        name=name,
    )(a, w, res)
    return head, tail[(m // tm - 1) * tail_rows:]


def _lower_bound(logits, layer):
    e = jnp.exp(logits - jnp.max(logits, axis=0, keepdims=True))
    return jnp.sum(e[:layer + 1], axis=0, keepdims=True) / jnp.sum(e, axis=0, keepdims=True)


def _gate_inputs(zq, zf, lb):
    f = lb + (1.0 - lb) * _sigmoid(zf)
    return zq * _sigmoid(zq), 1.0 - f, jnp.log(f)


def _gated_output(o, zg, gain):
    return _rms(o, gain) * (zg * _sigmoid(zg))


def _cumsum_rows(g, seg=None):
    n = g.shape[0]
    row = lax.broadcasted_iota(jnp.int32, g.shape, 0)
    pos = row if seg is None else _div_mod(row, seg)[1]
    span = n if seg is None else seg
    s = 1
    while s < span:
        g = g + jnp.where(pos >= s, pltpu.roll(g, s, 0), 0.0)
        s *= 2
    return g


def _gla_prompt_kernel(zq_ref, zf_ref, zi_ref, zg_ref, lbl_ref, gain_ref, o_init_ref, o_ref, s_ref,
                       st_ref, *, layer, hp):
    del o_init_ref
    t = pl.program_id(1)
    c = GLA_CHUNK
    mid = (c - 1) // 2

    @pl.when(t == 0)
    def _():
        st_ref[...] = jnp.zeros_like(st_ref)

    lb_all = _lower_bound(lbl_ref[...], layer)
    gain = gain_ref[...]
    r2 = lax.broadcasted_iota(jnp.int32, (c, c), 0)
    c2 = lax.broadcasted_iota(jnp.int32, (c, c), 1)
    st = [st_ref[j] for j in range(hp)]
    for i in range(o_ref.shape[0] // c):
        rows = slice(i * c, (i + 1) * c)
        for j in range(hp):
            cols = slice(j * HEAD, (j + 1) * HEAD)
            q, k, g = _gate_inputs(zq_ref[rows, cols], zf_ref[rows, cols], lb_all[:, cols])
            v16 = zi_ref[rows, cols].astype(BF16)
            cum = _cumsum_rows(g)
            ref = cum[mid:mid + 1, :]
            last = cum[c - 1:c, :]
            att = _dot_nt((q * jnp.exp(cum - ref)).astype(BF16),
                          (k * jnp.exp(ref - cum)).astype(BF16))
            att = jnp.where(r2 >= c2, att, 0.0)
            o = _dot(att.astype(BF16), v16) + _dot_nt((q * jnp.exp(cum)).astype(BF16),
                                                       st[j].astype(BF16))
            k_out = (k * jnp.exp(last - cum)).astype(BF16)
            st[j] = jnp.exp(last) * st[j] + _dot_tn(v16, k_out)
            o_ref[rows, cols] = _gated_output(o, zg_ref[rows, cols], gain).astype(BF16)
    for j in range(hp):
        st_ref[j] = st[j]

    @pl.when(t == pl.num_programs(1) - 1)
    def _():
        for j in range(hp):
            s_ref[j] = st[j].T


def _gla_prompt(z, lb_logits, gain, layer, seq, tl=512, hp=GLA_HEADS_PER_STEP):
    heads = z.shape[1] // (4 * HEAD)
    hp = _unroll(heads, hp)
    tl = min(tl, seq)
    slots = lb_logits.shape[0]
    groups = heads // hp
    zspec = lambda off: pl.BlockSpec((tl, hp * HEAD), lambda h, t: (t, h + off * groups))
    return pl.pallas_call(
        functools.partial(_gla_prompt_kernel, layer=layer, hp=hp),
        grid=(groups, seq // tl),
        in_specs=[zspec(0), zspec(1), zspec(2), zspec(3),
                  pl.BlockSpec((slots, hp * HEAD), lambda h, t: (0, h)),
                  pl.BlockSpec((1, HEAD), lambda h, t: (0, 0)),
                  pl.BlockSpec(memory_space=pl.ANY)],
        out_specs=[pl.BlockSpec((tl, hp * HEAD), lambda h, t: (t, h)),
                   pl.BlockSpec((hp, HEAD, HEAD), lambda h, t: (h, 0, 0))],
        out_shape=[jax.ShapeDtypeStruct((z.shape[0], heads * HEAD), BF16),
                   jax.ShapeDtypeStruct((heads, HEAD, HEAD), F32)],
        scratch_shapes=[pltpu.VMEM((hp, HEAD, HEAD), F32)],
        input_output_aliases={6: 0},
        compiler_params=_params("parallel", "arbitrary"),
        name="gla_prompt",
    )(z, z, z, z, lb_logits, gain.reshape(1, HEAD), jnp.zeros((z.shape[0], heads * HEAD), BF16))


def _gla_sample_kernel(zq_ref, zf_ref, zi_ref, zg_ref, lbl_ref, gain_ref, s0_ref, o_in_ref,
                       o_ref, s_ref, last_ref, kout_ref, qin_ref, v_ref, *, layer, steps):
    del o_in_ref
    rows = zq_ref.shape[0]
    mid = (steps - 1) // 2
    lb = _lower_bound(lbl_ref[...], layer)
    q, k, g = _gate_inputs(zq_ref[...], zf_ref[...], lb)
    v16 = zi_ref[...].astype(BF16)
    cum = _cumsum_rows(g, seg=steps)

    row = lax.broadcasted_iota(jnp.int32, cum.shape, 0)
    pos = _div_mod(row, steps)[1]

    def spread(src_pos):
        picked = jnp.where(pos == src_pos, cum, 0.0)
        out = picked
        for d in range(steps):
            if d != src_pos:
                out = out + pltpu.roll(picked, (d - src_pos) % rows, 0)
        return out

    ref = spread(mid)
    last = spread(steps - 1)
    r2 = lax.broadcasted_iota(jnp.int32, (rows, rows), 0)
    c2 = lax.broadcasted_iota(jnp.int32, (rows, rows), 1)
    att = _dot_nt((q * jnp.exp(cum - ref)).astype(BF16), (k * jnp.exp(ref - cum)).astype(BF16))
    att = jnp.where((r2 >= c2) & (_div_mod(r2, steps)[0] == _div_mod(c2, steps)[0]), att, 0.0)
    o_intra = _dot(att.astype(BF16), v16)
    qin_ref[...] = (q * jnp.exp(cum)).astype(BF16)
    v_ref[...] = v16
    last_ref[...] = last
    kout_ref[...] = k * jnp.exp(last - cum)
    n_seq = s0_ref.shape[0]

    def body(b, o_inter):
        lo = b * steps
        mine = (row >= lo) & (row < lo + steps)
        st = s0_ref[b].T
        o_inter = o_inter + jnp.where(mine, _dot_nt(qin_ref[...], st.astype(BF16)), 0.0)
        k_out = jnp.where(mine, kout_ref[...], 0.0).astype(BF16)
        decay = jnp.exp(last_ref[pl.ds(lo, 1), :])
        s_ref[b] = (decay * st + _dot_tn(v_ref[...], k_out)).T
        return o_inter

    o_inter = lax.fori_loop(0, n_seq, body, jnp.zeros(cum.shape, F32),
                            unroll=_unroll(n_seq, SAMPLE_SEQ_UNROLL))
    o_ref[...] = _gated_output(o_intra + o_inter, zg_ref[...], gain_ref[...]).astype(BF16)


def _gla_sample(z, o_all, row0, n_seq, steps, s0, layer_s0, lb_logits, gain, layer):
    heads = z.shape[1] // (4 * HEAD)
    rows = n_seq * steps
    slots = lb_logits.shape[0]
    rb = row0 // rows
    zspec = lambda off: pl.BlockSpec((rows, HEAD), lambda h: (rb, h + off * heads))
    s_in = pl.BlockSpec((None, n_seq, None, HEAD, HEAD), lambda h: (layer_s0, 0, h, 0, 0))
    s_out = pl.BlockSpec((n_seq, None, HEAD, HEAD), lambda h: (0, h, 0, 0))
    return pl.pallas_call(
        functools.partial(_gla_sample_kernel, layer=layer, steps=steps),
        grid=(heads,),
        in_specs=[zspec(0), zspec(1), zspec(2), zspec(3),
                  pl.BlockSpec((slots, HEAD), lambda h: (0, h)),
                  pl.BlockSpec((1, HEAD), lambda h: (0, 0)),
                  s_in,
                  pl.BlockSpec(memory_space=pl.ANY)],
        out_specs=[pl.BlockSpec((rows, HEAD), lambda h: (rb, h)), s_out],
        out_shape=[jax.ShapeDtypeStruct(o_all.shape, o_all.dtype),
                   jax.ShapeDtypeStruct(s0.shape[1:], s0.dtype)],
        scratch_shapes=[pltpu.VMEM((rows, HEAD), F32), pltpu.VMEM((rows, HEAD), F32),
                        pltpu.VMEM((rows, HEAD), BF16), pltpu.VMEM((rows, HEAD), BF16)],
        input_output_aliases={7: 0},
        compiler_params=_params("parallel"),
        name="gla_sample",
    )(z, z, z, z, lb_logits, gain.reshape(1, HEAD), s0, o_all)


def _top_blocks_bits(gate_t, blk_f, n_valid):
    gate_t = jnp.where(blk_f < n_valid, gate_t, -jnp.inf)
    bits = jnp.zeros((1, gate_t.shape[1]), jnp.int32)
    for _ in range(MOBA_TOPK):
        mx = jnp.max(gate_t, axis=0, keepdims=True)
        idx = jnp.min(jnp.where(gate_t == mx, blk_f, float(LANES)), axis=0, keepdims=True)
        live = mx > -jnp.inf
        gate_t = jnp.where(blk_f == idx, -jnp.inf, gate_t)
        shift = jnp.minimum(idx, 31.0).astype(jnp.int32)
        bits = bits | jnp.where(live, jnp.left_shift(1, shift), 0)
    return bits


def _moba_prompt_kernel(q_ref, k_ref, v_ref, o_init_ref, o_ref, kmean_ref, qb_ref, sel_ref, m_ref,
                        l_ref, acc_ref, *, n_blocks, group, chunk):
    del o_init_ref
    qi = pl.program_id(1)
    blk = MOBA_BLOCK
    rows = group * blk
    n_chunks = rows // chunk

    @pl.when(qi == 0)
    def _():
        kmean_ref[...] = jnp.zeros_like(kmean_ref)
        for n in range(n_blocks):
            kmean_ref[n:n + 1, :] = jnp.mean(k_ref[n * blk:(n + 1) * blk, :], axis=0, keepdims=True)

    blk_f = lax.broadcasted_iota(jnp.int32, (kmean_ref.shape[0], blk), 0).astype(F32)
    qi_f = qi.astype(F32)
    for g in range(group):
        qg = q_ref[:, g * HEAD:(g + 1) * HEAD]
        qb_ref[g * blk:(g + 1) * blk, :] = (qg * SCALE_LOG2E).astype(BF16)
        gate_t = _dot_nt(kmean_ref[...], qg, precision=lax.Precision.HIGHEST)
        bits = _top_blocks_bits(gate_t, blk_f, qi_f)
        per_row = jnp.broadcast_to(bits, (SUBLANES, blk)).T
        sel_ref[g * blk:(g + 1) * blk, :] = jnp.broadcast_to(per_row[:, :1], (blk, LANES))

    own = pl.ds(pl.multiple_of(qi * blk, blk), blk)
    kd = k_ref[own, :].astype(BF16)
    vd = v_ref[own, :].astype(BF16)
    own_chunk = min(PROMPT_OWN_CHUNK, blk)
    for c in range(rows // own_chunk):
        r = slice(c * own_chunk, (c + 1) * own_chunk)
        q_pos = lax.broadcasted_iota(jnp.int32, (own_chunk, blk), 0) + (c * own_chunk) % blk
        k_pos = lax.broadcasted_iota(jnp.int32, (own_chunk, blk), 1)
        s = jnp.where(k_pos <= q_pos, _dot_nt(qb_ref[r, :], kd), NEG)
        m = jnp.max(s, axis=1, keepdims=True)
        p = jnp.exp2(s - m)
        m_ref[r, :] = jnp.broadcast_to(m, (own_chunk, LANES))
        l_ref[r, :] = jnp.broadcast_to(jnp.sum(p, axis=1, keepdims=True), (own_chunk, LANES))
        acc_ref[r, :] = _dot(p.astype(BF16), vd)

    def attend(first_pair, n_pairs):
        blocks = []
        for t in range(n_pairs):
            ja = 2 * (first_pair + t)
            jb = jnp.minimum(ja + 1, qi - 1)
            bit_a = jnp.left_shift(jnp.int32(1), ja)
            bit_b = jnp.where(ja + 1 < qi, jnp.left_shift(jnp.int32(1), jb), 0)
            rows_a = pl.ds(pl.multiple_of(ja * blk, blk), blk)
            rows_b = pl.ds(pl.multiple_of(jb * blk, blk), blk)
            blocks.append((bit_a, bit_b,
                           k_ref[rows_a, :].astype(BF16), k_ref[rows_b, :].astype(BF16),
                           v_ref[rows_a, :].astype(BF16), v_ref[rows_b, :].astype(BF16)))
        work = [(t, c) for t in range(n_pairs) for c in range(n_chunks)]

        def scores(item):
            t, c = item
            qc = qb_ref[c * chunk:(c + 1) * chunk, :]
            return _dot_nt(qc, blocks[t][2]), _dot_nt(qc, blocks[t][3])

        pending = [scores(w) for w in work[:SCORE_LOOKAHEAD]]
        for n, (t, c) in enumerate(work):
            bit_a, bit_b, _, _, va, vb = blocks[t]
            r = slice(c * chunk, (c + 1) * chunk)
            sel = sel_ref[r, :]
            on_a = (sel & bit_a) != 0
            on_b = (sel & bit_b) != 0
            sa, sb = pending.pop(0)
            if n + SCORE_LOOKAHEAD < len(work):
                pending.append(scores(work[n + SCORE_LOOKAHEAD]))
            parts = [jnp.where(on_a, sa[:, :LANES], NEG), jnp.where(on_a, sa[:, LANES:], NEG),
                     jnp.where(on_b, sb[:, :LANES], NEG), jnp.where(on_b, sb[:, LANES:], NEG)]
            m_prev = m_ref[r, :]
            top = jnp.maximum(jnp.maximum(parts[0], parts[1]), jnp.maximum(parts[2], parts[3]))
            m_new = jnp.maximum(m_prev, jnp.max(top, axis=1, keepdims=True))
            ps = [jnp.exp2(x - m_new) for x in parts]
            alpha = jnp.exp2(m_prev - m_new)
            l_ref[r, :] = alpha * l_ref[r, :] + jnp.sum((ps[0] + ps[1]) + (ps[2] + ps[3]), axis=1,
                                                        keepdims=True)
            pa = jnp.concatenate(ps[:2], axis=1).astype(BF16)
            pb = jnp.concatenate(ps[2:], axis=1).astype(BF16)
            acc_ref[r, :] = alpha * acc_ref[r, :] + (_dot(pa, va) + _dot(pb, vb))
            m_ref[r, :] = m_new

    n_pairs = (qi + 1) // 2
    trips = n_pairs // PAIRS_PER_TRIP

    def body(i, carry):
        attend(i * PAIRS_PER_TRIP, PAIRS_PER_TRIP)
        return carry

    lax.fori_loop(0, trips, body, 0)
    for left in range(1, PAIRS_PER_TRIP):
        @pl.when(n_pairs - trips * PAIRS_PER_TRIP == left)
        def _():
            attend(trips * PAIRS_PER_TRIP, left)

    for g in range(group):
        r = slice(g * blk, (g + 1) * blk)
        o_ref[:, g * HEAD:(g + 1) * HEAD] = (acc_ref[r, :] / l_ref[r, :]).astype(BF16)


def _moba_prompt(q, kv, seq, n_kv, chunk=PROMPT_ROW_CHUNK):
    heads = q.shape[1] // HEAD
    group = heads // n_kv
    n_blocks = seq // MOBA_BLOCK
    blk = MOBA_BLOCK
    rows = group * blk
    return pl.pallas_call(
        functools.partial(_moba_prompt_kernel, n_blocks=n_blocks, group=group, chunk=chunk),
        grid=(n_kv, n_blocks),
        in_specs=[pl.BlockSpec((blk, group * HEAD), lambda h, i: (i, h)),
                  pl.BlockSpec((seq, HEAD), lambda h, i: (0, h)),
                  pl.BlockSpec((seq, HEAD), lambda h, i: (0, n_kv + h)),
                  pl.BlockSpec(memory_space=pl.ANY)],
        out_specs=pl.BlockSpec((blk, group * HEAD), lambda h, i: (i, h)),
        out_shape=jax.ShapeDtypeStruct((q.shape[0], heads * HEAD), BF16),
        scratch_shapes=[pltpu.VMEM((-(-n_blocks // SUBLANES) * SUBLANES, HEAD), F32),
                        pltpu.VMEM((rows, HEAD), BF16),
                        pltpu.VMEM((rows, LANES), jnp.int32),
                        pltpu.VMEM((rows, LANES), F32),
                        pltpu.VMEM((rows, LANES), F32),
                        pltpu.VMEM((rows, HEAD), F32)],
        input_output_aliases={3: 0},
        compiler_params=_params("arbitrary", "arbitrary"),
        name="moba_prompt",
    )(q, kv, kv, jnp.zeros((q.shape[0], heads * HEAD), BF16))


def _moba_sample_kernel(pt_ref, q_ref, kn_ref, vn_ref, *refs, n_kv, group, steps, pages_per_block,
                        blocks_per_step):
    del pt_ref
    n_pages = pages_per_block * blocks_per_step
    kp = refs[:n_pages]
    vp = refs[n_pages:2 * n_pages]
    o_ref, m_s, l_s, g_s, o_s, bias_s = refs[2 * n_pages:]
    j = pl.program_id(1)
    rows = q_ref.shape[1]
    rq = group * steps
    tok_per_vreg = SUBLANES // n_kv
    n_col = bias_s.shape[1]

    @pl.when((pl.program_id(0) == 0) & (j == 0))
    def _():
        row_head = _div_mod(lax.broadcasted_iota(jnp.int32, (rows, n_col), 0), rq)[0]
        col_head = _div_mod(lax.broadcasted_iota(jnp.int32, (rows, n_col), 1), n_kv)[1]
        bias_s[...] = jnp.where(row_head == col_head, 0.0, NEG)

    qf = q_ref[0]
    q16 = qf.astype(BF16)
    head64 = _div_mod(lax.broadcasted_iota(jnp.int32, (rows, HEAD), 0), rq)[0]
    kbs, vbs, scores = [], [], []
    for t in range(blocks_per_step):
        pages = slice(t * pages_per_block, (t + 1) * pages_per_block)
        kbs.append(jnp.concatenate([r[0] for r in kp[pages]], axis=0))
        vbs.append(jnp.concatenate([r[0] for r in vp[pages]], axis=0))
        scores.append(_dot_nt(q16, kbs[t].astype(BF16)))
    for t in range(blocks_per_step):
        kb = kbs[t]
        s = scores[t] * SCALE + bias_s[...]
        m = jnp.max(s, axis=1, keepdims=True)
        p = jnp.exp(s - m)

        parts = [kb[SUBLANES * i:SUBLANES * (i + 1), :] for i in range(n_col // SUBLANES)]
        while len(parts) > 1:
            parts = [a + b for a, b in zip(parts[::2], parts[1::2])]
        folded = parts[0]
        kmean = jnp.zeros((rows, HEAD), F32)
        for h in range(n_kv):
            total = folded[h:h + 1, :]
            for i in range(1, tok_per_vreg):
                total = total + folded[h + i * n_kv:h + i * n_kv + 1, :]
            kmean = jnp.where(head64 == h, total * (1.0 / MOBA_BLOCK), kmean)
        gate = jnp.sum(qf * kmean, axis=1, keepdims=True)

        n = j * blocks_per_step + t
        m_s[n] = jnp.broadcast_to(m, (rows, LANES))
        l_s[n] = jnp.broadcast_to(jnp.sum(p, axis=1, keepdims=True), (rows, LANES))
        g_s[n] = jnp.broadcast_to(gate, (rows, LANES))
        o_s[n] = _dot(p.astype(BF16), vbs[t].astype(BF16))

    @pl.when(j == pl.num_programs(1) - 1)
    def _():
        gates = g_s[...]
        n_io = lax.broadcasted_iota(jnp.int32, gates.shape, 0)
        sel = jnp.zeros(gates.shape, jnp.bool_)
        for _ in range(min(MOBA_TOPK, g_s.shape[0])):
            mx = jnp.max(gates, axis=0, keepdims=True)
            idx = jnp.min(jnp.where(gates == mx, n_io, g_s.shape[0]), axis=0, keepdims=True)
            pick = n_io == idx
            sel = sel | pick
            gates = jnp.where(pick, -jnp.inf, gates)

        row = lax.broadcasted_iota(jnp.int32, (rows, LANES), 0)
        lane = lax.broadcasted_iota(jnp.int32, (rows, LANES), 1)
        q_head, q_step = _div_mod(row, rq)[0], _div_mod(row, steps)[1]
        k_step, k_head = _div_mod(lane, n_kv)
        ok = (q_head == k_head) & (k_step <= q_step) & (k_step < steps)
        s_own = jnp.where(ok, _dot_nt(q16, kn_ref[0].astype(BF16)) * SCALE, NEG)
        m_own = jnp.max(s_own, axis=1, keepdims=True)
        p_own = jnp.exp(s_own - m_own)
        l_own = jnp.sum(p_own, axis=1, keepdims=True)
        o_own = _dot(p_own.astype(BF16), vn_ref[0].astype(BF16))

        m_all = m_s[...]
        m_top = jnp.maximum(m_own, jnp.max(jnp.where(sel, m_all, -jnp.inf), axis=0))
        w = jnp.where(sel, jnp.exp(jnp.minimum(m_all - m_top[None], 0.0)), 0.0)
        w_own = jnp.exp(m_own - m_top)
        den = w_own * l_own + jnp.sum(w * l_s[...], axis=0)
        num = w_own * o_own + jnp.sum(w * o_s[...], axis=0)
        o_ref[0] = num / den


def _moba_sample(q, k_new, v_new, cache_k, cache_v, page_table, n_kv, steps):
    n_seq, n_pages = page_table.shape
    n_phys, page, _, _ = cache_k.shape
    ppb = MOBA_BLOCK // page
    n_blocks = n_pages // ppb
    bps = _unroll(n_blocks, SAMPLE_BLOCKS_PER_STEP)
    rows = q.shape[1]
    group = rows // (n_kv * steps)
    assert SUBLANES % n_kv == 0 and steps * n_kv <= LANES
    ck = cache_k.reshape(n_phys, page * n_kv, HEAD)
    cv = cache_v.reshape(n_phys, page * n_kv, HEAD)
    pages_per_step = ppb * bps

    def page_spec(p):
        return pl.BlockSpec((1, page * n_kv, HEAD),
                            lambda b, j, pt: (pt[b, j * pages_per_step + p], 0, 0))

    per_seq = lambda shape: pl.BlockSpec(shape, lambda b, j, pt: (b, 0, 0))
    grid_spec = pltpu.PrefetchScalarGridSpec(
        num_scalar_prefetch=1,
        grid=(n_seq, n_blocks // bps),
        in_specs=[per_seq((1, rows, HEAD)), per_seq((1, LANES, HEAD)), per_seq((1, LANES, HEAD))]
                 + [page_spec(p) for p in range(pages_per_step)] * 2,
        out_specs=per_seq((1, rows, HEAD)),
        scratch_shapes=[pltpu.VMEM((n_blocks, rows, LANES), F32) for _ in range(4)]
                       + [pltpu.VMEM((rows, MOBA_BLOCK * n_kv), F32)],
    )
    return pl.pallas_call(
        functools.partial(_moba_sample_kernel, n_kv=n_kv, group=group, steps=steps,
                          pages_per_block=ppb, blocks_per_step=bps),
        grid_spec=grid_spec,
        out_shape=jax.ShapeDtypeStruct((n_seq, rows, HEAD), F32),
        compiler_params=_params("arbitrary", "arbitrary"),
        name="moba_sample",
    )(page_table, q, k_new, v_new, *([ck] * pages_per_step), *([cv] * pages_per_step))


def kernel(x_prompt, x_sample, state_hgrn, cache_k, cache_v, page_table, norm_mix_a, w_in_a, lb_logits,
           onorm_a, w_out_a, norm_kv, w_kv, k_norm, norm_mix_b, w_q_b, q_norm, w_o_b, norm_ffn,
           w_gate_up, w_down):
    batch, seq, d = x_prompt.shape
    n_seq, steps, _ = x_sample.shape
    n_a = w_in_a.shape[0]
    depth = norm_ffn.shape[0]
    heads = d // HEAD
    n_kv = cache_k.shape[2]
    group = heads // n_kv
    kv_width = n_kv * HEAD
    n_dec = n_seq * steps
    past = page_table.shape[1] * cache_k.shape[1]
    assert batch == 1 and seq % MOBA_BLOCK == 0 and seq // MOBA_BLOCK <= 32
    assert past % MOBA_BLOCK == 0 and MOBA_BLOCK % cache_k.shape[1] == 0
    assert steps <= LANES and seq % n_dec == 0 and seq % GLA_CHUNK == 0

    x_p, x_s = x_prompt.reshape(seq, d), x_sample.reshape(n_dec, d)
    h = None
    states_p, states_s = [], []
    kv = None
    for layer in range(depth):
        if layer < n_a:
            if h is None:
                z = _norm_matmul(x_p, norm_mix_a[layer], w_in_a, layer, name="hgrn_in", tail=x_s)
            else:
                z = _norm_matmul(h, norm_mix_a[layer], w_in_a, layer, name="hgrn_in")
            o, s_p = _gla_prompt(z, lb_logits, onorm_a[layer], layer, seq)
            o, s_s = _gla_sample(z, o, seq, n_seq, steps, state_hgrn, layer, lb_logits,
                                 onorm_a[layer], layer)
            if h is None:
                h = _matmul_residual(o, w_out_a, layer, x_p, 512, "hgrn_out", res_tail=x_s)
            else:
                h = _matmul_residual(o, w_out_a, layer, h, 512, "hgrn_out")
            states_p.append(s_p.reshape(1, heads, HEAD, HEAD).astype(state_hgrn.dtype))
            states_s.append(s_s)
        else:
            if h is None:
                h = jnp.concatenate([x_p, x_s], axis=0)
            jb = layer - n_a
            if kv is None:
                kv, q = _kv_q_proj(h, norm_kv, w_kv, k_norm, norm_mix_b[jb], w_q_b, jb, q_norm[jb])
            else:
                q = _norm_matmul(h, norm_mix_b[jb], w_q_b, jb, head_gain=q_norm[jb],
                                 norm_blocks=heads, name="q_proj")
            att = _moba_prompt(q, kv, seq, n_kv)
            q_s = q[seq:].reshape(n_seq, steps, n_kv, group, HEAD).transpose(0, 2, 3, 1, 4)
            pad = ((0, 0), (0, LANES - steps * n_kv), (0, 0))
            k_s = jnp.pad(kv[seq:, :kv_width].reshape(n_seq, steps * n_kv, HEAD), pad)
            v_s = jnp.pad(kv[seq:, kv_width:].reshape(n_seq, steps * n_kv, HEAD), pad)
            att_s = _moba_sample(q_s.reshape(n_seq, n_kv * group * steps, HEAD), k_s, v_s,
                                 cache_k, cache_v, page_table, n_kv, steps)
            att_s = att_s.reshape(n_seq, n_kv, group, steps, HEAD).transpose(0, 3, 1, 2, 4)
            att = lax.dynamic_update_slice(att, att_s.reshape(n_dec, d).astype(BF16), (seq, 0))
            h = _matmul_residual(att, w_o_b, jb, h, 512, "attn_out")
        hf = _swiglu_up(h, norm_ffn[layer], w_gate_up, layer, "ffn_up")
        if layer + 1 < depth:
            h = _matmul_residual(hf, w_down, layer, h, 256, "ffn_down", tm_cap=FFN_ROW_TILE_CAP)
        else:
            y_p, y_s = _matmul_residual(hf, w_down, layer, h, 256, "ffn_down",
                                        tm_cap=FFN_ROW_TILE_CAP, head_rows=seq)

    k_new = kv[:, :kv_width]
    v_new = kv[:, kv_width:]
    return (y_p.reshape(batch, seq, d),
            y_s.reshape(n_seq, steps, d),
            jnp.stack(states_p),
            jnp.stack(states_s),
            k_new[:seq].reshape(batch, seq, n_kv, HEAD),
            v_new[:seq].reshape(batch, seq, n_kv, HEAD),
            k_new[seq:].reshape(n_seq, steps, n_kv, HEAD),
            v_new[seq:].reshape(n_seq, steps, n_kv, HEAD))
```

```python
import functools

import jax
import jax.numpy as jnp
from jax import lax
from jax.experimental import pallas as pl
from jax.experimental.pallas import tpu as pltpu

F32 = jnp.float32
BF16 = jnp.bfloat16

EPS = 1e-6
HEAD = 128
GLA_CHUNK = 64
MOBA_BLOCK = 256
MOBA_TOPK = 3
SCALE = HEAD ** -0.5
SCALE_LOG2E = SCALE * 1.4426950408889634
NEG = -1e30
LANES = 128
SUBLANES = 8
GLA_HEADS_PER_STEP = 4
NORM_ROWS = 16
NORM_UNROLL = 5
ROW_TILE_CAP = 2080
FFN_ROW_TILE_CAP = 1040
PROMPT_ROW_CHUNK = 256
PROMPT_OWN_CHUNK = 128
SCORE_LOOKAHEAD = 2
PAIRS_PER_TRIP = 4
SAMPLE_BLOCKS_PER_STEP = 8
SAMPLE_SEQ_UNROLL = 4
VMEM_LIMIT = 56 * 1024 * 1024


def _dot(a, b):
    return jnp.dot(a, b, preferred_element_type=F32)


def _dot_nt(a, b, precision=None):
    return lax.dot_general(a, b, (((1,), (1,)), ((), ())), precision=precision,
                           preferred_element_type=F32)


def _dot_tn(a, b):
    return lax.dot_general(a, b, (((0,), (0,)), ((), ())), preferred_element_type=F32)


def _div_mod(x, n):
    if n & (n - 1) == 0:
        return jnp.right_shift(x, n.bit_length() - 1), x & (n - 1)
    q = x // n
    return q, x - q * n


def _sigmoid(x):
    return 1.0 / (1.0 + jnp.exp(-x))


def _rms(x, g):
    return x * lax.rsqrt(jnp.mean(x * x, axis=-1, keepdims=True) + EPS) * g


def _row_tile(m, cap=ROW_TILE_CAP):
    for t in range(min(m, cap), 0, -1):
        if m % t == 0 and t % NORM_ROWS == 0:
            return t
    raise ValueError(f"no row tile for {m} rows")


def _col_tile(n, cap):
    for t in range(min(n, cap), 0, -1):
        if n % t == 0 and t % LANES == 0:
            return t
    raise ValueError(f"no column tile for {n} columns")


def _unroll(trips, cap):
    for u in range(min(trips, cap), 0, -1):
        if trips % u == 0:
            return u
    return 1


def _params(*sem):
    return pltpu.CompilerParams(dimension_semantics=sem, vmem_limit_bytes=VMEM_LIMIT)


def _normalise_rows(x_ref, g_ref, xn_ref, dst_row, n_rows):
    g = g_ref[...]
    trips = n_rows // NORM_ROWS

    def body(i, carry):
        src = pl.ds(pl.multiple_of(i * NORM_ROWS, NORM_ROWS), NORM_ROWS)
        dst = pl.ds(pl.multiple_of(dst_row + i * NORM_ROWS, NORM_ROWS), NORM_ROWS)
        xn_ref[dst, :] = _rms(x_ref[src, :], g).astype(BF16)
        return carry

    lax.fori_loop(0, trips, body, 0, unroll=_unroll(trips, NORM_UNROLL))


def _norm_matmul_kernel(x_ref, tail_ref, g_ref, w_ref, hg_ref, o_ref, xn_ref, *, norm_blocks,
                        tail_start):
    j = pl.program_id(1)

    if tail_start is None:
        @pl.when(j == 0)
        def _():
            _normalise_rows(x_ref, g_ref, xn_ref, 0, x_ref.shape[0])
    else:
        last = pl.program_id(0) == pl.num_programs(0) - 1

        @pl.when((j == 0) & jnp.logical_not(last))
        def _():
            _normalise_rows(x_ref, g_ref, xn_ref, 0, x_ref.shape[0])

        @pl.when((j == 0) & last)
        def _():
            _normalise_rows(x_ref, g_ref, xn_ref, 0, tail_start)
            _normalise_rows(tail_ref, g_ref, xn_ref, tail_start, tail_ref.shape[0])

    acc = _dot(xn_ref[...], w_ref[...].astype(BF16))
    if norm_blocks == 0:
        o_ref[...] = acc
        return

    @pl.when(j < norm_blocks)
    def _():
        for h in range(acc.shape[1] // HEAD):
            cols = slice(h * HEAD, (h + 1) * HEAD)
            o_ref[:, cols] = _rms(acc[:, cols], hg_ref[...])

    @pl.when(j >= norm_blocks)
    def _():
        o_ref[...] = acc


def _norm_matmul(x, g, w, layer, head_gain=None, norm_blocks=0, tn_cap=512, name="norm_matmul",
                 tail=None):
    d = x.shape[1]
    m = x.shape[0] + (0 if tail is None else tail.shape[0])
    n = w.shape[-1]
    tm, tn = _row_tile(m), _col_tile(n, tn_cap)
    if head_gain is None:
        head_gain = jnp.ones((HEAD,), F32)
    if w.ndim == 2:
        w = w[None]
    tail_start = None
    if tail is not None:
        tail_start = x.shape[0] - (m // tm - 1) * tm
        assert tail_start > 0 and tail_start + tail.shape[0] == tm
        assert tail_start % NORM_ROWS == 0 and tail.shape[0] % NORM_ROWS == 0
    else:
        tail = jnp.zeros((NORM_ROWS, d), x.dtype)
    return pl.pallas_call(
        functools.partial(_norm_matmul_kernel, norm_blocks=norm_blocks, tail_start=tail_start),
        grid=(m // tm, n // tn),
        in_specs=[
            pl.BlockSpec((tm, d), lambda i, j: (i, 0), pipeline_mode=pl.Buffered(1)),
            pl.BlockSpec(tail.shape, lambda i, j: (0, 0)),
            pl.BlockSpec((1, d), lambda i, j: (0, 0)),
            pl.BlockSpec((None, d, tn), lambda i, j: (layer, 0, j)),
            pl.BlockSpec((1, HEAD), lambda i, j: (0, 0)),
        ],
        out_specs=pl.BlockSpec((tm, tn), lambda i, j: (i, j)),
        out_shape=jax.ShapeDtypeStruct((m, n), F32),
        scratch_shapes=[pltpu.VMEM((tm, d), BF16)],
        compiler_params=_params("parallel", "arbitrary"),
        name=name,
    )(x, tail, g.reshape(1, d), w, head_gain.reshape(1, HEAD))


def _kv_q_proj_kernel(x_ref, gkv_ref, gq_ref, wkv_ref, wq_ref, kn_ref, qn_ref, k_ref, v_ref, q_ref,
                      xkv_ref, xq_ref):
    j = pl.program_id(1)

    @pl.when(j == 0)
    def _():
        _normalise_rows(x_ref, gkv_ref, xkv_ref, 0, x_ref.shape[0])
        _normalise_rows(x_ref, gq_ref, xq_ref, 0, x_ref.shape[0])

    def head_norm(acc, gain_ref, o_ref):
        for h in range(acc.shape[1] // HEAD):
            cols = slice(h * HEAD, (h + 1) * HEAD)
            o_ref[:, cols] = _rms(acc[:, cols], gain_ref[...])

    @pl.when(j == 0)
    def _():
        head_norm(_dot(xkv_ref[...], wkv_ref[...].astype(BF16)), kn_ref, k_ref)

    @pl.when(j == 1)
    def _():
        v_ref[...] = _dot(xkv_ref[...], wkv_ref[...].astype(BF16))

    @pl.when(j >= 2)
    def _():
        head_norm(_dot(xq_ref[...], wq_ref[...].astype(BF16)), qn_ref, q_ref)


def _kv_q_proj(x, g_kv, w_kv, k_norm, g_q, w_q, layer_q, q_norm):
    m, d = x.shape
    tn, n_q = w_kv.shape[-1] // 2, w_q.shape[-1]
    tm = _row_tile(m, FFN_ROW_TILE_CAP)
    assert tn % HEAD == 0 and n_q % tn == 0
    return pl.pallas_call(
        _kv_q_proj_kernel,
        grid=(m // tm, 2 + n_q // tn),
        in_specs=[
            pl.BlockSpec((tm, d), lambda i, j: (i, 0), pipeline_mode=pl.Buffered(1)),
            pl.BlockSpec((1, d), lambda i, j: (0, 0)),
            pl.BlockSpec((1, d), lambda i, j: (0, 0)),
            pl.BlockSpec((d, tn), lambda i, j: (0, jnp.minimum(j, 1))),
            pl.BlockSpec((None, d, tn), lambda i, j: (layer_q, 0, jnp.maximum(j - 2, 0))),
            pl.BlockSpec((1, HEAD), lambda i, j: (0, 0)),
            pl.BlockSpec((1, HEAD), lambda i, j: (0, 0)),
        ],
        out_specs=[pl.BlockSpec((tm, tn), lambda i, j: (i, 0)),
                   pl.BlockSpec((tm, tn), lambda i, j: (i, 0)),
                   pl.BlockSpec((tm, tn), lambda i, j: (i, jnp.maximum(j - 2, 0)))],
        out_shape=[jax.ShapeDtypeStruct((m, tn), F32), jax.ShapeDtypeStruct((m, tn), F32),
                   jax.ShapeDtypeStruct((m, n_q), F32)],
        scratch_shapes=[pltpu.VMEM((tm, d), BF16), pltpu.VMEM((tm, d), BF16)],
        compiler_params=_params("parallel", "arbitrary"),
        name="kv_q_proj",
    )(x, g_kv.reshape(1, d), g_q.reshape(1, d), w_kv, w_q, k_norm.reshape(1, HEAD),
      q_norm.reshape(1, HEAD))


def _swiglu_up_kernel(x_ref, g_ref, wa_ref, wu_ref, o_ref, xn_ref):
    @pl.when(pl.program_id(1) == 0)
    def _():
        _normalise_rows(x_ref, g_ref, xn_ref, 0, x_ref.shape[0])

    xn = xn_ref[...]
    a = _dot(xn, wa_ref[...].astype(BF16))
    u = _dot(xn, wu_ref[...].astype(BF16))
    o_ref[...] = (a * _sigmoid(a) * u).astype(BF16)


def _swiglu_up(x, g, w_gu, layer, name):
    m, d = x.shape
    d_ff = w_gu.shape[-1] // 2
    tm, tn = _row_tile(m, FFN_ROW_TILE_CAP), _col_tile(d_ff, 512)
    nblk = d_ff // tn
    return pl.pallas_call(
        _swiglu_up_kernel,
        grid=(m // tm, nblk),
        in_specs=[
            pl.BlockSpec((tm, d), lambda i, j: (i, 0)),
            pl.BlockSpec((1, d), lambda i, j: (0, 0)),
            pl.BlockSpec((None, d, tn), lambda i, j: (layer, 0, j)),
            pl.BlockSpec((None, d, tn), lambda i, j: (layer, 0, j + nblk)),
        ],
        out_specs=pl.BlockSpec((tm, tn), lambda i, j: (i, j)),
        out_shape=jax.ShapeDtypeStruct((m, d_ff), BF16),
        scratch_shapes=[pltpu.VMEM((tm, d), BF16)],
        compiler_params=_params("parallel", "arbitrary"),
        name=name,
    )(x, g.reshape(1, d), w_gu, w_gu)


def _matmul_residual_kernel(a_ref, w_ref, r_ref, o_ref):
    o_ref[...] = r_ref[...] + _dot(a_ref[...], w_ref[...].astype(BF16))


def _matmul_residual_joined_kernel(a_ref, w_ref, r_ref, rt_ref, o_ref, *, tail_start):
    acc = _dot(a_ref[...], w_ref[...].astype(BF16))
    last = pl.program_id(0) == pl.num_programs(0) - 1

    @pl.when(jnp.logical_not(last))
    def _():
        o_ref[...] = r_ref[...] + acc

    @pl.when(last)
    def _():
        o_ref[:tail_start, :] = r_ref[:tail_start, :] + acc[:tail_start, :]
        o_ref[tail_start:, :] = rt_ref[...] + acc[tail_start:, :]


def _matmul_residual_split_kernel(a_ref, w_ref, r_ref, head_ref, tail_ref, *, tail_start):
    out = r_ref[...] + _dot(a_ref[...], w_ref[...].astype(BF16))
    head_ref[...] = out
    last = pl.program_id(0) == pl.num_programs(0) - 1

    @pl.when(last)
    def _():
        tail_ref[...] = out[tail_start:tail_start + tail_ref.shape[0], :]

    @pl.when(jnp.logical_not(last))
    def _():
        tail_ref[...] = jnp.zeros_like(tail_ref)


def _matmul_residual(a, w, layer, res, tn_cap, name, tm_cap=ROW_TILE_CAP, head_rows=None,
                     res_tail=None):
    m, k = a.shape
    n = w.shape[-1]
    tm, tn = _row_tile(m, tm_cap), _col_tile(n, tn_cap)
    if res_tail is not None:
        tail_start = res.shape[0] - (m // tm - 1) * tm
        assert tail_start > 0 and tail_start + res_tail.shape[0] == tm
        assert tail_start % SUBLANES == 0 and head_rows is None
        return pl.pallas_call(
            functools.partial(_matmul_residual_joined_kernel, tail_start=tail_start),
            grid=(m // tm, n // tn),
            in_specs=[
                pl.BlockSpec((tm, k), lambda i, j: (i, 0)),
                pl.BlockSpec((None, k, tn), lambda i, j: (layer, 0, j)),
                pl.BlockSpec((tm, tn), lambda i, j: (i, j)),
                pl.BlockSpec((res_tail.shape[0], tn), lambda i, j: (0, j)),
            ],
            out_specs=pl.BlockSpec((tm, tn), lambda i, j: (i, j)),
            out_shape=jax.ShapeDtypeStruct((m, n), F32),
            compiler_params=_params("parallel", "arbitrary"),
            name=name,
        )(a, w, res, res_tail)
    in_specs = [
        pl.BlockSpec((tm, k), lambda i, j: (i, 0)),
        pl.BlockSpec((None, k, tn), lambda i, j: (layer, 0, j)),
        pl.BlockSpec((tm, tn), lambda i, j: (i, j)),
    ]
    if head_rows is None:
        return pl.pallas_call(
            _matmul_residual_kernel,
            grid=(m // tm, n // tn),
            in_specs=in_specs,
            out_specs=pl.BlockSpec((tm, tn), lambda i, j: (i, j)),
            out_shape=jax.ShapeDtypeStruct((m, n), F32),
            compiler_params=_params("parallel", "arbitrary"),
            name=name,
        )(a, w, res)
    tail_rows = m - head_rows
    tail_start = head_rows - (m // tm - 1) * tm
    assert 0 < tail_start and tail_start + tail_rows == tm and tail_start % SUBLANES == 0
    head, tail = pl.pallas_call(
        functools.partial(_matmul_residual_split_kernel, tail_start=tail_start),
        grid=(m // tm, n // tn),
        in_specs=in_specs,
        out_specs=[pl.BlockSpec((tm, tn), lambda i, j: (i, j)),
                   pl.BlockSpec((tail_rows, tn), lambda i, j: (i, j))],
        out_shape=[jax.ShapeDtypeStruct((head_rows, n), F32),
                   jax.ShapeDtypeStruct((m // tm * tail_rows, n), F32)],
        compiler_params=_params("parallel", "arbitrary"),
        name=name,
    )(a, w, res)
    return head, tail[(m // tm - 1) * tail_rows:]


def _lower_bound(logits, layer):
    e = jnp.exp(logits - jnp.max(logits, axis=0, keepdims=True))
    return jnp.sum(e[:layer + 1], axis=0, keepdims=True) / jnp.sum(e, axis=0, keepdims=True)


def _gate_inputs(zq, zf, lb):
    f = lb + (1.0 - lb) * _sigmoid(zf)
    return zq * _sigmoid(zq), 1.0 - f, jnp.log(f)


def _gated_output(o, zg, gain):
    return _rms(o, gain) * (zg * _sigmoid(zg))


def _cumsum_rows(g, seg=None):
    n = g.shape[0]
    row = lax.broadcasted_iota(jnp.int32, g.shape, 0)
    pos = row if seg is None else _div_mod(row, seg)[1]
    span = n if seg is None else seg
    s = 1
    while s < span:
        g = g + jnp.where(pos >= s, pltpu.roll(g, s, 0), 0.0)
        s *= 2
    return g


def _gla_prompt_kernel(zq_ref, zf_ref, zi_ref, zg_ref, lbl_ref, gain_ref, o_init_ref, o_ref, s_ref,
                       st_ref, *, layer, hp):
    del o_init_ref
    t = pl.program_id(1)
    c = GLA_CHUNK
    mid = (c - 1) // 2

    @pl.when(t == 0)
    def _():
        st_ref[...] = jnp.zeros_like(st_ref)

    lb_all = _lower_bound(lbl_ref[...], layer)
    gain = gain_ref[...]
    r2 = lax.broadcasted_iota(jnp.int32, (c, c), 0)
    c2 = lax.broadcasted_iota(jnp.int32, (c, c), 1)
    st = [st_ref[j] for j in range(hp)]
    for i in range(o_ref.shape[0] // c):
        rows = slice(i * c, (i + 1) * c)
        for j in range(hp):
            cols = slice(j * HEAD, (j + 1) * HEAD)
            q, k, g = _gate_inputs(zq_ref[rows, cols], zf_ref[rows, cols], lb_all[:, cols])
            v16 = zi_ref[rows, cols].astype(BF16)
            cum = _cumsum_rows(g)
            ref = cum[mid:mid + 1, :]
            last = cum[c - 1:c, :]
            att = _dot_nt((q * jnp.exp(cum - ref)).astype(BF16),
                          (k * jnp.exp(ref - cum)).astype(BF16))
            att = jnp.where(r2 >= c2, att, 0.0)
            o = _dot(att.astype(BF16), v16) + _dot_nt((q * jnp.exp(cum)).astype(BF16),
                                                       st[j].astype(BF16))
            k_out = (k * jnp.exp(last - cum)).astype(BF16)
            st[j] = jnp.exp(last) * st[j] + _dot_tn(v16, k_out)
            o_ref[rows, cols] = _gated_output(o, zg_ref[rows, cols], gain).astype(BF16)
    for j in range(hp):
        st_ref[j] = st[j]

    @pl.when(t == pl.num_programs(1) - 1)
    def _():
        for j in range(hp):
            s_ref[j] = st[j].T


def _gla_prompt(z, lb_logits, gain, layer, seq, tl=512, hp=GLA_HEADS_PER_STEP):
    heads = z.shape[1] // (4 * HEAD)
    hp = _unroll(heads, hp)
    tl = min(tl, seq)
    slots = lb_logits.shape[0]
    groups = heads // hp
    zspec = lambda off: pl.BlockSpec((tl, hp * HEAD), lambda h, t: (t, h + off * groups))
    return pl.pallas_call(
        functools.partial(_gla_prompt_kernel, layer=layer, hp=hp),
        grid=(groups, seq // tl),
        in_specs=[zspec(0), zspec(1), zspec(2), zspec(3),
                  pl.BlockSpec((slots, hp * HEAD), lambda h, t: (0, h)),
                  pl.BlockSpec((1, HEAD), lambda h, t: (0, 0)),
                  pl.BlockSpec(memory_space=pl.ANY)],
        out_specs=[pl.BlockSpec((tl, hp * HEAD), lambda h, t: (t, h)),
                   pl.BlockSpec((hp, HEAD, HEAD), lambda h, t: (h, 0, 0))],
        out_shape=[jax.ShapeDtypeStruct((z.shape[0], heads * HEAD), BF16),
                   jax.ShapeDtypeStruct((heads, HEAD, HEAD), F32)],
        scratch_shapes=[pltpu.VMEM((hp, HEAD, HEAD), F32)],
        input_output_aliases={6: 0},
        compiler_params=_params("parallel", "arbitrary"),
        name="gla_prompt",
    )(z, z, z, z, lb_logits, gain.reshape(1, HEAD), jnp.zeros((z.shape[0], heads * HEAD), BF16))


def _gla_sample_kernel(zq_ref, zf_ref, zi_ref, zg_ref, lbl_ref, gain_ref, s0_ref, o_in_ref,
                       o_ref, s_ref, last_ref, kout_ref, qin_ref, v_ref, *, layer, steps):
    del o_in_ref
    rows = zq_ref.shape[0]
    mid = (steps - 1) // 2
    lb = _lower_bound(lbl_ref[...], layer)
    q, k, g = _gate_inputs(zq_ref[...], zf_ref[...], lb)
    v16 = zi_ref[...].astype(BF16)
    cum = _cumsum_rows(g, seg=steps)

    row = lax.broadcasted_iota(jnp.int32, cum.shape, 0)
    pos = _div_mod(row, steps)[1]

    def spread(src_pos):
        picked = jnp.where(pos == src_pos, cum, 0.0)
        out = picked
        for d in range(steps):
            if d != src_pos:
                out = out + pltpu.roll(picked, (d - src_pos) % rows, 0)
        return out

    ref = spread(mid)
    last = spread(steps - 1)
    r2 = lax.broadcasted_iota(jnp.int32, (rows, rows), 0)
    c2 = lax.broadcasted_iota(jnp.int32, (rows, rows), 1)
    att = _dot_nt((q * jnp.exp(cum - ref)).astype(BF16), (k * jnp.exp(ref - cum)).astype(BF16))
    att = jnp.where((r2 >= c2) & (_div_mod(r2, steps)[0] == _div_mod(c2, steps)[0]), att, 0.0)
    o_intra = _dot(att.astype(BF16), v16)
    qin_ref[...] = (q * jnp.exp(cum)).astype(BF16)
    v_ref[...] = v16
    last_ref[...] = last
    kout_ref[...] = k * jnp.exp(last - cum)
    n_seq = s0_ref.shape[0]

    def body(b, o_inter):
        lo = b * steps
        mine = (row >= lo) & (row < lo + steps)
        st = s0_ref[b]
        o_inter = o_inter + jnp.where(mine, _dot(qin_ref[...], st.astype(BF16)), 0.0)
        k_out = jnp.where(mine, kout_ref[...], 0.0).astype(BF16)
        decay_row = jnp.exp(last_ref[pl.ds(lo, 1), :])
        decay_col = jnp.broadcast_to(decay_row, (SUBLANES, HEAD)).T[:, :1]
        s_ref[b] = decay_col * st + _dot_tn(k_out, v_ref[...])
        return o_inter

    o_inter = lax.fori_loop(0, n_seq, body, jnp.zeros(cum.shape, F32),
                            unroll=_unroll(n_seq, SAMPLE_SEQ_UNROLL))
    o_ref[...] = _gated_output(o_intra + o_inter, zg_ref[...], gain_ref[...]).astype(BF16)


def _gla_sample(z, o_all, row0, n_seq, steps, s0, layer_s0, lb_logits, gain, layer):
    heads = z.shape[1] // (4 * HEAD)
    rows = n_seq * steps
    slots = lb_logits.shape[0]
    rb = row0 // rows
    zspec = lambda off: pl.BlockSpec((rows, HEAD), lambda h: (rb, h + off * heads))
    s_in = pl.BlockSpec((None, n_seq, None, HEAD, HEAD), lambda h: (layer_s0, 0, h, 0, 0))
    s_out = pl.BlockSpec((n_seq, None, HEAD, HEAD), lambda h: (0, h, 0, 0))
    return pl.pallas_call(
        functools.partial(_gla_sample_kernel, layer=layer, steps=steps),
        grid=(heads,),
        in_specs=[zspec(0), zspec(1), zspec(2), zspec(3),
                  pl.BlockSpec((slots, HEAD), lambda h: (0, h)),
                  pl.BlockSpec((1, HEAD), lambda h: (0, 0)),
                  s_in,
                  pl.BlockSpec(memory_space=pl.ANY)],
        out_specs=[pl.BlockSpec((rows, HEAD), lambda h: (rb, h)), s_out],
        out_shape=[jax.ShapeDtypeStruct(o_all.shape, o_all.dtype),
                   jax.ShapeDtypeStruct(s0.shape[1:], s0.dtype)],
        scratch_shapes=[pltpu.VMEM((rows, HEAD), F32), pltpu.VMEM((rows, HEAD), F32),
                        pltpu.VMEM((rows, HEAD), BF16), pltpu.VMEM((rows, HEAD), BF16)],
        input_output_aliases={7: 0},
        compiler_params=_params("parallel"),
        name="gla_sample",
    )(z, z, z, z, lb_logits, gain.reshape(1, HEAD), s0, o_all)


def _top_blocks_bits(gate_t, blk_f, n_valid):
    gate_t = jnp.where(blk_f < n_valid, gate_t, -jnp.inf)
    bits = jnp.zeros((1, gate_t.shape[1]), jnp.int32)
    for _ in range(MOBA_TOPK):
        mx = jnp.max(gate_t, axis=0, keepdims=True)
        idx = jnp.min(jnp.where(gate_t == mx, blk_f, float(LANES)), axis=0, keepdims=True)
        live = mx > -jnp.inf
        gate_t = jnp.where(blk_f == idx, -jnp.inf, gate_t)
        shift = jnp.minimum(idx, 31.0).astype(jnp.int32)
        bits = bits | jnp.where(live, jnp.left_shift(1, shift), 0)
    return bits


def _moba_prompt_kernel(q_ref, k_ref, v_ref, o_init_ref, o_ref, kmean_ref, qb_ref, sel_ref, m_ref,
                        l_ref, acc_ref, *, n_blocks, group, chunk):
    del o_init_ref
    qi = pl.program_id(1)
    blk = MOBA_BLOCK
    rows = group * blk
    n_chunks = rows // chunk

    @pl.when(qi == 0)
    def _():
        kmean_ref[...] = jnp.zeros_like(kmean_ref)
        for n in range(n_blocks):
            kmean_ref[n:n + 1, :] = jnp.mean(k_ref[n * blk:(n + 1) * blk, :], axis=0, keepdims=True)

    blk_f = lax.broadcasted_iota(jnp.int32, (kmean_ref.shape[0], blk), 0).astype(F32)
    qi_f = qi.astype(F32)
    for g in range(group):
        qg = q_ref[:, g * HEAD:(g + 1) * HEAD]
        qb_ref[g * blk:(g + 1) * blk, :] = (qg * SCALE_LOG2E).astype(BF16)
        gate_t = _dot_nt(kmean_ref[...], qg, precision=lax.Precision.HIGHEST)
        bits = _top_blocks_bits(gate_t, blk_f, qi_f)
        per_row = jnp.broadcast_to(bits, (SUBLANES, blk)).T
        sel_ref[g * blk:(g + 1) * blk, :] = jnp.broadcast_to(per_row[:, :1], (blk, LANES))

    own = pl.ds(pl.multiple_of(qi * blk, blk), blk)
    kd = k_ref[own, :].astype(BF16)
    vd = v_ref[own, :].astype(BF16)
    own_chunk = min(PROMPT_OWN_CHUNK, blk)
    for c in range(rows // own_chunk):
        r = slice(c * own_chunk, (c + 1) * own_chunk)
        q_pos = lax.broadcasted_iota(jnp.int32, (own_chunk, blk), 0) + (c * own_chunk) % blk
        k_pos = lax.broadcasted_iota(jnp.int32, (own_chunk, blk), 1)
        s = jnp.where(k_pos <= q_pos, _dot_nt(qb_ref[r, :], kd), NEG)
        m = jnp.max(s, axis=1, keepdims=True)
        p = jnp.exp2(s - m)
        m_ref[r, :] = jnp.broadcast_to(m, (own_chunk, LANES))
        l_ref[r, :] = jnp.broadcast_to(jnp.sum(p, axis=1, keepdims=True), (own_chunk, LANES))
        acc_ref[r, :] = _dot(p.astype(BF16), vd)

    def attend(first_pair, n_pairs):
        blocks = []
        for t in range(n_pairs):
            ja = 2 * (first_pair + t)
            jb = jnp.minimum(ja + 1, qi - 1)
            bit_a = jnp.left_shift(jnp.int32(1), ja)
            bit_b = jnp.where(ja + 1 < qi, jnp.left_shift(jnp.int32(1), jb), 0)
            rows_a = pl.ds(pl.multiple_of(ja * blk, blk), blk)
            rows_b = pl.ds(pl.multiple_of(jb * blk, blk), blk)
            blocks.append((bit_a, bit_b,
                           k_ref[rows_a, :].astype(BF16), k_ref[rows_b, :].astype(BF16),
                           v_ref[rows_a, :].astype(BF16), v_ref[rows_b, :].astype(BF16)))
        work = [(t, c) for t in range(n_pairs) for c in range(n_chunks)]

        def scores(item):
            t, c = item
            qc = qb_ref[c * chunk:(c + 1) * chunk, :]
            return _dot_nt(qc, blocks[t][2]), _dot_nt(qc, blocks[t][3])

        pending = [scores(w) for w in work[:SCORE_LOOKAHEAD]]
        for n, (t, c) in enumerate(work):
            bit_a, bit_b, _, _, va, vb = blocks[t]
            r = slice(c * chunk, (c + 1) * chunk)
            sel = sel_ref[r, :]
            on_a = (sel & bit_a) != 0
            on_b = (sel & bit_b) != 0
            sa, sb = pending.pop(0)
            if n + SCORE_LOOKAHEAD < len(work):
                pending.append(scores(work[n + SCORE_LOOKAHEAD]))
            parts = [jnp.where(on_a, sa[:, :LANES], NEG), jnp.where(on_a, sa[:, LANES:], NEG),
                     jnp.where(on_b, sb[:, :LANES], NEG), jnp.where(on_b, sb[:, LANES:], NEG)]
            m_prev = m_ref[r, :]
            top = jnp.maximum(jnp.maximum(parts[0], parts[1]), jnp.maximum(parts[2], parts[3]))
            m_new = jnp.maximum(m_prev, jnp.max(top, axis=1, keepdims=True))
            ps = [jnp.exp2(x - m_new) for x in parts]
            alpha = jnp.exp2(m_prev - m_new)
            l_ref[r, :] = alpha * l_ref[r, :] + jnp.sum((ps[0] + ps[1]) + (ps[2] + ps[3]), axis=1,
                                                        keepdims=True)
            pa = jnp.concatenate(ps[:2], axis=1).astype(BF16)
            pb = jnp.concatenate(ps[2:], axis=1).astype(BF16)
            acc_ref[r, :] = alpha * acc_ref[r, :] + (_dot(pa, va) + _dot(pb, vb))
            m_ref[r, :] = m_new

    n_pairs = (qi + 1) // 2
    trips = n_pairs // PAIRS_PER_TRIP

    def body(i, carry):
        attend(i * PAIRS_PER_TRIP, PAIRS_PER_TRIP)
        return carry

    lax.fori_loop(0, trips, body, 0)
    for left in range(1, PAIRS_PER_TRIP):
        @pl.when(n_pairs - trips * PAIRS_PER_TRIP == left)
        def _():
            attend(trips * PAIRS_PER_TRIP, left)

    for g in range(group):
        r = slice(g * blk, (g + 1) * blk)
        o_ref[:, g * HEAD:(g + 1) * HEAD] = (acc_ref[r, :] / l_ref[r, :]).astype(BF16)


def _moba_prompt(q, k, v, seq, n_kv, chunk=PROMPT_ROW_CHUNK):
    heads = q.shape[1] // HEAD
    group = heads // n_kv
    n_blocks = seq // MOBA_BLOCK
    blk = MOBA_BLOCK
    rows = group * blk
    return pl.pallas_call(
        functools.partial(_moba_prompt_kernel, n_blocks=n_blocks, group=group, chunk=chunk),
        grid=(n_kv, n_blocks),
        in_specs=[pl.BlockSpec((blk, group * HEAD), lambda h, i: (i, h)),
                  pl.BlockSpec((seq, HEAD), lambda h, i: (0, h)),
                  pl.BlockSpec((seq, HEAD), lambda h, i: (0, h)),
                  pl.BlockSpec(memory_space=pl.ANY)],
        out_specs=pl.BlockSpec((blk, group * HEAD), lambda h, i: (i, h)),
        out_shape=jax.ShapeDtypeStruct((q.shape[0], heads * HEAD), BF16),
        scratch_shapes=[pltpu.VMEM((-(-n_blocks // SUBLANES) * SUBLANES, HEAD), F32),
                        pltpu.VMEM((rows, HEAD), BF16),
                        pltpu.VMEM((rows, LANES), jnp.int32),
                        pltpu.VMEM((rows, LANES), F32),
                        pltpu.VMEM((rows, LANES), F32),
                        pltpu.VMEM((rows, HEAD), F32)],
        input_output_aliases={3: 0},
        compiler_params=_params("arbitrary", "arbitrary"),
        name="moba_prompt",
    )(q, k, v, jnp.zeros((q.shape[0], heads * HEAD), BF16))


def _moba_sample_kernel(pt_ref, q_ref, kn_ref, vn_ref, *refs, n_kv, group, steps, pages_per_block,
                        blocks_per_step):
    del pt_ref
    n_pages = pages_per_block * blocks_per_step
    kp = refs[:n_pages]
    vp = refs[n_pages:2 * n_pages]
    o_ref, m_s, l_s, g_s, o_s, bias_s = refs[2 * n_pages:]
    j = pl.program_id(1)
    rows = q_ref.shape[1]
    rq = group * steps
    tok_per_vreg = SUBLANES // n_kv
    n_col = bias_s.shape[1]

    @pl.when((pl.program_id(0) == 0) & (j == 0))
    def _():
        row_head = _div_mod(lax.broadcasted_iota(jnp.int32, (rows, n_col), 0), rq)[0]
        col_head = _div_mod(lax.broadcasted_iota(jnp.int32, (rows, n_col), 1), n_kv)[1]
        bias_s[...] = jnp.where(row_head == col_head, 0.0, NEG)

    qf = q_ref[0]
    q16 = qf.astype(BF16)
    head64 = _div_mod(lax.broadcasted_iota(jnp.int32, (rows, HEAD), 0), rq)[0]
    kbs, vbs, scores = [], [], []
    for t in range(blocks_per_step):
        pages = slice(t * pages_per_block, (t + 1) * pages_per_block)
        kbs.append(jnp.concatenate([r[0] for r in kp[pages]], axis=0))
        vbs.append(jnp.concatenate([r[0] for r in vp[pages]], axis=0))
        scores.append(_dot_nt(q16, kbs[t].astype(BF16)))
    for t in range(blocks_per_step):
        kb = kbs[t]
        s = scores[t] * SCALE + bias_s[...]
        m = jnp.max(s, axis=1, keepdims=True)
        p = jnp.exp(s - m)

        parts = [kb[SUBLANES * i:SUBLANES * (i + 1), :] for i in range(n_col // SUBLANES)]
        while len(parts) > 1:
            parts = [a + b for a, b in zip(parts[::2], parts[1::2])]
        folded = parts[0]
        kmean = jnp.zeros((rows, HEAD), F32)
        for h in range(n_kv):
            total = folded[h:h + 1, :]
            for i in range(1, tok_per_vreg):
                total = total + folded[h + i * n_kv:h + i * n_kv + 1, :]
            kmean = jnp.where(head64 == h, total * (1.0 / MOBA_BLOCK), kmean)
        gate = jnp.sum(qf * kmean, axis=1, keepdims=True)

        n = j * blocks_per_step + t
        m_s[n] = jnp.broadcast_to(m, (rows, LANES))
        l_s[n] = jnp.broadcast_to(jnp.sum(p, axis=1, keepdims=True), (rows, LANES))
        g_s[n] = jnp.broadcast_to(gate, (rows, LANES))
        o_s[n] = _dot(p.astype(BF16), vbs[t].astype(BF16))

    @pl.when(j == pl.num_programs(1) - 1)
    def _():
        gates = g_s[...]
        n_io = lax.broadcasted_iota(jnp.int32, gates.shape, 0)
        sel = jnp.zeros(gates.shape, jnp.bool_)
        for _ in range(min(MOBA_TOPK, g_s.shape[0])):
            mx = jnp.max(gates, axis=0, keepdims=True)
            idx = jnp.min(jnp.where(gates == mx, n_io, g_s.shape[0]), axis=0, keepdims=True)
            pick = n_io == idx
            sel = sel | pick
            gates = jnp.where(pick, -jnp.inf, gates)

        row = lax.broadcasted_iota(jnp.int32, (rows, LANES), 0)
        lane = lax.broadcasted_iota(jnp.int32, (rows, LANES), 1)
        q_head, q_step = _div_mod(row, rq)[0], _div_mod(row, steps)[1]
        k_step, k_head = _div_mod(lane, n_kv)
        ok = (q_head == k_head) & (k_step <= q_step) & (k_step < steps)
        s_own = jnp.where(ok, _dot_nt(q16, kn_ref[0].astype(BF16)) * SCALE, NEG)
        m_own = jnp.max(s_own, axis=1, keepdims=True)
        p_own = jnp.exp(s_own - m_own)
        l_own = jnp.sum(p_own, axis=1, keepdims=True)
        o_own = _dot(p_own.astype(BF16), vn_ref[0].astype(BF16))

        m_all = m_s[...]
        m_top = jnp.maximum(m_own, jnp.max(jnp.where(sel, m_all, -jnp.inf), axis=0))
        w = jnp.where(sel, jnp.exp(jnp.minimum(m_all - m_top[None], 0.0)), 0.0)
        w_own = jnp.exp(m_own - m_top)
        den = w_own * l_own + jnp.sum(w * l_s[...], axis=0)
        num = w_own * o_own + jnp.sum(w * o_s[...], axis=0)
        o_ref[0] = num / den


def _moba_sample(q, k_new, v_new, cache_k, cache_v, page_table, n_kv, steps):
    n_seq, n_pages = page_table.shape
    n_phys, page, _, _ = cache_k.shape
    ppb = MOBA_BLOCK // page
    n_blocks = n_pages // ppb
    bps = _unroll(n_blocks, SAMPLE_BLOCKS_PER_STEP)
    rows = q.shape[1]
    group = rows // (n_kv * steps)
    assert SUBLANES % n_kv == 0 and steps * n_kv <= LANES
    ck = cache_k.reshape(n_phys, page * n_kv, HEAD)
    cv = cache_v.reshape(n_phys, page * n_kv, HEAD)
    pages_per_step = ppb * bps

    def page_spec(p):
        return pl.BlockSpec((1, page * n_kv, HEAD),
                            lambda b, j, pt: (pt[b, j * pages_per_step + p], 0, 0))

    per_seq = lambda shape: pl.BlockSpec(shape, lambda b, j, pt: (b, 0, 0))
    grid_spec = pltpu.PrefetchScalarGridSpec(
        num_scalar_prefetch=1,
        grid=(n_seq, n_blocks // bps),
        in_specs=[per_seq((1, rows, HEAD)), per_seq((1, LANES, HEAD)), per_seq((1, LANES, HEAD))]
                 + [page_spec(p) for p in range(pages_per_step)] * 2,
        out_specs=per_seq((1, rows, HEAD)),
        scratch_shapes=[pltpu.VMEM((n_blocks, rows, LANES), F32) for _ in range(4)]
                       + [pltpu.VMEM((rows, MOBA_BLOCK * n_kv), F32)],
    )
    return pl.pallas_call(
        functools.partial(_moba_sample_kernel, n_kv=n_kv, group=group, steps=steps,
                          pages_per_block=ppb, blocks_per_step=bps),
        grid_spec=grid_spec,
        out_shape=jax.ShapeDtypeStruct((n_seq, rows, HEAD), F32),
        compiler_params=_params("arbitrary", "arbitrary"),
        name="moba_sample",
    )(page_table, q, k_new, v_new, *([ck] * pages_per_step), *([cv] * pages_per_step))


def kernel(x_prompt, x_sample, state_hgrn, cache_k, cache_v, page_table, norm_mix_a, w_in_a, lb_logits,
           onorm_a, w_out_a, norm_kv, w_kv, k_norm, norm_mix_b, w_q_b, q_norm, w_o_b, norm_ffn,
           w_gate_up, w_down):
    batch, seq, d = x_prompt.shape
    n_seq, steps, _ = x_sample.shape
    n_a = w_in_a.shape[0]
    depth = norm_ffn.shape[0]
    heads = d // HEAD
    n_kv = cache_k.shape[2]
    group = heads // n_kv
    n_dec = n_seq * steps
    past = page_table.shape[1] * cache_k.shape[1]
    assert batch == 1 and seq % MOBA_BLOCK == 0 and seq // MOBA_BLOCK <= 32
    assert past % MOBA_BLOCK == 0 and MOBA_BLOCK % cache_k.shape[1] == 0
    assert steps <= LANES and seq % n_dec == 0 and seq % GLA_CHUNK == 0

    x_p, x_s = x_prompt.reshape(seq, d), x_sample.reshape(n_dec, d)
    h = None
    states_p, states_s = [], []
    k_new = v_new = None
    for layer in range(depth):
        if layer < n_a:
            if h is None:
                z = _norm_matmul(x_p, norm_mix_a[layer], w_in_a, layer, name="hgrn_in", tail=x_s)
            else:
                z = _norm_matmul(h, norm_mix_a[layer], w_in_a, layer, name="hgrn_in")
            o, s_p = _gla_prompt(z, lb_logits, onorm_a[layer], layer, seq)
            o, s_s = _gla_sample(z, o, seq, n_seq, steps, state_hgrn, layer, lb_logits,
                                 onorm_a[layer], layer)
            if h is None:
                h = _matmul_residual(o, w_out_a, layer, x_p, 512, "hgrn_out", res_tail=x_s)
            else:
                h = _matmul_residual(o, w_out_a, layer, h, 512, "hgrn_out")
            states_p.append(s_p.reshape(1, heads, HEAD, HEAD).astype(state_hgrn.dtype))
            states_s.append(s_s)
        else:
            if h is None:
                h = jnp.concatenate([x_p, x_s], axis=0)
            jb = layer - n_a
            if k_new is None:
                k_new, v_new, q = _kv_q_proj(h, norm_kv, w_kv, k_norm, norm_mix_b[jb], w_q_b, jb,
                                             q_norm[jb])
            else:
                q = _norm_matmul(h, norm_mix_b[jb], w_q_b, jb, head_gain=q_norm[jb],
                                 norm_blocks=heads, name="q_proj")
            att = _moba_prompt(q, k_new, v_new, seq, n_kv)
            q_s = q[seq:].reshape(n_seq, steps, n_kv, group, HEAD).transpose(0, 2, 3, 1, 4)
            pad = ((0, 0), (0, LANES - steps * n_kv), (0, 0))
            k_s = jnp.pad(k_new[seq:].reshape(n_seq, steps * n_kv, HEAD), pad)
            v_s = jnp.pad(v_new[seq:].reshape(n_seq, steps * n_kv, HEAD), pad)
            att_s = _moba_sample(q_s.reshape(n_seq, n_kv * group * steps, HEAD), k_s, v_s,
                                 cache_k, cache_v, page_table, n_kv, steps)
            att_s = att_s.reshape(n_seq, n_kv, group, steps, HEAD).transpose(0, 3, 1, 2, 4)
            att = lax.dynamic_update_slice(att, att_s.reshape(n_dec, d).astype(BF16), (seq, 0))
            h = _matmul_residual(att, w_o_b, jb, h, 512, "attn_out")
        hf = _swiglu_up(h, norm_ffn[layer], w_gate_up, layer, "ffn_up")
        if layer + 1 < depth:
            h = _matmul_residual(hf, w_down, layer, h, 256, "ffn_down", tm_cap=FFN_ROW_TILE_CAP)
        else:
            y_p, y_s = _matmul_residual(hf, w_down, layer, h, 256, "ffn_down",
                                        tm_cap=FFN_ROW_TILE_CAP, head_rows=seq)

    return (y_p.reshape(batch, seq, d),
            y_s.reshape(n_seq, steps, d),
            jnp.stack(states_p),
            jnp.stack(states_s),
            k_new[:seq].reshape(batch, seq, n_kv, HEAD),
            v_new[:seq].reshape(batch, seq, n_kv, HEAD),
            k_new[seq:].reshape(n_seq, steps, n_kv, HEAD),
            v_new[seq:].reshape(n_seq, steps, n_kv, HEAD))
```

```python
import functools

import jax
import jax.numpy as jnp
from jax import lax
from jax.experimental import pallas as pl
from jax.experimental.pallas import tpu as pltpu

F32 = jnp.float32
BF16 = jnp.bfloat16

EPS = 1e-6
HEAD = 128
GLA_CHUNK = 64
MOBA_BLOCK = 256
MOBA_TOPK = 3
SCALE = HEAD ** -0.5
SCALE_LOG2E = SCALE * 1.4426950408889634
NEG = -1e30
LANES = 128
SUBLANES = 8
GLA_HEADS_PER_STEP = 4
NORM_ROWS = 16
NORM_UNROLL = 5
ROW_TILE_CAP = 2080
FFN_ROW_TILE_CAP = 1040
PROMPT_ROW_CHUNK = 256
PROMPT_OWN_CHUNK = 128
SCORE_LOOKAHEAD = 2
PAIRS_PER_TRIP = 4
SAMPLE_BLOCKS_PER_STEP = 8
SAMPLE_SEQ_UNROLL = 4
VMEM_LIMIT = 56 * 1024 * 1024


def _dot(a, b):
    return jnp.dot(a, b, preferred_element_type=F32)


def _dot_nt(a, b, precision=None):
    return lax.dot_general(a, b, (((1,), (1,)), ((), ())), precision=precision,
                           preferred_element_type=F32)


def _dot_tn(a, b):
    return lax.dot_general(a, b, (((0,), (0,)), ((), ())), preferred_element_type=F32)


def _div_mod(x, n):
    if n & (n - 1) == 0:
        return jnp.right_shift(x, n.bit_length() - 1), x & (n - 1)
    q = x // n
    return q, x - q * n


def _sigmoid(x):
    return 1.0 / (1.0 + jnp.exp(-x))


def _rms(x, g):
    return x * lax.rsqrt(jnp.mean(x * x, axis=-1, keepdims=True) + EPS) * g


def _row_tile(m, cap=ROW_TILE_CAP):
    for t in range(min(m, cap), 0, -1):
        if m % t == 0 and t % NORM_ROWS == 0:
            return t
    raise ValueError(f"no row tile for {m} rows")


def _col_tile(n, cap):
    for t in range(min(n, cap), 0, -1):
        if n % t == 0 and t % LANES == 0:
            return t
    raise ValueError(f"no column tile for {n} columns")


def _unroll(trips, cap):
    for u in range(min(trips, cap), 0, -1):
        if trips % u == 0:
            return u
    return 1


def _params(*sem):
    return pltpu.CompilerParams(dimension_semantics=sem, vmem_limit_bytes=VMEM_LIMIT)


def _normalise_rows(x_ref, g_ref, xn_ref, dst_row, n_rows):
    g = g_ref[...]
    trips = n_rows // NORM_ROWS

    def body(i, carry):
        src = pl.ds(pl.multiple_of(i * NORM_ROWS, NORM_ROWS), NORM_ROWS)
        dst = pl.ds(pl.multiple_of(dst_row + i * NORM_ROWS, NORM_ROWS), NORM_ROWS)
        xn_ref[dst, :] = _rms(x_ref[src, :], g).astype(BF16)
        return carry

    lax.fori_loop(0, trips, body, 0, unroll=_unroll(trips, NORM_UNROLL))


def _norm_matmul_kernel(x_ref, tail_ref, g_ref, w_ref, hg_ref, o_ref, xn_ref, *, norm_blocks,
                        tail_start):
    j = pl.program_id(1)

    if tail_start is None:
        @pl.when(j == 0)
        def _():
            _normalise_rows(x_ref, g_ref, xn_ref, 0, x_ref.shape[0])
    else:
        last = pl.program_id(0) == pl.num_programs(0) - 1

        @pl.when((j == 0) & jnp.logical_not(last))
        def _():
            _normalise_rows(x_ref, g_ref, xn_ref, 0, x_ref.shape[0])

        @pl.when((j == 0) & last)
        def _():
            _normalise_rows(x_ref, g_ref, xn_ref, 0, tail_start)
            _normalise_rows(tail_ref, g_ref, xn_ref, tail_start, tail_ref.shape[0])

    acc = _dot(xn_ref[...], w_ref[...].astype(BF16))
    if norm_blocks == 0:
        o_ref[...] = acc
        return

    @pl.when(j < norm_blocks)
    def _():
        for h in range(acc.shape[1] // HEAD):
            cols = slice(h * HEAD, (h + 1) * HEAD)
            o_ref[:, cols] = _rms(acc[:, cols], hg_ref[...])

    @pl.when(j >= norm_blocks)
    def _():
        o_ref[...] = acc


def _norm_matmul(x, g, w, layer, head_gain=None, norm_blocks=0, tn_cap=512, name="norm_matmul",
                 tail=None):
    d = x.shape[1]
    m = x.shape[0] + (0 if tail is None else tail.shape[0])
    n = w.shape[-1]
    tm, tn = _row_tile(m), _col_tile(n, tn_cap)
    if head_gain is None:
        head_gain = jnp.ones((HEAD,), F32)
    if w.ndim == 2:
        w = w[None]
    tail_start = None
    if tail is not None:
        tail_start = x.shape[0] - (m // tm - 1) * tm
        assert tail_start > 0 and tail_start + tail.shape[0] == tm
        assert tail_start % NORM_ROWS == 0 and tail.shape[0] % NORM_ROWS == 0
    else:
        tail = jnp.zeros((NORM_ROWS, d), x.dtype)
    return pl.pallas_call(
        functools.partial(_norm_matmul_kernel, norm_blocks=norm_blocks, tail_start=tail_start),
        grid=(m // tm, n // tn),
        in_specs=[
            pl.BlockSpec((tm, d), lambda i, j: (i, 0), pipeline_mode=pl.Buffered(1)),
            pl.BlockSpec(tail.shape, lambda i, j: (0, 0)),
            pl.BlockSpec((1, d), lambda i, j: (0, 0)),
            pl.BlockSpec((None, d, tn), lambda i, j: (layer, 0, j)),
            pl.BlockSpec((1, HEAD), lambda i, j: (0, 0)),
        ],
        out_specs=pl.BlockSpec((tm, tn), lambda i, j: (i, j)),
        out_shape=jax.ShapeDtypeStruct((m, n), F32),
        scratch_shapes=[pltpu.VMEM((tm, d), BF16)],
        compiler_params=_params("parallel", "arbitrary"),
        name=name,
    )(x, tail, g.reshape(1, d), w, head_gain.reshape(1, HEAD))


def _kv_q_proj_kernel(x_ref, gkv_ref, gq_ref, wkv_ref, wq_ref, kn_ref, qn_ref, k_ref, kt_ref, v_ref,
                      vt_ref, q_ref, xkv_ref, xq_ref, *, tail_start):
    j = pl.program_id(1)
    last = pl.program_id(0) == pl.num_programs(0) - 1

    @pl.when(j == 0)
    def _():
        _normalise_rows(x_ref, gkv_ref, xkv_ref, 0, x_ref.shape[0])
        _normalise_rows(x_ref, gq_ref, xq_ref, 0, x_ref.shape[0])

    def head_norm(acc, gain_ref, o_ref):
        for h in range(acc.shape[1] // HEAD):
            cols = slice(h * HEAD, (h + 1) * HEAD)
            o_ref[:, cols] = _rms(acc[:, cols], gain_ref[...])

    def copy_tail(head_ref, tail_ref):
        @pl.when(last)
        def _():
            tail_ref[...] = head_ref[tail_start:tail_start + tail_ref.shape[0], :]

        @pl.when(jnp.logical_not(last))
        def _():
            tail_ref[...] = jnp.zeros_like(tail_ref)

    @pl.when(j == 0)
    def _():
        head_norm(_dot(xkv_ref[...], wkv_ref[...].astype(BF16)), kn_ref, k_ref)
        copy_tail(k_ref, kt_ref)

    @pl.when(j == 1)
    def _():
        v_ref[...] = _dot(xkv_ref[...], wkv_ref[...].astype(BF16))
        copy_tail(v_ref, vt_ref)

    @pl.when(j >= 2)
    def _():
        head_norm(_dot(xq_ref[...], wq_ref[...].astype(BF16)), qn_ref, q_ref)


def _kv_q_proj(x, g_kv, w_kv, k_norm, g_q, w_q, layer_q, q_norm, head_rows):
    m, d = x.shape
    tn, n_q = w_kv.shape[-1] // 2, w_q.shape[-1]
    tm = _row_tile(m, FFN_ROW_TILE_CAP)
    assert tn % HEAD == 0 and n_q % tn == 0
    n_tiles = m // tm
    tail_rows = m - head_rows
    tail_start = head_rows - (n_tiles - 1) * tm
    assert 0 < tail_start and tail_start + tail_rows == tm and tail_start % SUBLANES == 0
    rows_only = lambda i, j: (i, 0)
    k, kt, v, vt, q = pl.pallas_call(
        functools.partial(_kv_q_proj_kernel, tail_start=tail_start),
        grid=(n_tiles, 2 + n_q // tn),
        in_specs=[
            pl.BlockSpec((tm, d), lambda i, j: (i, 0), pipeline_mode=pl.Buffered(1)),
            pl.BlockSpec((1, d), lambda i, j: (0, 0)),
            pl.BlockSpec((1, d), lambda i, j: (0, 0)),
            pl.BlockSpec((d, tn), lambda i, j: (0, jnp.minimum(j, 1))),
            pl.BlockSpec((None, d, tn), lambda i, j: (layer_q, 0, jnp.maximum(j - 2, 0))),
            pl.BlockSpec((1, HEAD), lambda i, j: (0, 0)),
            pl.BlockSpec((1, HEAD), lambda i, j: (0, 0)),
        ],
        out_specs=[pl.BlockSpec((tm, tn), rows_only), pl.BlockSpec((tail_rows, tn), rows_only),
                   pl.BlockSpec((tm, tn), rows_only), pl.BlockSpec((tail_rows, tn), rows_only),
                   pl.BlockSpec((tm, tn), lambda i, j: (i, jnp.maximum(j - 2, 0)))],
        out_shape=[jax.ShapeDtypeStruct((head_rows, tn), F32),
                   jax.ShapeDtypeStruct((n_tiles * tail_rows, tn), F32),
                   jax.ShapeDtypeStruct((head_rows, tn), F32),
                   jax.ShapeDtypeStruct((n_tiles * tail_rows, tn), F32),
                   jax.ShapeDtypeStruct((m, n_q), F32)],
        scratch_shapes=[pltpu.VMEM((tm, d), BF16), pltpu.VMEM((tm, d), BF16)],
        compiler_params=_params("parallel", "arbitrary"),
        name="kv_q_proj",
    )(x, g_kv.reshape(1, d), g_q.reshape(1, d), w_kv, w_q, k_norm.reshape(1, HEAD),
      q_norm.reshape(1, HEAD))
    keep = (n_tiles - 1) * tail_rows
    return k, kt[keep:], v, vt[keep:], q


def _swiglu_up_kernel(x_ref, g_ref, wa_ref, wu_ref, o_ref, xn_ref):
    @pl.when(pl.program_id(1) == 0)
    def _():
        _normalise_rows(x_ref, g_ref, xn_ref, 0, x_ref.shape[0])

    xn = xn_ref[...]
    a = _dot(xn, wa_ref[...].astype(BF16))
    u = _dot(xn, wu_ref[...].astype(BF16))
    o_ref[...] = (a * _sigmoid(a) * u).astype(BF16)


def _swiglu_up(x, g, w_gu, layer, name):
    m, d = x.shape
    d_ff = w_gu.shape[-1] // 2
    tm, tn = _row_tile(m, FFN_ROW_TILE_CAP), _col_tile(d_ff, 512)
    nblk = d_ff // tn
    return pl.pallas_call(
        _swiglu_up_kernel,
        grid=(m // tm, nblk),
        in_specs=[
            pl.BlockSpec((tm, d), lambda i, j: (i, 0)),
            pl.BlockSpec((1, d), lambda i, j: (0, 0)),
            pl.BlockSpec((None, d, tn), lambda i, j: (layer, 0, j)),
            pl.BlockSpec((None, d, tn), lambda i, j: (layer, 0, j + nblk)),
        ],
        out_specs=pl.BlockSpec((tm, tn), lambda i, j: (i, j)),
        out_shape=jax.ShapeDtypeStruct((m, d_ff), BF16),
        scratch_shapes=[pltpu.VMEM((tm, d), BF16)],
        compiler_params=_params("parallel", "arbitrary"),
        name=name,
    )(x, g.reshape(1, d), w_gu, w_gu)


def _matmul_residual_kernel(a_ref, w_ref, r_ref, o_ref):
    o_ref[...] = r_ref[...] + _dot(a_ref[...], w_ref[...].astype(BF16))


def _matmul_residual_joined_kernel(a_ref, w_ref, r_ref, rt_ref, o_ref, *, tail_start):
    acc = _dot(a_ref[...], w_ref[...].astype(BF16))
    last = pl.program_id(0) == pl.num_programs(0) - 1

    @pl.when(jnp.logical_not(last))
    def _():
        o_ref[...] = r_ref[...] + acc

    @pl.when(last)
    def _():
        o_ref[:tail_start, :] = r_ref[:tail_start, :] + acc[:tail_start, :]
        o_ref[tail_start:, :] = rt_ref[...] + acc[tail_start:, :]


def _matmul_residual_split_kernel(a_ref, w_ref, r_ref, head_ref, tail_ref, *, tail_start):
    out = r_ref[...] + _dot(a_ref[...], w_ref[...].astype(BF16))
    head_ref[...] = out
    last = pl.program_id(0) == pl.num_programs(0) - 1

    @pl.when(last)
    def _():
        tail_ref[...] = out[tail_start:tail_start + tail_ref.shape[0], :]

    @pl.when(jnp.logical_not(last))
    def _():
        tail_ref[...] = jnp.zeros_like(tail_ref)


def _matmul_residual(a, w, layer, res, tn_cap, name, tm_cap=ROW_TILE_CAP, head_rows=None,
                     res_tail=None):
    m, k = a.shape
    n = w.shape[-1]
    tm, tn = _row_tile(m, tm_cap), _col_tile(n, tn_cap)
    if res_tail is not None:
        tail_start = res.shape[0] - (m // tm - 1) * tm
        assert tail_start > 0 and tail_start + res_tail.shape[0] == tm
        assert tail_start % SUBLANES == 0 and head_rows is None
        return pl.pallas_call(
            functools.partial(_matmul_residual_joined_kernel, tail_start=tail_start),
            grid=(m // tm, n // tn),
            in_specs=[
                pl.BlockSpec((tm, k), lambda i, j: (i, 0)),
                pl.BlockSpec((None, k, tn), lambda i, j: (layer, 0, j)),
                pl.BlockSpec((tm, tn), lambda i, j: (i, j)),
                pl.BlockSpec((res_tail.shape[0], tn), lambda i, j: (0, j)),
            ],
            out_specs=pl.BlockSpec((tm, tn), lambda i, j: (i, j)),
            out_shape=jax.ShapeDtypeStruct((m, n), F32),
            compiler_params=_params("parallel", "arbitrary"),
            name=name,
        )(a, w, res, res_tail)
    in_specs = [
        pl.BlockSpec((tm, k), lambda i, j: (i, 0)),
        pl.BlockSpec((None, k, tn), lambda i, j: (layer, 0, j)),
        pl.BlockSpec((tm, tn), lambda i, j: (i, j)),
    ]
    if head_rows is None:
        return pl.pallas_call(
            _matmul_residual_kernel,
            grid=(m // tm, n // tn),
            in_specs=in_specs,
            out_specs=pl.BlockSpec((tm, tn), lambda i, j: (i, j)),
            out_shape=jax.ShapeDtypeStruct((m, n), F32),
            compiler_params=_params("parallel", "arbitrary"),
            name=name,
        )(a, w, res)
    tail_rows = m - head_rows
    tail_start = head_rows - (m // tm - 1) * tm
    assert 0 < tail_start and tail_start + tail_rows == tm and tail_start % SUBLANES == 0
    head, tail = pl.pallas_call(
        functools.partial(_matmul_residual_split_kernel, tail_start=tail_start),
        grid=(m // tm, n // tn),
        in_specs=in_specs,
        out_specs=[pl.BlockSpec((tm, tn), lambda i, j: (i, j)),
                   pl.BlockSpec((tail_rows, tn), lambda i, j: (i, j))],
        out_shape=[jax.ShapeDtypeStruct((head_rows, n), F32),
                   jax.ShapeDtypeStruct((m // tm * tail_rows, n), F32)],
        compiler_params=_params("parallel", "arbitrary"),
        name=name,
    )(a, w, res)
    return head, tail[(m // tm - 1) * tail_rows:]


def _lower_bound(logits, layer):
    e = jnp.exp(logits - jnp.max(logits, axis=0, keepdims=True))
    return jnp.sum(e[:layer + 1], axis=0, keepdims=True) / jnp.sum(e, axis=0, keepdims=True)


def _gate_inputs(zq, zf, lb):
    f = lb + (1.0 - lb) * _sigmoid(zf)
    return zq * _sigmoid(zq), 1.0 - f, jnp.log(f)


def _gated_output(o, zg, gain):
    return _rms(o, gain) * (zg * _sigmoid(zg))


def _cumsum_rows(g, seg=None):
    n = g.shape[0]
    row = lax.broadcasted_iota(jnp.int32, g.shape, 0)
    pos = row if seg is None else _div_mod(row, seg)[1]
    span = n if seg is None else seg
    s = 1
    while s < span:
        g = g + jnp.where(pos >= s, pltpu.roll(g, s, 0), 0.0)
        s *= 2
    return g


def _gla_prompt_kernel(zq_ref, zf_ref, zi_ref, zg_ref, lbl_ref, gain_ref, o_init_ref, o_ref, s_ref,
                       st_ref, *, layer, hp):
    del o_init_ref
    t = pl.program_id(1)
    c = GLA_CHUNK
    mid = (c - 1) // 2

    @pl.when(t == 0)
    def _():
        st_ref[...] = jnp.zeros_like(st_ref)

    lb_all = _lower_bound(lbl_ref[...], layer)
    gain = gain_ref[...]
    r2 = lax.broadcasted_iota(jnp.int32, (c, c), 0)
    c2 = lax.broadcasted_iota(jnp.int32, (c, c), 1)
    st = [st_ref[j] for j in range(hp)]
    for i in range(o_ref.shape[0] // c):
        rows = slice(i * c, (i + 1) * c)
        for j in range(hp):
            cols = slice(j * HEAD, (j + 1) * HEAD)
            q, k, g = _gate_inputs(zq_ref[rows, cols], zf_ref[rows, cols], lb_all[:, cols])
            v16 = zi_ref[rows, cols].astype(BF16)
            cum = _cumsum_rows(g)
            ref = cum[mid:mid + 1, :]
            last = cum[c - 1:c, :]
            att = _dot_nt((q * jnp.exp(cum - ref)).astype(BF16),
                          (k * jnp.exp(ref - cum)).astype(BF16))
            att = jnp.where(r2 >= c2, att, 0.0)
            o = _dot(att.astype(BF16), v16) + _dot_nt((q * jnp.exp(cum)).astype(BF16),
                                                       st[j].astype(BF16))
            k_out = (k * jnp.exp(last - cum)).astype(BF16)
            st[j] = jnp.exp(last) * st[j] + _dot_tn(v16, k_out)
            o_ref[rows, cols] = _gated_output(o, zg_ref[rows, cols], gain).astype(BF16)
    for j in range(hp):
        st_ref[j] = st[j]

    @pl.when(t == pl.num_programs(1) - 1)
    def _():
        for j in range(hp):
            s_ref[j] = st[j].T


def _gla_prompt(z, lb_logits, gain, layer, seq, tl=512, hp=GLA_HEADS_PER_STEP):
    heads = z.shape[1] // (4 * HEAD)
    hp = _unroll(heads, hp)
    tl = min(tl, seq)
    slots = lb_logits.shape[0]
    groups = heads // hp
    zspec = lambda off: pl.BlockSpec((tl, hp * HEAD), lambda h, t: (t, h + off * groups))
    return pl.pallas_call(
        functools.partial(_gla_prompt_kernel, layer=layer, hp=hp),
        grid=(groups, seq // tl),
        in_specs=[zspec(0), zspec(1), zspec(2), zspec(3),
                  pl.BlockSpec((slots, hp * HEAD), lambda h, t: (0, h)),
                  pl.BlockSpec((1, HEAD), lambda h, t: (0, 0)),
                  pl.BlockSpec(memory_space=pl.ANY)],
        out_specs=[pl.BlockSpec((tl, hp * HEAD), lambda h, t: (t, h)),
                   pl.BlockSpec((hp, HEAD, HEAD), lambda h, t: (h, 0, 0))],
        out_shape=[jax.ShapeDtypeStruct((z.shape[0], heads * HEAD), BF16),
                   jax.ShapeDtypeStruct((heads, HEAD, HEAD), F32)],
        scratch_shapes=[pltpu.VMEM((hp, HEAD, HEAD), F32)],
        input_output_aliases={6: 0},
        compiler_params=_params("parallel", "arbitrary"),
        name="gla_prompt",
    )(z, z, z, z, lb_logits, gain.reshape(1, HEAD), jnp.zeros((z.shape[0], heads * HEAD), BF16))


def _gla_sample_kernel(zq_ref, zf_ref, zi_ref, zg_ref, lbl_ref, gain_ref, s0_ref, o_in_ref,
                       o_ref, s_ref, last_ref, kout_ref, qin_ref, v_ref, *, layer, steps):
    del o_in_ref
    rows = zq_ref.shape[0]
    mid = (steps - 1) // 2
    lb = _lower_bound(lbl_ref[...], layer)
    q, k, g = _gate_inputs(zq_ref[...], zf_ref[...], lb)
    v16 = zi_ref[...].astype(BF16)
    cum = _cumsum_rows(g, seg=steps)

    row = lax.broadcasted_iota(jnp.int32, cum.shape, 0)
    pos = _div_mod(row, steps)[1]

    def spread(src_pos):
        picked = jnp.where(pos == src_pos, cum, 0.0)
        out = picked
        for d in range(steps):
            if d != src_pos:
                out = out + pltpu.roll(picked, (d - src_pos) % rows, 0)
        return out

    ref = spread(mid)
    last = spread(steps - 1)
    r2 = lax.broadcasted_iota(jnp.int32, (rows, rows), 0)
    c2 = lax.broadcasted_iota(jnp.int32, (rows, rows), 1)
    att = _dot_nt((q * jnp.exp(cum - ref)).astype(BF16), (k * jnp.exp(ref - cum)).astype(BF16))
    att = jnp.where((r2 >= c2) & (_div_mod(r2, steps)[0] == _div_mod(c2, steps)[0]), att, 0.0)
    o_intra = _dot(att.astype(BF16), v16)
    qin_ref[...] = (q * jnp.exp(cum)).astype(BF16)
    v_ref[...] = v16
    last_ref[...] = last
    kout_ref[...] = k * jnp.exp(last - cum)
    n_seq = s0_ref.shape[0]

    def body(b, o_inter):
        lo = b * steps
        mine = (row >= lo) & (row < lo + steps)
        st = s0_ref[b]
        o_inter = o_inter + jnp.where(mine, _dot(qin_ref[...], st.astype(BF16)), 0.0)
        k_out = jnp.where(mine, kout_ref[...], 0.0).astype(BF16)
        decay_row = jnp.exp(last_ref[pl.ds(lo, 1), :])
        decay_col = jnp.broadcast_to(decay_row, (SUBLANES, HEAD)).T[:, :1]
        s_ref[b] = decay_col * st + _dot_tn(k_out, v_ref[...])
        return o_inter

    o_inter = lax.fori_loop(0, n_seq, body, jnp.zeros(cum.shape, F32),
                            unroll=_unroll(n_seq, SAMPLE_SEQ_UNROLL))
    o_ref[...] = _gated_output(o_intra + o_inter, zg_ref[...], gain_ref[...]).astype(BF16)


def _gla_sample(z, o_all, row0, n_seq, steps, s0, layer_s0, lb_logits, gain, layer):
    heads = z.shape[1] // (4 * HEAD)
    rows = n_seq * steps
    slots = lb_logits.shape[0]
    rb = row0 // rows
    zspec = lambda off: pl.BlockSpec((rows, HEAD), lambda h: (rb, h + off * heads))
    s_in = pl.BlockSpec((None, n_seq, None, HEAD, HEAD), lambda h: (layer_s0, 0, h, 0, 0))
    s_out = pl.BlockSpec((n_seq, None, HEAD, HEAD), lambda h: (0, h, 0, 0))
    return pl.pallas_call(
        functools.partial(_gla_sample_kernel, layer=layer, steps=steps),
        grid=(heads,),
        in_specs=[zspec(0), zspec(1), zspec(2), zspec(3),
                  pl.BlockSpec((slots, HEAD), lambda h: (0, h)),
                  pl.BlockSpec((1, HEAD), lambda h: (0, 0)),
                  s_in,
                  pl.BlockSpec(memory_space=pl.ANY)],
        out_specs=[pl.BlockSpec((rows, HEAD), lambda h: (rb, h)), s_out],
        out_shape=[jax.ShapeDtypeStruct(o_all.shape, o_all.dtype),
                   jax.ShapeDtypeStruct(s0.shape[1:], s0.dtype)],
        scratch_shapes=[pltpu.VMEM((rows, HEAD), F32), pltpu.VMEM((rows, HEAD), F32),
                        pltpu.VMEM((rows, HEAD), BF16), pltpu.VMEM((rows, HEAD), BF16)],
        input_output_aliases={7: 0},
        compiler_params=_params("parallel"),
        name="gla_sample",
    )(z, z, z, z, lb_logits, gain.reshape(1, HEAD), s0, o_all)


def _top_blocks_bits(gate_t, blk_f, n_valid):
    gate_t = jnp.where(blk_f < n_valid, gate_t, -jnp.inf)
    bits = jnp.zeros((1, gate_t.shape[1]), jnp.int32)
    for _ in range(MOBA_TOPK):
        mx = jnp.max(gate_t, axis=0, keepdims=True)
        idx = jnp.min(jnp.where(gate_t == mx, blk_f, float(LANES)), axis=0, keepdims=True)
        live = mx > -jnp.inf
        gate_t = jnp.where(blk_f == idx, -jnp.inf, gate_t)
        shift = jnp.minimum(idx, 31.0).astype(jnp.int32)
        bits = bits | jnp.where(live, jnp.left_shift(1, shift), 0)
    return bits


def _moba_prompt_kernel(q_ref, k_ref, v_ref, o_init_ref, o_ref, kmean_ref, qb_ref, sel_ref, m_ref,
                        l_ref, acc_ref, *, n_blocks, group, chunk):
    del o_init_ref
    qi = pl.program_id(1)
    blk = MOBA_BLOCK
    rows = group * blk
    n_chunks = rows // chunk

    @pl.when(qi == 0)
    def _():
        kmean_ref[...] = jnp.zeros_like(kmean_ref)
        for n in range(n_blocks):
            kmean_ref[n:n + 1, :] = jnp.mean(k_ref[n * blk:(n + 1) * blk, :], axis=0, keepdims=True)

    blk_f = lax.broadcasted_iota(jnp.int32, (kmean_ref.shape[0], blk), 0).astype(F32)
    qi_f = qi.astype(F32)
    for g in range(group):
        qg = q_ref[:, g * HEAD:(g + 1) * HEAD]
        qb_ref[g * blk:(g + 1) * blk, :] = (qg * SCALE_LOG2E).astype(BF16)
        gate_t = _dot_nt(kmean_ref[...], qg, precision=lax.Precision.HIGHEST)
        bits = _top_blocks_bits(gate_t, blk_f, qi_f)
        per_row = jnp.broadcast_to(bits, (SUBLANES, blk)).T
        sel_ref[g * blk:(g + 1) * blk, :] = jnp.broadcast_to(per_row[:, :1], (blk, LANES))

    own = pl.ds(pl.multiple_of(qi * blk, blk), blk)
    kd = k_ref[own, :].astype(BF16)
    vd = v_ref[own, :].astype(BF16)
    own_chunk = min(PROMPT_OWN_CHUNK, blk)
    for c in range(rows // own_chunk):
        r = slice(c * own_chunk, (c + 1) * own_chunk)
        q_pos = lax.broadcasted_iota(jnp.int32, (own_chunk, blk), 0) + (c * own_chunk) % blk
        k_pos = lax.broadcasted_iota(jnp.int32, (own_chunk, blk), 1)
        s = jnp.where(k_pos <= q_pos, _dot_nt(qb_ref[r, :], kd), NEG)
        m = jnp.max(s, axis=1, keepdims=True)
        p = jnp.exp2(s - m)
        m_ref[r, :] = jnp.broadcast_to(m, (own_chunk, LANES))
        l_ref[r, :] = jnp.broadcast_to(jnp.sum(p, axis=1, keepdims=True), (own_chunk, LANES))
        acc_ref[r, :] = _dot(p.astype(BF16), vd)

    def attend(first_pair, n_pairs):
        blocks = []
        for t in range(n_pairs):
            ja = 2 * (first_pair + t)
            jb = jnp.minimum(ja + 1, qi - 1)
            bit_a = jnp.left_shift(jnp.int32(1), ja)
            bit_b = jnp.where(ja + 1 < qi, jnp.left_shift(jnp.int32(1), jb), 0)
            rows_a = pl.ds(pl.multiple_of(ja * blk, blk), blk)
            rows_b = pl.ds(pl.multiple_of(jb * blk, blk), blk)
            blocks.append((bit_a, bit_b,
                           k_ref[rows_a, :].astype(BF16), k_ref[rows_b, :].astype(BF16),
                           v_ref[rows_a, :].astype(BF16), v_ref[rows_b, :].astype(BF16)))
        work = [(t, c) for t in range(n_pairs) for c in range(n_chunks)]

        def scores(item):
            t, c = item
            qc = qb_ref[c * chunk:(c + 1) * chunk, :]
            return _dot_nt(qc, blocks[t][2]), _dot_nt(qc, blocks[t][3])

        pending = [scores(w) for w in work[:SCORE_LOOKAHEAD]]
        for n, (t, c) in enumerate(work):
            bit_a, bit_b, _, _, va, vb = blocks[t]
            r = slice(c * chunk, (c + 1) * chunk)
            sel = sel_ref[r, :]
            on_a = (sel & bit_a) != 0
            on_b = (sel & bit_b) != 0
            sa, sb = pending.pop(0)
            if n + SCORE_LOOKAHEAD < len(work):
                pending.append(scores(work[n + SCORE_LOOKAHEAD]))
            parts = [jnp.where(on_a, sa[:, :LANES], NEG), jnp.where(on_a, sa[:, LANES:], NEG),
                     jnp.where(on_b, sb[:, :LANES], NEG), jnp.where(on_b, sb[:, LANES:], NEG)]
            m_prev = m_ref[r, :]
            top = jnp.maximum(jnp.maximum(parts[0], parts[1]), jnp.maximum(parts[2], parts[3]))
            m_new = jnp.maximum(m_prev, jnp.max(top, axis=1, keepdims=True))
            ps = [jnp.exp2(x - m_new) for x in parts]
            alpha = jnp.exp2(m_prev - m_new)
            l_ref[r, :] = alpha * l_ref[r, :] + jnp.sum((ps[0] + ps[1]) + (ps[2] + ps[3]), axis=1,
                                                        keepdims=True)
            pa = jnp.concatenate(ps[:2], axis=1).astype(BF16)
            pb = jnp.concatenate(ps[2:], axis=1).astype(BF16)
            acc_ref[r, :] = alpha * acc_ref[r, :] + (_dot(pa, va) + _dot(pb, vb))
            m_ref[r, :] = m_new

    n_pairs = (qi + 1) // 2
    trips = n_pairs // PAIRS_PER_TRIP

    def body(i, carry):
        attend(i * PAIRS_PER_TRIP, PAIRS_PER_TRIP)
        return carry

    lax.fori_loop(0, trips, body, 0)
    for left in range(1, PAIRS_PER_TRIP):
        @pl.when(n_pairs - trips * PAIRS_PER_TRIP == left)
        def _():
            attend(trips * PAIRS_PER_TRIP, left)

    for g in range(group):
        r = slice(g * blk, (g + 1) * blk)
        o_ref[:, g * HEAD:(g + 1) * HEAD] = (acc_ref[r, :] / l_ref[r, :]).astype(BF16)


def _moba_prompt(q, k, v, seq, n_kv, chunk=PROMPT_ROW_CHUNK):
    heads = q.shape[1] // HEAD
    group = heads // n_kv
    n_blocks = seq // MOBA_BLOCK
    blk = MOBA_BLOCK
    rows = group * blk
    return pl.pallas_call(
        functools.partial(_moba_prompt_kernel, n_blocks=n_blocks, group=group, chunk=chunk),
        grid=(n_kv, n_blocks),
        in_specs=[pl.BlockSpec((blk, group * HEAD), lambda h, i: (i, h)),
                  pl.BlockSpec((seq, HEAD), lambda h, i: (0, h)),
                  pl.BlockSpec((seq, HEAD), lambda h, i: (0, h)),
                  pl.BlockSpec(memory_space=pl.ANY)],
        out_specs=pl.BlockSpec((blk, group * HEAD), lambda h, i: (i, h)),
        out_shape=jax.ShapeDtypeStruct((q.shape[0], heads * HEAD), BF16),
        scratch_shapes=[pltpu.VMEM((-(-n_blocks // SUBLANES) * SUBLANES, HEAD), F32),
                        pltpu.VMEM((rows, HEAD), BF16),
                        pltpu.VMEM((rows, LANES), jnp.int32),
                        pltpu.VMEM((rows, LANES), F32),
                        pltpu.VMEM((rows, LANES), F32),
                        pltpu.VMEM((rows, HEAD), F32)],
        input_output_aliases={3: 0},
        compiler_params=_params("arbitrary", "arbitrary"),
        name="moba_prompt",
    )(q, k, v, jnp.zeros((q.shape[0], heads * HEAD), BF16))


def _moba_sample_kernel(pt_ref, q_ref, kn_ref, vn_ref, *refs, n_kv, group, steps, pages_per_block,
                        blocks_per_step):
    del pt_ref
    n_pages = pages_per_block * blocks_per_step
    kp = refs[:n_pages]
    vp = refs[n_pages:2 * n_pages]
    o_ref, m_s, l_s, g_s, o_s, bias_s = refs[2 * n_pages:]
    j = pl.program_id(1)
    rows = q_ref.shape[1]
    rq = group * steps
    tok_per_vreg = SUBLANES // n_kv
    n_col = bias_s.shape[1]

    @pl.when((pl.program_id(0) == 0) & (j == 0))
    def _():
        row_head = _div_mod(lax.broadcasted_iota(jnp.int32, (rows, n_col), 0), rq)[0]
        col_head = _div_mod(lax.broadcasted_iota(jnp.int32, (rows, n_col), 1), n_kv)[1]
        bias_s[...] = jnp.where(row_head == col_head, 0.0, NEG)

    qf = q_ref[0]
    q16 = qf.astype(BF16)
    head64 = _div_mod(lax.broadcasted_iota(jnp.int32, (rows, HEAD), 0), rq)[0]
    kbs, vbs, scores = [], [], []
    for t in range(blocks_per_step):
        pages = slice(t * pages_per_block, (t + 1) * pages_per_block)
        kbs.append(jnp.concatenate([r[0] for r in kp[pages]], axis=0))
        vbs.append(jnp.concatenate([r[0] for r in vp[pages]], axis=0))
        scores.append(_dot_nt(q16, kbs[t].astype(BF16)))
    for t in range(blocks_per_step):
        kb = kbs[t]
        s = scores[t] * SCALE + bias_s[...]
        m = jnp.max(s, axis=1, keepdims=True)
        p = jnp.exp(s - m)

        parts = [kb[SUBLANES * i:SUBLANES * (i + 1), :] for i in range(n_col // SUBLANES)]
        while len(parts) > 1:
            parts = [a + b for a, b in zip(parts[::2], parts[1::2])]
        folded = parts[0]
        kmean = jnp.zeros((rows, HEAD), F32)
        for h in range(n_kv):
            total = folded[h:h + 1, :]
            for i in range(1, tok_per_vreg):
                total = total + folded[h + i * n_kv:h + i * n_kv + 1, :]
            kmean = jnp.where(head64 == h, total * (1.0 / MOBA_BLOCK), kmean)
        gate = jnp.sum(qf * kmean, axis=1, keepdims=True)

        n = j * blocks_per_step + t
        m_s[n] = jnp.broadcast_to(m, (rows, LANES))
        l_s[n] = jnp.broadcast_to(jnp.sum(p, axis=1, keepdims=True), (rows, LANES))
        g_s[n] = jnp.broadcast_to(gate, (rows, LANES))
        o_s[n] = _dot(p.astype(BF16), vbs[t].astype(BF16))

    @pl.when(j == pl.num_programs(1) - 1)
    def _():
        gates = g_s[...]
        n_io = lax.broadcasted_iota(jnp.int32, gates.shape, 0)
        sel = jnp.zeros(gates.shape, jnp.bool_)
        for _ in range(min(MOBA_TOPK, g_s.shape[0])):
            mx = jnp.max(gates, axis=0, keepdims=True)
            idx = jnp.min(jnp.where(gates == mx, n_io, g_s.shape[0]), axis=0, keepdims=True)
            pick = n_io == idx
            sel = sel | pick
            gates = jnp.where(pick, -jnp.inf, gates)

        row = lax.broadcasted_iota(jnp.int32, (rows, LANES), 0)
        lane = lax.broadcasted_iota(jnp.int32, (rows, LANES), 1)
        q_head, q_step = _div_mod(row, rq)[0], _div_mod(row, steps)[1]
        k_step, k_head = _div_mod(lane, n_kv)
        ok = (q_head == k_head) & (k_step <= q_step) & (k_step < steps)
        s_own = jnp.where(ok, _dot_nt(q16, kn_ref[0].astype(BF16)) * SCALE, NEG)
        m_own = jnp.max(s_own, axis=1, keepdims=True)
        p_own = jnp.exp(s_own - m_own)
        l_own = jnp.sum(p_own, axis=1, keepdims=True)
        o_own = _dot(p_own.astype(BF16), vn_ref[0].astype(BF16))

        m_all = m_s[...]
        m_top = jnp.maximum(m_own, jnp.max(jnp.where(sel, m_all, -jnp.inf), axis=0))
        w = jnp.where(sel, jnp.exp(jnp.minimum(m_all - m_top[None], 0.0)), 0.0)
        w_own = jnp.exp(m_own - m_top)
        den = w_own * l_own + jnp.sum(w * l_s[...], axis=0)
        num = w_own * o_own + jnp.sum(w * o_s[...], axis=0)
        o_ref[0] = num / den


def _moba_sample(q, k_new, v_new, cache_k, cache_v, page_table, n_kv, steps):
    n_seq, n_pages = page_table.shape
    n_phys, page, _, _ = cache_k.shape
    ppb = MOBA_BLOCK // page
    n_blocks = n_pages // ppb
    bps = _unroll(n_blocks, SAMPLE_BLOCKS_PER_STEP)
    rows = q.shape[1]
    group = rows // (n_kv * steps)
    assert SUBLANES % n_kv == 0 and steps * n_kv <= LANES
    ck = cache_k.reshape(n_phys, page * n_kv, HEAD)
    cv = cache_v.reshape(n_phys, page * n_kv, HEAD)
    pages_per_step = ppb * bps

    def page_spec(p):
        return pl.BlockSpec((1, page * n_kv, HEAD),
                            lambda b, j, pt: (pt[b, j * pages_per_step + p], 0, 0))

    per_seq = lambda shape: pl.BlockSpec(shape, lambda b, j, pt: (b, 0, 0))
    grid_spec = pltpu.PrefetchScalarGridSpec(
        num_scalar_prefetch=1,
        grid=(n_seq, n_blocks // bps),
        in_specs=[per_seq((1, rows, HEAD)), per_seq((1, LANES, HEAD)), per_seq((1, LANES, HEAD))]
                 + [page_spec(p) for p in range(pages_per_step)] * 2,
        out_specs=per_seq((1, rows, HEAD)),
        scratch_shapes=[pltpu.VMEM((n_blocks, rows, LANES), F32) for _ in range(4)]
                       + [pltpu.VMEM((rows, MOBA_BLOCK * n_kv), F32)],
    )
    return pl.pallas_call(
        functools.partial(_moba_sample_kernel, n_kv=n_kv, group=group, steps=steps,
                          pages_per_block=ppb, blocks_per_step=bps),
        grid_spec=grid_spec,
        out_shape=jax.ShapeDtypeStruct((n_seq, rows, HEAD), F32),
        compiler_params=_params("arbitrary", "arbitrary"),
        name="moba_sample",
    )(page_table, q, k_new, v_new, *([ck] * pages_per_step), *([cv] * pages_per_step))


def kernel(x_prompt, x_sample, state_hgrn, cache_k, cache_v, page_table, norm_mix_a, w_in_a, lb_logits,
           onorm_a, w_out_a, norm_kv, w_kv, k_norm, norm_mix_b, w_q_b, q_norm, w_o_b, norm_ffn,
           w_gate_up, w_down):
    batch, seq, d = x_prompt.shape
    n_seq, steps, _ = x_sample.shape
    n_a = w_in_a.shape[0]
    depth = norm_ffn.shape[0]
    heads = d // HEAD
    n_kv = cache_k.shape[2]
    group = heads // n_kv
    n_dec = n_seq * steps
    past = page_table.shape[1] * cache_k.shape[1]
    assert batch == 1 and seq % MOBA_BLOCK == 0 and seq // MOBA_BLOCK <= 32
    assert past % MOBA_BLOCK == 0 and MOBA_BLOCK % cache_k.shape[1] == 0
    assert steps <= LANES and seq % n_dec == 0 and seq % GLA_CHUNK == 0

    x_p, x_s = x_prompt.reshape(seq, d), x_sample.reshape(n_dec, d)
    h = None
    states_p, states_s = [], []
    k_p = v_p = k_d = v_d = None
    for layer in range(depth):
        if layer < n_a:
            if h is None:
                z = _norm_matmul(x_p, norm_mix_a[layer], w_in_a, layer, name="hgrn_in", tail=x_s)
            else:
                z = _norm_matmul(h, norm_mix_a[layer], w_in_a, layer, name="hgrn_in")
            o, s_p = _gla_prompt(z, lb_logits, onorm_a[layer], layer, seq)
            o, s_s = _gla_sample(z, o, seq, n_seq, steps, state_hgrn, layer, lb_logits,
                                 onorm_a[layer], layer)
            if h is None:
                h = _matmul_residual(o, w_out_a, layer, x_p, 512, "hgrn_out", res_tail=x_s)
            else:
                h = _matmul_residual(o, w_out_a, layer, h, 512, "hgrn_out")
            states_p.append(s_p.reshape(1, heads, HEAD, HEAD).astype(state_hgrn.dtype))
            states_s.append(s_s)
        else:
            if h is None:
                h = jnp.concatenate([x_p, x_s], axis=0)
            jb = layer - n_a
            if k_p is None:
                k_p, k_d, v_p, v_d, q = _kv_q_proj(h, norm_kv, w_kv, k_norm, norm_mix_b[jb], w_q_b,
                                                   jb, q_norm[jb], seq)
            else:
                q = _norm_matmul(h, norm_mix_b[jb], w_q_b, jb, head_gain=q_norm[jb],
                                 norm_blocks=heads, name="q_proj")
            att = _moba_prompt(q, k_p, v_p, seq, n_kv)
            q_s = q[seq:].reshape(n_seq, steps, n_kv, group, HEAD).transpose(0, 2, 3, 1, 4)
            pad = ((0, 0), (0, LANES - steps * n_kv), (0, 0))
            k_s = jnp.pad(k_d.reshape(n_seq, steps * n_kv, HEAD), pad)
            v_s = jnp.pad(v_d.reshape(n_seq, steps * n_kv, HEAD), pad)
            att_s = _moba_sample(q_s.reshape(n_seq, n_kv * group * steps, HEAD), k_s, v_s,
                                 cache_k, cache_v, page_table, n_kv, steps)
            att_s = att_s.reshape(n_seq, n_kv, group, steps, HEAD).transpose(0, 3, 1, 2, 4)
            att = lax.dynamic_update_slice(att, att_s.reshape(n_dec, d).astype(BF16), (seq, 0))
            h = _matmul_residual(att, w_o_b, jb, h, 512, "attn_out")
        hf = _swiglu_up(h, norm_ffn[layer], w_gate_up, layer, "ffn_up")
        if layer + 1 < depth:
            h = _matmul_residual(hf, w_down, layer, h, 256, "ffn_down", tm_cap=FFN_ROW_TILE_CAP)
        else:
            y_p, y_s = _matmul_residual(hf, w_down, layer, h, 256, "ffn_down",
                                        tm_cap=FFN_ROW_TILE_CAP, head_rows=seq)

    return (y_p.reshape(batch, seq, d),
            y_s.reshape(n_seq, steps, d),
            jnp.stack(states_p),
            jnp.stack(states_s),
            k_p.reshape(batch, seq, n_kv, HEAD),
            v_p.reshape(batch, seq, n_kv, HEAD),
            k_d.reshape(n_seq, steps, n_kv, HEAD),
            v_d.reshape(n_seq, steps, n_kv, HEAD))
```

```python
import functools

import jax
import jax.numpy as jnp
from jax import lax
from jax.experimental import pallas as pl
from jax.experimental.pallas import tpu as pltpu

F32 = jnp.float32
BF16 = jnp.bfloat16

EPS = 1e-6
HEAD = 128
GLA_CHUNK = 64
MOBA_BLOCK = 256
MOBA_TOPK = 3
SCALE = HEAD ** -0.5
SCALE_LOG2E = SCALE * 1.4426950408889634
NEG = -1e30
LANES = 128
SUBLANES = 8
GLA_HEADS_PER_STEP = 4
NORM_ROWS = 16
NORM_UNROLL = 5
ROW_TILE_CAP = 2080
FFN_ROW_TILE_CAP = 1040
PROMPT_ROW_CHUNK = 256
PROMPT_OWN_CHUNK = 128
SCORE_LOOKAHEAD = 2
PAIRS_PER_TRIP = 4
SAMPLE_BLOCKS_PER_STEP = 8
SAMPLE_SEQ_UNROLL = 4
VMEM_LIMIT = 56 * 1024 * 1024


def _dot(a, b):
    return jnp.dot(a, b, preferred_element_type=F32)


def _dot_nt(a, b, precision=None):
    return lax.dot_general(a, b, (((1,), (1,)), ((), ())), precision=precision,
                           preferred_element_type=F32)


def _dot_tn(a, b):
    return lax.dot_general(a, b, (((0,), (0,)), ((), ())), preferred_element_type=F32)


def _div_mod(x, n):
    if n & (n - 1) == 0:
        return jnp.right_shift(x, n.bit_length() - 1), x & (n - 1)
    q = x // n
    return q, x - q * n


def _sigmoid(x):
    return 1.0 / (1.0 + jnp.exp(-x))


def _rms(x, g):
    return x * lax.rsqrt(jnp.mean(x * x, axis=-1, keepdims=True) + EPS) * g


def _row_tile(m, cap=ROW_TILE_CAP):
    for t in range(min(m, cap), 0, -1):
        if m % t == 0 and t % NORM_ROWS == 0:
            return t
    raise ValueError(f"no row tile for {m} rows")


def _col_tile(n, cap):
    for t in range(min(n, cap), 0, -1):
        if n % t == 0 and t % LANES == 0:
            return t
    raise ValueError(f"no column tile for {n} columns")


def _unroll(trips, cap):
    for u in range(min(trips, cap), 0, -1):
        if trips % u == 0:
            return u
    return 1


def _params(*sem):
    return pltpu.CompilerParams(dimension_semantics=sem, vmem_limit_bytes=VMEM_LIMIT)


def _normalise_rows(x_ref, g_ref, xn_ref, dst_row, n_rows):
    g = g_ref[...]
    trips = n_rows // NORM_ROWS

    def body(i, carry):
        src = pl.ds(pl.multiple_of(i * NORM_ROWS, NORM_ROWS), NORM_ROWS)
        dst = pl.ds(pl.multiple_of(dst_row + i * NORM_ROWS, NORM_ROWS), NORM_ROWS)
        xn_ref[dst, :] = _rms(x_ref[src, :], g).astype(BF16)
        return carry

    lax.fori_loop(0, trips, body, 0, unroll=_unroll(trips, NORM_UNROLL))


def _norm_matmul_kernel(x_ref, tail_ref, g_ref, w_ref, hg_ref, o_ref, xn_ref, *, norm_blocks,
                        tail_start):
    j = pl.program_id(1)

    if tail_start is None:
        @pl.when(j == 0)
        def _():
            _normalise_rows(x_ref, g_ref, xn_ref, 0, x_ref.shape[0])
    else:
        last = pl.program_id(0) == pl.num_programs(0) - 1

        @pl.when((j == 0) & jnp.logical_not(last))
        def _():
            _normalise_rows(x_ref, g_ref, xn_ref, 0, x_ref.shape[0])

        @pl.when((j == 0) & last)
        def _():
            _normalise_rows(x_ref, g_ref, xn_ref, 0, tail_start)
            _normalise_rows(tail_ref, g_ref, xn_ref, tail_start, tail_ref.shape[0])

    acc = _dot(xn_ref[...], w_ref[...].astype(BF16))
    if norm_blocks == 0:
        o_ref[...] = acc
        return

    @pl.when(j < norm_blocks)
    def _():
        for h in range(acc.shape[1] // HEAD):
            cols = slice(h * HEAD, (h + 1) * HEAD)
            o_ref[:, cols] = _rms(acc[:, cols], hg_ref[...])

    @pl.when(j >= norm_blocks)
    def _():
        o_ref[...] = acc


def _norm_matmul(x, g, w, layer, head_gain=None, norm_blocks=0, tn_cap=512, name="norm_matmul",
                 tail=None):
    d = x.shape[1]
    m = x.shape[0] + (0 if tail is None else tail.shape[0])
    n = w.shape[-1]
    tm, tn = _row_tile(m), _col_tile(n, tn_cap)
    if head_gain is None:
        head_gain = jnp.ones((HEAD,), F32)
    if w.ndim == 2:
        w = w[None]
    tail_start = None
    if tail is not None:
        tail_start = x.shape[0] - (m // tm - 1) * tm
        assert tail_start > 0 and tail_start + tail.shape[0] == tm
        assert tail_start % NORM_ROWS == 0 and tail.shape[0] % NORM_ROWS == 0
    else:
        tail = jnp.zeros((NORM_ROWS, d), x.dtype)
    return pl.pallas_call(
        functools.partial(_norm_matmul_kernel, norm_blocks=norm_blocks, tail_start=tail_start),
        grid=(m // tm, n // tn),
        in_specs=[
            pl.BlockSpec((tm, d), lambda i, j: (i, 0), pipeline_mode=pl.Buffered(1)),
            pl.BlockSpec(tail.shape, lambda i, j: (0, 0)),
            pl.BlockSpec((1, d), lambda i, j: (0, 0)),
            pl.BlockSpec((None, d, tn), lambda i, j: (layer, 0, j)),
            pl.BlockSpec((1, HEAD), lambda i, j: (0, 0)),
        ],
        out_specs=pl.BlockSpec((tm, tn), lambda i, j: (i, j)),
        out_shape=jax.ShapeDtypeStruct((m, n), F32),
        scratch_shapes=[pltpu.VMEM((tm, d), BF16)],
        compiler_params=_params("parallel", "arbitrary"),
        name=name,
    )(x, tail, g.reshape(1, d), w, head_gain.reshape(1, HEAD))


def _kv_q_proj_kernel(x_ref, gkv_ref, gq_ref, wkv_ref, wq_ref, kn_ref, qn_ref, k_ref, kt_ref, v_ref,
                      vt_ref, q_ref, xkv_ref, xq_ref, *, tail_start):
    j = pl.program_id(1)
    last = pl.program_id(0) == pl.num_programs(0) - 1

    @pl.when(j == 0)
    def _():
        _normalise_rows(x_ref, gkv_ref, xkv_ref, 0, x_ref.shape[0])
        _normalise_rows(x_ref, gq_ref, xq_ref, 0, x_ref.shape[0])

    def head_norm(acc, gain_ref, o_ref):
        for h in range(acc.shape[1] // HEAD):
            cols = slice(h * HEAD, (h + 1) * HEAD)
            o_ref[:, cols] = _rms(acc[:, cols], gain_ref[...])

    def copy_tail(head_ref, tail_ref):
        @pl.when(last)
        def _():
            tail_ref[...] = head_ref[tail_start:tail_start + tail_ref.shape[0], :]

        @pl.when(jnp.logical_not(last))
        def _():
            tail_ref[...] = jnp.zeros_like(tail_ref)

    @pl.when(j == 0)
    def _():
        head_norm(_dot(xkv_ref[...], wkv_ref[...].astype(BF16)), kn_ref, k_ref)
        copy_tail(k_ref, kt_ref)

    @pl.when(j == 1)
    def _():
        v_ref[...] = _dot(xkv_ref[...], wkv_ref[...].astype(BF16))
        copy_tail(v_ref, vt_ref)

    @pl.when(j >= 2)
    def _():
        head_norm(_dot(xq_ref[...], wq_ref[...].astype(BF16)), qn_ref, q_ref)


def _kv_q_proj(x, g_kv, w_kv, k_norm, g_q, w_q, layer_q, q_norm, head_rows):
    m, d = x.shape
    tn, n_q = w_kv.shape[-1] // 2, w_q.shape[-1]
    tm = _row_tile(m, FFN_ROW_TILE_CAP)
    assert tn % HEAD == 0 and n_q % tn == 0
    n_tiles = m // tm
    tail_rows = m - head_rows
    tail_start = head_rows - (n_tiles - 1) * tm
    assert 0 < tail_start and tail_start + tail_rows == tm and tail_start % SUBLANES == 0
    rows_only = lambda i, j: (i, 0)
    k, kt, v, vt, q = pl.pallas_call(
        functools.partial(_kv_q_proj_kernel, tail_start=tail_start),
        grid=(n_tiles, 2 + n_q // tn),
        in_specs=[
            pl.BlockSpec((tm, d), lambda i, j: (i, 0), pipeline_mode=pl.Buffered(1)),
            pl.BlockSpec((1, d), lambda i, j: (0, 0)),
            pl.BlockSpec((1, d), lambda i, j: (0, 0)),
            pl.BlockSpec((d, tn), lambda i, j: (0, jnp.minimum(j, 1))),
            pl.BlockSpec((None, d, tn), lambda i, j: (layer_q, 0, jnp.maximum(j - 2, 0))),
            pl.BlockSpec((1, HEAD), lambda i, j: (0, 0)),
            pl.BlockSpec((1, HEAD), lambda i, j: (0, 0)),
        ],
        out_specs=[pl.BlockSpec((tm, tn), rows_only), pl.BlockSpec((tail_rows, tn), rows_only),
                   pl.BlockSpec((tm, tn), rows_only), pl.BlockSpec((tail_rows, tn), rows_only),
                   pl.BlockSpec((tm, tn), lambda i, j: (i, jnp.maximum(j - 2, 0)))],
        out_shape=[jax.ShapeDtypeStruct((head_rows, tn), F32),
                   jax.ShapeDtypeStruct((n_tiles * tail_rows, tn), F32),
                   jax.ShapeDtypeStruct((head_rows, tn), F32),
                   jax.ShapeDtypeStruct((n_tiles * tail_rows, tn), F32),
                   jax.ShapeDtypeStruct((m, n_q), F32)],
        scratch_shapes=[pltpu.VMEM((tm, d), BF16), pltpu.VMEM((tm, d), BF16)],
        compiler_params=_params("parallel", "arbitrary"),
        name="kv_q_proj",
    )(x, g_kv.reshape(1, d), g_q.reshape(1, d), w_kv, w_q, k_norm.reshape(1, HEAD),
      q_norm.reshape(1, HEAD))
    keep = (n_tiles - 1) * tail_rows
    return k, kt[keep:], v, vt[keep:], q


def _swiglu_up_kernel(x_ref, g_ref, wa_ref, wu_ref, o_ref, xn_ref):
    @pl.when(pl.program_id(1) == 0)
    def _():
        _normalise_rows(x_ref, g_ref, xn_ref, 0, x_ref.shape[0])

    xn = xn_ref[...]
    a = _dot(xn, wa_ref[...].astype(BF16))
    u = _dot(xn, wu_ref[...].astype(BF16))
    o_ref[...] = (a * _sigmoid(a) * u).astype(BF16)


def _swiglu_up(x, g, w_gu, layer, name):
    m, d = x.shape
    d_ff = w_gu.shape[-1] // 2
    tm, tn = _row_tile(m, FFN_ROW_TILE_CAP), _col_tile(d_ff, 512)
    nblk = d_ff // tn
    return pl.pallas_call(
        _swiglu_up_kernel,
        grid=(m // tm, nblk),
        in_specs=[
            pl.BlockSpec((tm, d), lambda i, j: (i, 0)),
            pl.BlockSpec((1, d), lambda i, j: (0, 0)),
            pl.BlockSpec((None, d, tn), lambda i, j: (layer, 0, j)),
            pl.BlockSpec((None, d, tn), lambda i, j: (layer, 0, j + nblk)),
        ],
        out_specs=pl.BlockSpec((tm, tn), lambda i, j: (i, j)),
        out_shape=jax.ShapeDtypeStruct((m, d_ff), BF16),
        scratch_shapes=[pltpu.VMEM((tm, d), BF16)],
        compiler_params=_params("parallel", "arbitrary"),
        name=name,
    )(x, g.reshape(1, d), w_gu, w_gu)


def _matmul_residual_kernel(a_ref, w_ref, r_ref, o_ref):
    o_ref[...] = r_ref[...] + _dot(a_ref[...], w_ref[...].astype(BF16))


def _matmul_residual_joined_kernel(a_ref, w_ref, r_ref, rt_ref, o_ref, *, tail_start):
    acc = _dot(a_ref[...], w_ref[...].astype(BF16))
    last = pl.program_id(0) == pl.num_programs(0) - 1

    @pl.when(jnp.logical_not(last))
    def _():
        o_ref[...] = r_ref[...] + acc

    @pl.when(last)
    def _():
        o_ref[:tail_start, :] = r_ref[:tail_start, :] + acc[:tail_start, :]
        o_ref[tail_start:, :] = rt_ref[...] + acc[tail_start:, :]


def _matmul_residual_split_kernel(a_ref, w_ref, r_ref, head_ref, tail_ref, *, tail_start):
    out = r_ref[...] + _dot(a_ref[...], w_ref[...].astype(BF16))
    head_ref[...] = out
    last = pl.program_id(0) == pl.num_programs(0) - 1

    @pl.when(last)
    def _():
        tail_ref[...] = out[tail_start:tail_start + tail_ref.shape[0], :]

    @pl.when(jnp.logical_not(last))
    def _():
        tail_ref[...] = jnp.zeros_like(tail_ref)


def _matmul_residual(a, w, layer, res, tn_cap, name, tm_cap=ROW_TILE_CAP, head_rows=None,
                     res_tail=None):
    m, k = a.shape
    n = w.shape[-1]
    tm, tn = _row_tile(m, tm_cap), _col_tile(n, tn_cap)
    if res_tail is not None:
        tail_start = res.shape[0] - (m // tm - 1) * tm
        assert tail_start > 0 and tail_start + res_tail.shape[0] == tm
        assert tail_start % SUBLANES == 0 and head_rows is None
        return pl.pallas_call(
            functools.partial(_matmul_residual_joined_kernel, tail_start=tail_start),
            grid=(m // tm, n // tn),
            in_specs=[
                pl.BlockSpec((tm, k), lambda i, j: (i, 0)),
                pl.BlockSpec((None, k, tn), lambda i, j: (layer, 0, j)),
                pl.BlockSpec((tm, tn), lambda i, j: (i, j)),
                pl.BlockSpec((res_tail.shape[0], tn), lambda i, j: (0, j)),
            ],
            out_specs=pl.BlockSpec((tm, tn), lambda i, j: (i, j)),
            out_shape=jax.ShapeDtypeStruct((m, n), F32),
            compiler_params=_params("parallel", "arbitrary"),
            name=name,
        )(a, w, res, res_tail)
    in_specs = [
        pl.BlockSpec((tm, k), lambda i, j: (i, 0)),
        pl.BlockSpec((None, k, tn), lambda i, j: (layer, 0, j)),
        pl.BlockSpec((tm, tn), lambda i, j: (i, j)),
    ]
    if head_rows is None:
        return pl.pallas_call(
            _matmul_residual_kernel,
            grid=(m // tm, n // tn),
            in_specs=in_specs,
            out_specs=pl.BlockSpec((tm, tn), lambda i, j: (i, j)),
            out_shape=jax.ShapeDtypeStruct((m, n), F32),
            compiler_params=_params("parallel", "arbitrary"),
            name=name,
        )(a, w, res)
    tail_rows = m - head_rows
    tail_start = head_rows - (m // tm - 1) * tm
    assert 0 < tail_start and tail_start + tail_rows == tm and tail_start % SUBLANES == 0
    head, tail = pl.pallas_call(
        functools.partial(_matmul_residual_split_kernel, tail_start=tail_start),
        grid=(m // tm, n // tn),
        in_specs=in_specs,
        out_specs=[pl.BlockSpec((tm, tn), lambda i, j: (i, j)),
                   pl.BlockSpec((tail_rows, tn), lambda i, j: (i, j))],
        out_shape=[jax.ShapeDtypeStruct((head_rows, n), F32),
                   jax.ShapeDtypeStruct((m // tm * tail_rows, n), F32)],
        compiler_params=_params("parallel", "arbitrary"),
        name=name,
    )(a, w, res)
    return head, tail[(m // tm - 1) * tail_rows:]


def _lower_bound(logits, layer):
    e = jnp.exp(logits - jnp.max(logits, axis=0, keepdims=True))
    return jnp.sum(e[:layer + 1], axis=0, keepdims=True) / jnp.sum(e, axis=0, keepdims=True)


def _gate_inputs(zq, zf, lb):
    f = lb + (1.0 - lb) * _sigmoid(zf)
    return zq * _sigmoid(zq), 1.0 - f, jnp.log(f)


def _gated_output(o, zg, gain):
    return _rms(o, gain) * (zg * _sigmoid(zg))


def _cumsum_rows(g, seg=None):
    n = g.shape[0]
    row = lax.broadcasted_iota(jnp.int32, g.shape, 0)
    pos = row if seg is None else _div_mod(row, seg)[1]
    span = n if seg is None else seg
    s = 1
    while s < span:
        g = g + jnp.where(pos >= s, pltpu.roll(g, s, 0), 0.0)
        s *= 2
    return g


def _gla_prompt_kernel(zq_ref, zf_ref, zi_ref, zg_ref, lbl_ref, gain_ref, o_ref, s_ref, st_ref,
                       *, layer, hp):
    t = pl.program_id(1)
    n_t = pl.num_programs(1) - 1
    c = GLA_CHUNK
    mid = (c - 1) // 2

    @pl.when(t == n_t)
    def _():
        o_ref[...] = jnp.zeros_like(o_ref)

    @pl.when(t < n_t)
    def _():
        _gla_prompt_step(zq_ref, zf_ref, zi_ref, zg_ref, lbl_ref, gain_ref, o_ref, s_ref, st_ref,
                         t, n_t, c, mid, layer, hp)


def _gla_prompt_step(zq_ref, zf_ref, zi_ref, zg_ref, lbl_ref, gain_ref, o_ref, s_ref, st_ref,
                     t, n_t, c, mid, layer, hp):

    @pl.when(t == 0)
    def _():
        st_ref[...] = jnp.zeros_like(st_ref)

    lb_all = _lower_bound(lbl_ref[...], layer)
    gain = gain_ref[...]
    r2 = lax.broadcasted_iota(jnp.int32, (c, c), 0)
    c2 = lax.broadcasted_iota(jnp.int32, (c, c), 1)
    st = [st_ref[j] for j in range(hp)]
    for i in range(o_ref.shape[0] // c):
        rows = slice(i * c, (i + 1) * c)
        for j in range(hp):
            cols = slice(j * HEAD, (j + 1) * HEAD)
            q, k, g = _gate_inputs(zq_ref[rows, cols], zf_ref[rows, cols], lb_all[:, cols])
            v16 = zi_ref[rows, cols].astype(BF16)
            cum = _cumsum_rows(g)
            ref = cum[mid:mid + 1, :]
            last = cum[c - 1:c, :]
            att = _dot_nt((q * jnp.exp(cum - ref)).astype(BF16),
                          (k * jnp.exp(ref - cum)).astype(BF16))
            att = jnp.where(r2 >= c2, att, 0.0)
            o = _dot(att.astype(BF16), v16) + _dot_nt((q * jnp.exp(cum)).astype(BF16),
                                                       st[j].astype(BF16))
            k_out = (k * jnp.exp(last - cum)).astype(BF16)
            st[j] = jnp.exp(last) * st[j] + _dot_tn(v16, k_out)
            o_ref[rows, cols] = _gated_output(o, zg_ref[rows, cols], gain).astype(BF16)
    for j in range(hp):
        st_ref[j] = st[j]

    @pl.when(t == n_t - 1)
    def _():
        for j in range(hp):
            s_ref[j] = st[j].T


def _gla_prompt(z, lb_logits, gain, layer, seq, tl=512, hp=GLA_HEADS_PER_STEP):
    heads = z.shape[1] // (4 * HEAD)
    hp = _unroll(heads, hp)
    tl = min(tl, seq)
    slots = lb_logits.shape[0]
    groups = heads // hp
    n_t = seq // tl
    assert z.shape[0] - seq <= tl
    zspec = lambda off: pl.BlockSpec((tl, hp * HEAD),
                                     lambda h, t: (jnp.minimum(t, n_t - 1), h + off * groups))
    return pl.pallas_call(
        functools.partial(_gla_prompt_kernel, layer=layer, hp=hp),
        grid=(groups, n_t + 1),
        in_specs=[zspec(0), zspec(1), zspec(2), zspec(3),
                  pl.BlockSpec((slots, hp * HEAD), lambda h, t: (0, h)),
                  pl.BlockSpec((1, HEAD), lambda h, t: (0, 0))],
        out_specs=[pl.BlockSpec((tl, hp * HEAD), lambda h, t: (t, h)),
                   pl.BlockSpec((hp, HEAD, HEAD), lambda h, t: (h, 0, 0))],
        out_shape=[jax.ShapeDtypeStruct((z.shape[0], heads * HEAD), BF16),
                   jax.ShapeDtypeStruct((heads, HEAD, HEAD), F32)],
        scratch_shapes=[pltpu.VMEM((hp, HEAD, HEAD), F32)],
        compiler_params=_params("parallel", "arbitrary"),
        name="gla_prompt",
    )(z, z, z, z, lb_logits, gain.reshape(1, HEAD))


def _gla_sample_kernel(zq_ref, zf_ref, zi_ref, zg_ref, lbl_ref, gain_ref, s0_ref, o_in_ref,
                       o_ref, s_ref, last_ref, kout_ref, qin_ref, v_ref, *, layer, steps):
    del o_in_ref
    rows = zq_ref.shape[0]
    mid = (steps - 1) // 2
    lb = _lower_bound(lbl_ref[...], layer)
    q, k, g = _gate_inputs(zq_ref[...], zf_ref[...], lb)
    v16 = zi_ref[...].astype(BF16)
    cum = _cumsum_rows(g, seg=steps)

    row = lax.broadcasted_iota(jnp.int32, cum.shape, 0)
    pos = _div_mod(row, steps)[1]

    def spread(src_pos):
        picked = jnp.where(pos == src_pos, cum, 0.0)
        out = picked
        for d in range(steps):
            if d != src_pos:
                out = out + pltpu.roll(picked, (d - src_pos) % rows, 0)
        return out

    ref = spread(mid)
    last = spread(steps - 1)
    r2 = lax.broadcasted_iota(jnp.int32, (rows, rows), 0)
    c2 = lax.broadcasted_iota(jnp.int32, (rows, rows), 1)
    att = _dot_nt((q * jnp.exp(cum - ref)).astype(BF16), (k * jnp.exp(ref - cum)).astype(BF16))
    att = jnp.where((r2 >= c2) & (_div_mod(r2, steps)[0] == _div_mod(c2, steps)[0]), att, 0.0)
    o_intra = _dot(att.astype(BF16), v16)
    qin_ref[...] = (q * jnp.exp(cum)).astype(BF16)
    v_ref[...] = v16
    last_ref[...] = last
    kout_ref[...] = k * jnp.exp(last - cum)
    n_seq = s0_ref.shape[0]

    def body(b, o_inter):
        lo = b * steps
        mine = (row >= lo) & (row < lo + steps)
        st = s0_ref[b]
        o_inter = o_inter + jnp.where(mine, _dot(qin_ref[...], st.astype(BF16)), 0.0)
        k_out = jnp.where(mine, kout_ref[...], 0.0).astype(BF16)
        decay_row = jnp.exp(last_ref[pl.ds(lo, 1), :])
        decay_col = jnp.broadcast_to(decay_row, (SUBLANES, HEAD)).T[:, :1]
        s_ref[b] = decay_col * st + _dot_tn(k_out, v_ref[...])
        return o_inter

    o_inter = lax.fori_loop(0, n_seq, body, jnp.zeros(cum.shape, F32),
                            unroll=_unroll(n_seq, SAMPLE_SEQ_UNROLL))
    o_ref[...] = _gated_output(o_intra + o_inter, zg_ref[...], gain_ref[...]).astype(BF16)


def _gla_sample(z, o_all, row0, n_seq, steps, s0, layer_s0, lb_logits, gain, layer):
    heads = z.shape[1] // (4 * HEAD)
    rows = n_seq * steps
    slots = lb_logits.shape[0]
    rb = row0 // rows
    zspec = lambda off: pl.BlockSpec((rows, HEAD), lambda h: (rb, h + off * heads))
    s_in = pl.BlockSpec((None, n_seq, None, HEAD, HEAD), lambda h: (layer_s0, 0, h, 0, 0))
    s_out = pl.BlockSpec((n_seq, None, HEAD, HEAD), lambda h: (0, h, 0, 0))
    return pl.pallas_call(
        functools.partial(_gla_sample_kernel, layer=layer, steps=steps),
        grid=(heads,),
        in_specs=[zspec(0), zspec(1), zspec(2), zspec(3),
                  pl.BlockSpec((slots, HEAD), lambda h: (0, h)),
                  pl.BlockSpec((1, HEAD), lambda h: (0, 0)),
                  s_in,
                  pl.BlockSpec(memory_space=pl.ANY)],
        out_specs=[pl.BlockSpec((rows, HEAD), lambda h: (rb, h)), s_out],
        out_shape=[jax.ShapeDtypeStruct(o_all.shape, o_all.dtype),
                   jax.ShapeDtypeStruct(s0.shape[1:], s0.dtype)],
        scratch_shapes=[pltpu.VMEM((rows, HEAD), F32), pltpu.VMEM((rows, HEAD), F32),
                        pltpu.VMEM((rows, HEAD), BF16), pltpu.VMEM((rows, HEAD), BF16)],
        input_output_aliases={7: 0},
        compiler_params=_params("parallel"),
        name="gla_sample",
    )(z, z, z, z, lb_logits, gain.reshape(1, HEAD), s0, o_all)


def _top_blocks_bits(gate_t, blk_f, n_valid):
    gate_t = jnp.where(blk_f < n_valid, gate_t, -jnp.inf)
    bits = jnp.zeros((1, gate_t.shape[1]), jnp.int32)
    for _ in range(MOBA_TOPK):
        mx = jnp.max(gate_t, axis=0, keepdims=True)
        idx = jnp.min(jnp.where(gate_t == mx, blk_f, float(LANES)), axis=0, keepdims=True)
        live = mx > -jnp.inf
        gate_t = jnp.where(blk_f == idx, -jnp.inf, gate_t)
        shift = jnp.minimum(idx, 31.0).astype(jnp.int32)
        bits = bits | jnp.where(live, jnp.left_shift(1, shift), 0)
    return bits


def _moba_prompt_kernel(q_ref, k_ref, v_ref, o_ref, *scratch, n_blocks, group, chunk):
    qi = pl.program_id(1)

    @pl.when(qi == n_blocks)
    def _():
        o_ref[...] = jnp.zeros_like(o_ref)

    @pl.when(qi < n_blocks)
    def _():
        _moba_prompt_block(q_ref, k_ref, v_ref, o_ref, *scratch, qi=qi, n_blocks=n_blocks,
                           group=group, chunk=chunk)


def _moba_prompt_block(q_ref, k_ref, v_ref, o_ref, kmean_ref, qb_ref, sel_ref, m_ref, l_ref, acc_ref,
                       *, qi, n_blocks, group, chunk):
    blk = MOBA_BLOCK
    rows = group * blk
    n_chunks = rows // chunk

    @pl.when(qi == 0)
    def _():
        kmean_ref[...] = jnp.zeros_like(kmean_ref)
        for n in range(n_blocks):
            kmean_ref[n:n + 1, :] = jnp.mean(k_ref[n * blk:(n + 1) * blk, :], axis=0, keepdims=True)

    blk_f = lax.broadcasted_iota(jnp.int32, (kmean_ref.shape[0], blk), 0).astype(F32)
    qi_f = qi.astype(F32)
    for g in range(group):
        qg = q_ref[:, g * HEAD:(g + 1) * HEAD]
        qb_ref[g * blk:(g + 1) * blk, :] = (qg * SCALE_LOG2E).astype(BF16)
        gate_t = _dot_nt(kmean_ref[...], qg, precision=lax.Precision.HIGHEST)
        bits = _top_blocks_bits(gate_t, blk_f, qi_f)
        per_row = jnp.broadcast_to(bits, (SUBLANES, blk)).T
        sel_ref[g * blk:(g + 1) * blk, :] = jnp.broadcast_to(per_row[:, :1], (blk, LANES))

    own = pl.ds(pl.multiple_of(qi * blk, blk), blk)
    kd = k_ref[own, :].astype(BF16)
    vd = v_ref[own, :].astype(BF16)
    own_chunk = min(PROMPT_OWN_CHUNK, blk)
    for c in range(rows // own_chunk):
        r = slice(c * own_chunk, (c + 1) * own_chunk)
        q_pos = lax.broadcasted_iota(jnp.int32, (own_chunk, blk), 0) + (c * own_chunk) % blk
        k_pos = lax.broadcasted_iota(jnp.int32, (own_chunk, blk), 1)
        s = jnp.where(k_pos <= q_pos, _dot_nt(qb_ref[r, :], kd), NEG)
        m = jnp.max(s, axis=1, keepdims=True)
        p = jnp.exp2(s - m)
        m_ref[r, :] = jnp.broadcast_to(m, (own_chunk, LANES))
        l_ref[r, :] = jnp.broadcast_to(jnp.sum(p, axis=1, keepdims=True), (own_chunk, LANES))
        acc_ref[r, :] = _dot(p.astype(BF16), vd)

    def attend(first_pair, n_pairs):
        blocks = []
        for t in range(n_pairs):
            ja = 2 * (first_pair + t)
            jb = jnp.minimum(ja + 1, qi - 1)
            bit_a = jnp.left_shift(jnp.int32(1), ja)
            bit_b = jnp.where(ja + 1 < qi, jnp.left_shift(jnp.int32(1), jb), 0)
            rows_a = pl.ds(pl.multiple_of(ja * blk, blk), blk)
            rows_b = pl.ds(pl.multiple_of(jb * blk, blk), blk)
            blocks.append((bit_a, bit_b,
                           k_ref[rows_a, :].astype(BF16), k_ref[rows_b, :].astype(BF16),
                           v_ref[rows_a, :].astype(BF16), v_ref[rows_b, :].astype(BF16)))
        work = [(t, c) for t in range(n_pairs) for c in range(n_chunks)]

        def scores(item):
            t, c = item
            qc = qb_ref[c * chunk:(c + 1) * chunk, :]
            return _dot_nt(qc, blocks[t][2]), _dot_nt(qc, blocks[t][3])

        pending = [scores(w) for w in work[:SCORE_LOOKAHEAD]]
        for n, (t, c) in enumerate(work):
            bit_a, bit_b, _, _, va, vb = blocks[t]
            r = slice(c * chunk, (c + 1) * chunk)
            sel = sel_ref[r, :]
            on_a = (sel & bit_a) != 0
            on_b = (sel & bit_b) != 0
            sa, sb = pending.pop(0)
            if n + SCORE_LOOKAHEAD < len(work):
                pending.append(scores(work[n + SCORE_LOOKAHEAD]))
            parts = [jnp.where(on_a, sa[:, :LANES], NEG), jnp.where(on_a, sa[:, LANES:], NEG),
                     jnp.where(on_b, sb[:, :LANES], NEG), jnp.where(on_b, sb[:, LANES:], NEG)]
            m_prev = m_ref[r, :]
            top = jnp.maximum(jnp.maximum(parts[0], parts[1]), jnp.maximum(parts[2], parts[3]))
            m_new = jnp.maximum(m_prev, jnp.max(top, axis=1, keepdims=True))
            ps = [jnp.exp2(x - m_new) for x in parts]
            alpha = jnp.exp2(m_prev - m_new)
            l_ref[r, :] = alpha * l_ref[r, :] + jnp.sum((ps[0] + ps[1]) + (ps[2] + ps[3]), axis=1,
                                                        keepdims=True)
            pa = jnp.concatenate(ps[:2], axis=1).astype(BF16)
            pb = jnp.concatenate(ps[2:], axis=1).astype(BF16)
            acc_ref[r, :] = alpha * acc_ref[r, :] + (_dot(pa, va) + _dot(pb, vb))
            m_ref[r, :] = m_new

    n_pairs = (qi + 1) // 2
    trips = n_pairs // PAIRS_PER_TRIP

    def body(i, carry):
        attend(i * PAIRS_PER_TRIP, PAIRS_PER_TRIP)
        return carry

    lax.fori_loop(0, trips, body, 0)
    for left in range(1, PAIRS_PER_TRIP):
        @pl.when(n_pairs - trips * PAIRS_PER_TRIP == left)
        def _():
            attend(trips * PAIRS_PER_TRIP, left)

    for g in range(group):
        r = slice(g * blk, (g + 1) * blk)
        o_ref[:, g * HEAD:(g + 1) * HEAD] = (acc_ref[r, :] / l_ref[r, :]).astype(BF16)


def _moba_prompt(q, k, v, seq, n_kv, chunk=PROMPT_ROW_CHUNK):
    heads = q.shape[1] // HEAD
    group = heads // n_kv
    n_blocks = seq // MOBA_BLOCK
    blk = MOBA_BLOCK
    rows = group * blk
    assert q.shape[0] - seq <= blk
    return pl.pallas_call(
        functools.partial(_moba_prompt_kernel, n_blocks=n_blocks, group=group, chunk=chunk),
        grid=(n_kv, n_blocks + 1),
        in_specs=[pl.BlockSpec((blk, group * HEAD), lambda h, i: (jnp.minimum(i, n_blocks - 1), h)),
                  pl.BlockSpec((seq, HEAD), lambda h, i: (0, h)),
                  pl.BlockSpec((seq, HEAD), lambda h, i: (0, h))],
        out_specs=pl.BlockSpec((blk, group * HEAD), lambda h, i: (i, h)),
        out_shape=jax.ShapeDtypeStruct((q.shape[0], heads * HEAD), BF16),
        scratch_shapes=[pltpu.VMEM((-(-n_blocks // SUBLANES) * SUBLANES, HEAD), F32),
                        pltpu.VMEM((rows, HEAD), BF16),
                        pltpu.VMEM((rows, LANES), jnp.int32),
                        pltpu.VMEM((rows, LANES), F32),
                        pltpu.VMEM((rows, LANES), F32),
                        pltpu.VMEM((rows, HEAD), F32)],
        compiler_params=_params("arbitrary", "arbitrary"),
        name="moba_prompt",
    )(q, k, v)


def _moba_sample_kernel(pt_ref, q_ref, kn_ref, vn_ref, *refs, n_kv, group, steps, pages_per_block,
                        blocks_per_step):
    del pt_ref
    n_pages = pages_per_block * blocks_per_step
    kp = refs[:n_pages]
    vp = refs[n_pages:2 * n_pages]
    o_ref, m_s, l_s, g_s, o_s, bias_s = refs[2 * n_pages:]
    j = pl.program_id(1)
    rows = q_ref.shape[1]
    rq = group * steps
    tok_per_vreg = SUBLANES // n_kv
    n_col = bias_s.shape[1]

    @pl.when((pl.program_id(0) == 0) & (j == 0))
    def _():
        row_head = _div_mod(lax.broadcasted_iota(jnp.int32, (rows, n_col), 0), rq)[0]
        col_head = _div_mod(lax.broadcasted_iota(jnp.int32, (rows, n_col), 1), n_kv)[1]
        bias_s[...] = jnp.where(row_head == col_head, 0.0, NEG)

    qf = q_ref[0]
    q16 = qf.astype(BF16)
    head64 = _div_mod(lax.broadcasted_iota(jnp.int32, (rows, HEAD), 0), rq)[0]
    kbs, vbs, scores = [], [], []
    for t in range(blocks_per_step):
        pages = slice(t * pages_per_block, (t + 1) * pages_per_block)
        kbs.append(jnp.concatenate([r[0] for r in kp[pages]], axis=0))
        vbs.append(jnp.concatenate([r[0] for r in vp[pages]], axis=0))
        scores.append(_dot_nt(q16, kbs[t].astype(BF16)))
    for t in range(blocks_per_step):
        kb = kbs[t]
        s = scores[t] * SCALE + bias_s[...]
        m = jnp.max(s, axis=1, keepdims=True)
        p = jnp.exp(s - m)

        parts = [kb[SUBLANES * i:SUBLANES * (i + 1), :] for i in range(n_col // SUBLANES)]
        while len(parts) > 1:
            parts = [a + b for a, b in zip(parts[::2], parts[1::2])]
        folded = parts[0]
        kmean = jnp.zeros((rows, HEAD), F32)
        for h in range(n_kv):
            total = folded[h:h + 1, :]
            for i in range(1, tok_per_vreg):
                total = total + folded[h + i * n_kv:h + i * n_kv + 1, :]
            kmean = jnp.where(head64 == h, total * (1.0 / MOBA_BLOCK), kmean)
        gate = jnp.sum(qf * kmean, axis=1, keepdims=True)

        n = j * blocks_per_step + t
        m_s[n] = jnp.broadcast_to(m, (rows, LANES))
        l_s[n] = jnp.broadcast_to(jnp.sum(p, axis=1, keepdims=True), (rows, LANES))
        g_s[n] = jnp.broadcast_to(gate, (rows, LANES))
        o_s[n] = _dot(p.astype(BF16), vbs[t].astype(BF16))

    @pl.when(j == pl.num_programs(1) - 1)
    def _():
        gates = g_s[...]
        n_io = lax.broadcasted_iota(jnp.int32, gates.shape, 0)
        sel = jnp.zeros(gates.shape, jnp.bool_)
        for _ in range(min(MOBA_TOPK, g_s.shape[0])):
            mx = jnp.max(gates, axis=0, keepdims=True)
            idx = jnp.min(jnp.where(gates == mx, n_io, g_s.shape[0]), axis=0, keepdims=True)
            pick = n_io == idx
            sel = sel | pick
            gates = jnp.where(pick, -jnp.inf, gates)

        row = lax.broadcasted_iota(jnp.int32, (rows, LANES), 0)
        lane = lax.broadcasted_iota(jnp.int32, (rows, LANES), 1)
        q_head, q_step = _div_mod(row, rq)[0], _div_mod(row, steps)[1]
        k_step, k_head = _div_mod(lane, n_kv)
        ok = (q_head == k_head) & (k_step <= q_step) & (k_step < steps)
        s_own = jnp.where(ok, _dot_nt(q16, kn_ref[0].astype(BF16)) * SCALE, NEG)
        m_own = jnp.max(s_own, axis=1, keepdims=True)
        p_own = jnp.exp(s_own - m_own)
        l_own = jnp.sum(p_own, axis=1, keepdims=True)
        o_own = _dot(p_own.astype(BF16), vn_ref[0].astype(BF16))

        m_all = m_s[...]
        m_top = jnp.maximum(m_own, jnp.max(jnp.where(sel, m_all, -jnp.inf), axis=0))
        w = jnp.where(sel, jnp.exp(jnp.minimum(m_all - m_top[None], 0.0)), 0.0)
        w_own = jnp.exp(m_own - m_top)
        den = w_own * l_own + jnp.sum(w * l_s[...], axis=0)
        num = w_own * o_own + jnp.sum(w * o_s[...], axis=0)
        o_ref[0] = num / den


def _moba_sample(q, k_new, v_new, cache_k, cache_v, page_table, n_kv, steps):
    n_seq, n_pages = page_table.shape
    n_phys, page, _, _ = cache_k.shape
    ppb = MOBA_BLOCK // page
    n_blocks = n_pages // ppb
    bps = _unroll(n_blocks, SAMPLE_BLOCKS_PER_STEP)
    rows = q.shape[1]
    group = rows // (n_kv * steps)
    assert SUBLANES % n_kv == 0 and steps * n_kv <= LANES
    ck = cache_k.reshape(n_phys, page * n_kv, HEAD)
    cv = cache_v.reshape(n_phys, page * n_kv, HEAD)
    pages_per_step = ppb * bps

    def page_spec(p):
        return pl.BlockSpec((1, page * n_kv, HEAD),
                            lambda b, j, pt: (pt[b, j * pages_per_step + p], 0, 0))

    per_seq = lambda shape: pl.BlockSpec(shape, lambda b, j, pt: (b, 0, 0))
    grid_spec = pltpu.PrefetchScalarGridSpec(
        num_scalar_prefetch=1,
        grid=(n_seq, n_blocks // bps),
        in_specs=[per_seq((1, rows, HEAD)), per_seq((1, LANES, HEAD)), per_seq((1, LANES, HEAD))]
                 + [page_spec(p) for p in range(pages_per_step)] * 2,
        out_specs=per_seq((1, rows, HEAD)),
        scratch_shapes=[pltpu.VMEM((n_blocks, rows, LANES), F32) for _ in range(4)]
                       + [pltpu.VMEM((rows, MOBA_BLOCK * n_kv), F32)],
    )
    return pl.pallas_call(
        functools.partial(_moba_sample_kernel, n_kv=n_kv, group=group, steps=steps,
                          pages_per_block=ppb, blocks_per_step=bps),
        grid_spec=grid_spec,
        out_shape=jax.ShapeDtypeStruct((n_seq, rows, HEAD), F32),
        compiler_params=_params("arbitrary", "arbitrary"),
        name="moba_sample",
    )(page_table, q, k_new, v_new, *([ck] * pages_per_step), *([cv] * pages_per_step))


def kernel(x_prompt, x_sample, state_hgrn, cache_k, cache_v, page_table, norm_mix_a, w_in_a, lb_logits,
           onorm_a, w_out_a, norm_kv, w_kv, k_norm, norm_mix_b, w_q_b, q_norm, w_o_b, norm_ffn,
           w_gate_up, w_down):
    batch, seq, d = x_prompt.shape
    n_seq, steps, _ = x_sample.shape
    n_a = w_in_a.shape[0]
    depth = norm_ffn.shape[0]
    heads = d // HEAD
    n_kv = cache_k.shape[2]
    group = heads // n_kv
    n_dec = n_seq * steps
    past = page_table.shape[1] * cache_k.shape[1]
    assert batch == 1 and seq % MOBA_BLOCK == 0 and seq // MOBA_BLOCK <= 32
    assert past % MOBA_BLOCK == 0 and MOBA_BLOCK % cache_k.shape[1] == 0
    assert steps <= LANES and seq % n_dec == 0 and seq % GLA_CHUNK == 0

    x_p, x_s = x_prompt.reshape(seq, d), x_sample.reshape(n_dec, d)
    h = None
    states_p, states_s = [], []
    k_p = v_p = k_d = v_d = None
    for layer in range(depth):
        if layer < n_a:
            if h is None:
                z = _norm_matmul(x_p, norm_mix_a[layer], w_in_a, layer, name="hgrn_in", tail=x_s)
            else:
                z = _norm_matmul(h, norm_mix_a[layer], w_in_a, layer, name="hgrn_in")
            o, s_p = _gla_prompt(z, lb_logits, onorm_a[layer], layer, seq)
            o, s_s = _gla_sample(z, o, seq, n_seq, steps, state_hgrn, layer, lb_logits,
                                 onorm_a[layer], layer)
            if h is None:
                h = _matmul_residual(o, w_out_a, layer, x_p, 512, "hgrn_out", res_tail=x_s)
            else:
                h = _matmul_residual(o, w_out_a, layer, h, 512, "hgrn_out")
            states_p.append(s_p.reshape(1, heads, HEAD, HEAD).astype(state_hgrn.dtype))
            states_s.append(s_s)
        else:
            if h is None:
                h = jnp.concatenate([x_p, x_s], axis=0)
            jb = layer - n_a
            if k_p is None:
                k_p, k_d, v_p, v_d, q = _kv_q_proj(h, norm_kv, w_kv, k_norm, norm_mix_b[jb], w_q_b,
                                                   jb, q_norm[jb], seq)
            else:
                q = _norm_matmul(h, norm_mix_b[jb], w_q_b, jb, head_gain=q_norm[jb],
                                 norm_blocks=heads, name="q_proj")
            att = _moba_prompt(q, k_p, v_p, seq, n_kv)
            q_s = q[seq:].reshape(n_seq, steps, n_kv, group, HEAD).transpose(0, 2, 3, 1, 4)
            pad = ((0, 0), (0, LANES - steps * n_kv), (0, 0))
            k_s = jnp.pad(k_d.reshape(n_seq, steps * n_kv, HEAD), pad)
            v_s = jnp.pad(v_d.reshape(n_seq, steps * n_kv, HEAD), pad)
            att_s = _moba_sample(q_s.reshape(n_seq, n_kv * group * steps, HEAD), k_s, v_s,
                                 cache_k, cache_v, page_table, n_kv, steps)
            att_s = att_s.reshape(n_seq, n_kv, group, steps, HEAD).transpose(0, 3, 1, 2, 4)
            att = lax.dynamic_update_slice(att, att_s.reshape(n_dec, d).astype(BF16), (seq, 0))
            h = _matmul_residual(att, w_o_b, jb, h, 512, "attn_out")
        hf = _swiglu_up(h, norm_ffn[layer], w_gate_up, layer, "ffn_up")
        if layer + 1 < depth:
            h = _matmul_residual(hf, w_down, layer, h, 256, "ffn_down", tm_cap=FFN_ROW_TILE_CAP)
        else:
            y_p, y_s = _matmul_residual(hf, w_down, layer, h, 256, "ffn_down",
                                        tm_cap=FFN_ROW_TILE_CAP, head_rows=seq)

    return (y_p.reshape(batch, seq, d),
            y_s.reshape(n_seq, steps, d),
            jnp.stack(states_p),
            jnp.stack(states_s),
            k_p.reshape(batch, seq, n_kv, HEAD),
            v_p.reshape(batch, seq, n_kv, HEAD),
            k_d.reshape(n_seq, steps, n_kv, HEAD),
            v_d.reshape(n_seq, steps, n_kv, HEAD))
```

```python
import functools

import jax
import jax.numpy as jnp
from jax import lax
from jax.experimental import pallas as pl
from jax.experimental.pallas import tpu as pltpu

F32 = jnp.float32
BF16 = jnp.bfloat16

EPS = 1e-6
HEAD = 128
GLA_CHUNK = 64
MOBA_BLOCK = 256
MOBA_TOPK = 3
SCALE = HEAD ** -0.5
SCALE_LOG2E = SCALE * 1.4426950408889634
NEG = -1e30
LANES = 128
SUBLANES = 8
GLA_HEADS_PER_STEP = 4
NORM_ROWS = 16
NORM_UNROLL = 5
ROW_TILE_CAP = 2080
FFN_ROW_TILE_CAP = 1040
PROMPT_ROW_CHUNK = 256
PROMPT_OWN_CHUNK = 128
SCORE_LOOKAHEAD = 2
PAIRS_PER_TRIP = 4
SAMPLE_BLOCKS_PER_STEP = 8
SAMPLE_SEQ_UNROLL = 4
VMEM_LIMIT = 56 * 1024 * 1024


def _dot(a, b):
    return jnp.dot(a, b, preferred_element_type=F32)


def _dot_nt(a, b, precision=None):
    return lax.dot_general(a, b, (((1,), (1,)), ((), ())), precision=precision,
                           preferred_element_type=F32)


def _dot_tn(a, b):
    return lax.dot_general(a, b, (((0,), (0,)), ((), ())), preferred_element_type=F32)


def _div_mod(x, n):
    if n & (n - 1) == 0:
        return jnp.right_shift(x, n.bit_length() - 1), x & (n - 1)
    q = x // n
    return q, x - q * n


def _sigmoid(x):
    return 1.0 / (1.0 + jnp.exp(-x))


def _rms(x, g):
    return x * lax.rsqrt(jnp.mean(x * x, axis=-1, keepdims=True) + EPS) * g


def _row_tile(m, cap=ROW_TILE_CAP):
    for t in range(min(m, cap), 0, -1):
        if m % t == 0 and t % NORM_ROWS == 0:
            return t
    raise ValueError(f"no row tile for {m} rows")


def _col_tile(n, cap):
    for t in range(min(n, cap), 0, -1):
        if n % t == 0 and t % LANES == 0:
            return t
    raise ValueError(f"no column tile for {n} columns")


def _unroll(trips, cap):
    for u in range(min(trips, cap), 0, -1):
        if trips % u == 0:
            return u
    return 1


def _params(*sem):
    return pltpu.CompilerParams(dimension_semantics=sem, vmem_limit_bytes=VMEM_LIMIT)


def _normalise_rows(x_ref, g_ref, xn_ref, dst_row, n_rows):
    g = g_ref[...]
    trips = n_rows // NORM_ROWS

    def body(i, carry):
        src = pl.ds(pl.multiple_of(i * NORM_ROWS, NORM_ROWS), NORM_ROWS)
        dst = pl.ds(pl.multiple_of(dst_row + i * NORM_ROWS, NORM_ROWS), NORM_ROWS)
        xn_ref[dst, :] = _rms(x_ref[src, :], g).astype(BF16)
        return carry

    lax.fori_loop(0, trips, body, 0, unroll=_unroll(trips, NORM_UNROLL))


def _norm_matmul_kernel(x_ref, tail_ref, g_ref, w_ref, hg_ref, o_ref, xn_ref, *, norm_blocks,
                        tail_start):
    j = pl.program_id(1)

    if tail_start is None:
        @pl.when(j == 0)
        def _():
            _normalise_rows(x_ref, g_ref, xn_ref, 0, x_ref.shape[0])
    else:
        last = pl.program_id(0) == pl.num_programs(0) - 1

        @pl.when((j == 0) & jnp.logical_not(last))
        def _():
            _normalise_rows(x_ref, g_ref, xn_ref, 0, x_ref.shape[0])

        @pl.when((j == 0) & last)
        def _():
            _normalise_rows(x_ref, g_ref, xn_ref, 0, tail_start)
            _normalise_rows(tail_ref, g_ref, xn_ref, tail_start, tail_ref.shape[0])

    acc = _dot(xn_ref[...], w_ref[...].astype(BF16))
    if norm_blocks == 0:
        o_ref[...] = acc
        return

    @pl.when(j < norm_blocks)
    def _():
        for h in range(acc.shape[1] // HEAD):
            cols = slice(h * HEAD, (h + 1) * HEAD)
            o_ref[:, cols] = _rms(acc[:, cols], hg_ref[...])

    @pl.when(j >= norm_blocks)
    def _():
        o_ref[...] = acc


def _norm_matmul(x, g, w, layer, head_gain=None, norm_blocks=0, tn_cap=512, name="norm_matmul",
                 tail=None):
    d = x.shape[1]
    m = x.shape[0] + (0 if tail is None else tail.shape[0])
    n = w.shape[-1]
    tm, tn = _row_tile(m), _col_tile(n, tn_cap)
    if head_gain is None:
        head_gain = jnp.ones((HEAD,), F32)
    if w.ndim == 2:
        w = w[None]
    tail_start = None
    if tail is not None:
        tail_start = x.shape[0] - (m // tm - 1) * tm
        assert tail_start > 0 and tail_start + tail.shape[0] == tm
        assert tail_start % NORM_ROWS == 0 and tail.shape[0] % NORM_ROWS == 0
    else:
        tail = jnp.zeros((NORM_ROWS, d), x.dtype)
    return pl.pallas_call(
        functools.partial(_norm_matmul_kernel, norm_blocks=norm_blocks, tail_start=tail_start),
        grid=(m // tm, n // tn),
        in_specs=[
            pl.BlockSpec((tm, d), lambda i, j: (i, 0), pipeline_mode=pl.Buffered(1)),
            pl.BlockSpec(tail.shape, lambda i, j: (0, 0)),
            pl.BlockSpec((1, d), lambda i, j: (0, 0)),
            pl.BlockSpec((None, d, tn), lambda i, j: (layer, 0, j)),
            pl.BlockSpec((1, HEAD), lambda i, j: (0, 0)),
        ],
        out_specs=pl.BlockSpec((tm, tn), lambda i, j: (i, j)),
        out_shape=jax.ShapeDtypeStruct((m, n), F32),
        scratch_shapes=[pltpu.VMEM((tm, d), BF16)],
        compiler_params=_params("parallel", "arbitrary"),
        name=name,
    )(x, tail, g.reshape(1, d), w, head_gain.reshape(1, HEAD))


def _kv_q_proj_kernel(x_ref, gkv_ref, gq_ref, wkv_ref, wq_ref, kn_ref, qn_ref, k_ref, kt_ref, v_ref,
                      vt_ref, q_ref, xkv_ref, xq_ref, *, tail_start):
    j = pl.program_id(1)
    last = pl.program_id(0) == pl.num_programs(0) - 1

    @pl.when(j == 0)
    def _():
        _normalise_rows(x_ref, gkv_ref, xkv_ref, 0, x_ref.shape[0])
        _normalise_rows(x_ref, gq_ref, xq_ref, 0, x_ref.shape[0])

    def head_norm(acc, gain_ref, o_ref):
        for h in range(acc.shape[1] // HEAD):
            cols = slice(h * HEAD, (h + 1) * HEAD)
            o_ref[:, cols] = _rms(acc[:, cols], gain_ref[...])

    def copy_tail(head_ref, tail_ref):
        @pl.when(last)
        def _():
            tail_ref[...] = head_ref[tail_start:tail_start + tail_ref.shape[0], :]

        @pl.when(jnp.logical_not(last))
        def _():
            tail_ref[...] = jnp.zeros_like(tail_ref)

    @pl.when(j == 0)
    def _():
        head_norm(_dot(xkv_ref[...], wkv_ref[...].astype(BF16)), kn_ref, k_ref)
        copy_tail(k_ref, kt_ref)

    @pl.when(j == 1)
    def _():
        v_ref[...] = _dot(xkv_ref[...], wkv_ref[...].astype(BF16))
        copy_tail(v_ref, vt_ref)

    @pl.when(j >= 2)
    def _():
        head_norm(_dot(xq_ref[...], wq_ref[...].astype(BF16)), qn_ref, q_ref)


def _kv_q_proj(x, g_kv, w_kv, k_norm, g_q, w_q, layer_q, q_norm, head_rows):
    m, d = x.shape
    tn, n_q = w_kv.shape[-1] // 2, w_q.shape[-1]
    tm = _row_tile(m, FFN_ROW_TILE_CAP)
    assert tn % HEAD == 0 and n_q % tn == 0
    n_tiles = m // tm
    tail_rows = m - head_rows
    tail_start = head_rows - (n_tiles - 1) * tm
    assert 0 < tail_start and tail_start + tail_rows == tm and tail_start % SUBLANES == 0
    rows_only = lambda i, j: (i, 0)
    k, kt, v, vt, q = pl.pallas_call(
        functools.partial(_kv_q_proj_kernel, tail_start=tail_start),
        grid=(n_tiles, 2 + n_q // tn),
        in_specs=[
            pl.BlockSpec((tm, d), lambda i, j: (i, 0), pipeline_mode=pl.Buffered(1)),
            pl.BlockSpec((1, d), lambda i, j: (0, 0)),
            pl.BlockSpec((1, d), lambda i, j: (0, 0)),
            pl.BlockSpec((d, tn), lambda i, j: (0, jnp.minimum(j, 1))),
            pl.BlockSpec((None, d, tn), lambda i, j: (layer_q, 0, jnp.maximum(j - 2, 0))),
            pl.BlockSpec((1, HEAD), lambda i, j: (0, 0)),
            pl.BlockSpec((1, HEAD), lambda i, j: (0, 0)),
        ],
        out_specs=[pl.BlockSpec((tm, tn), rows_only), pl.BlockSpec((tail_rows, tn), rows_only),
                   pl.BlockSpec((tm, tn), rows_only), pl.BlockSpec((tail_rows, tn), rows_only),
                   pl.BlockSpec((tm, tn), lambda i, j: (i, jnp.maximum(j - 2, 0)))],
        out_shape=[jax.ShapeDtypeStruct((head_rows, tn), F32),
                   jax.ShapeDtypeStruct((n_tiles * tail_rows, tn), F32),
                   jax.ShapeDtypeStruct((head_rows, tn), F32),
                   jax.ShapeDtypeStruct((n_tiles * tail_rows, tn), F32),
                   jax.ShapeDtypeStruct((m, n_q), F32)],
        scratch_shapes=[pltpu.VMEM((tm, d), BF16), pltpu.VMEM((tm, d), BF16)],
        compiler_params=_params("parallel", "arbitrary"),
        name="kv_q_proj",
    )(x, g_kv.reshape(1, d), g_q.reshape(1, d), w_kv, w_q, k_norm.reshape(1, HEAD),
      q_norm.reshape(1, HEAD))
    keep = (n_tiles - 1) * tail_rows
    return k, kt[keep:], v, vt[keep:], q


def _swiglu_up_kernel(x_ref, g_ref, wa_ref, wu_ref, o_ref, xn_ref):
    @pl.when(pl.program_id(1) == 0)
    def _():
        _normalise_rows(x_ref, g_ref, xn_ref, 0, x_ref.shape[0])

    xn = xn_ref[...]
    a = _dot(xn, wa_ref[...].astype(BF16))
    u = _dot(xn, wu_ref[...].astype(BF16))
    o_ref[...] = (a * _sigmoid(a) * u).astype(BF16)


def _swiglu_up(x, g, w_gu, layer, name):
    m, d = x.shape
    d_ff = w_gu.shape[-1] // 2
    tm, tn = _row_tile(m, FFN_ROW_TILE_CAP), _col_tile(d_ff, 512)
    nblk = d_ff // tn
    return pl.pallas_call(
        _swiglu_up_kernel,
        grid=(m // tm, nblk),
        in_specs=[
            pl.BlockSpec((tm, d), lambda i, j: (i, 0)),
            pl.BlockSpec((1, d), lambda i, j: (0, 0)),
            pl.BlockSpec((None, d, tn), lambda i, j: (layer, 0, j)),
            pl.BlockSpec((None, d, tn), lambda i, j: (layer, 0, j + nblk)),
        ],
        out_specs=pl.BlockSpec((tm, tn), lambda i, j: (i, j)),
        out_shape=jax.ShapeDtypeStruct((m, d_ff), BF16),
        scratch_shapes=[pltpu.VMEM((tm, d), BF16)],
        compiler_params=_params("parallel", "arbitrary"),
        name=name,
    )(x, g.reshape(1, d), w_gu, w_gu)


def _matmul_residual_kernel(a_ref, w_ref, r_ref, o_ref):
    o_ref[...] = r_ref[...] + _dot(a_ref[...], w_ref[...].astype(BF16))


def _matmul_residual_joined_kernel(a_ref, w_ref, r_ref, rt_ref, o_ref, *, tail_start):
    acc = _dot(a_ref[...], w_ref[...].astype(BF16))
    last = pl.program_id(0) == pl.num_programs(0) - 1

    @pl.when(jnp.logical_not(last))
    def _():
        o_ref[...] = r_ref[...] + acc

    @pl.when(last)
    def _():
        o_ref[:tail_start, :] = r_ref[:tail_start, :] + acc[:tail_start, :]
        o_ref[tail_start:, :] = rt_ref[...] + acc[tail_start:, :]


def _matmul_residual_split_kernel(a_ref, w_ref, r_ref, head_ref, tail_ref, *, tail_start):
    out = r_ref[...] + _dot(a_ref[...], w_ref[...].astype(BF16))
    head_ref[...] = out
    last = pl.program_id(0) == pl.num_programs(0) - 1

    @pl.when(last)
    def _():
        tail_ref[...] = out[tail_start:tail_start + tail_ref.shape[0], :]

    @pl.when(jnp.logical_not(last))
    def _():
        tail_ref[...] = jnp.zeros_like(tail_ref)


def _matmul_residual(a, w, layer, res, tn_cap, name, tm_cap=ROW_TILE_CAP, head_rows=None,
                     res_tail=None, a_buffers=2):
    m, k = a.shape
    n = w.shape[-1]
    tm, tn = _row_tile(m, tm_cap), _col_tile(n, tn_cap)
    if res_tail is not None:
        tail_start = res.shape[0] - (m // tm - 1) * tm
        assert tail_start > 0 and tail_start + res_tail.shape[0] == tm
        assert tail_start % SUBLANES == 0 and head_rows is None
        return pl.pallas_call(
            functools.partial(_matmul_residual_joined_kernel, tail_start=tail_start),
            grid=(m // tm, n // tn),
            in_specs=[
                pl.BlockSpec((tm, k), lambda i, j: (i, 0)),
                pl.BlockSpec((None, k, tn), lambda i, j: (layer, 0, j)),
                pl.BlockSpec((tm, tn), lambda i, j: (i, j)),
                pl.BlockSpec((res_tail.shape[0], tn), lambda i, j: (0, j)),
            ],
            out_specs=pl.BlockSpec((tm, tn), lambda i, j: (i, j)),
            out_shape=jax.ShapeDtypeStruct((m, n), F32),
            compiler_params=_params("parallel", "arbitrary"),
            name=name,
        )(a, w, res, res_tail)
    in_specs = [
        pl.BlockSpec((tm, k), lambda i, j: (i, 0), pipeline_mode=pl.Buffered(a_buffers)),
        pl.BlockSpec((None, k, tn), lambda i, j: (layer, 0, j)),
        pl.BlockSpec((tm, tn), lambda i, j: (i, j)),
    ]
    if head_rows is None:
        return pl.pallas_call(
            _matmul_residual_kernel,
            grid=(m // tm, n // tn),
            in_specs=in_specs,
            out_specs=pl.BlockSpec((tm, tn), lambda i, j: (i, j)),
            out_shape=jax.ShapeDtypeStruct((m, n), F32),
            compiler_params=_params("parallel", "arbitrary"),
            name=name,
        )(a, w, res)
    tail_rows = m - head_rows
    tail_start = head_rows - (m // tm - 1) * tm
    assert 0 < tail_start and tail_start + tail_rows == tm and tail_start % SUBLANES == 0
    head, tail = pl.pallas_call(
        functools.partial(_matmul_residual_split_kernel, tail_start=tail_start),
        grid=(m // tm, n // tn),
        in_specs=in_specs,
        out_specs=[pl.BlockSpec((tm, tn), lambda i, j: (i, j)),
                   pl.BlockSpec((tail_rows, tn), lambda i, j: (i, j))],
        out_shape=[jax.ShapeDtypeStruct((head_rows, n), F32),
                   jax.ShapeDtypeStruct((m // tm * tail_rows, n), F32)],
        compiler_params=_params("parallel", "arbitrary"),
        name=name,
    )(a, w, res)
    return head, tail[(m // tm - 1) * tail_rows:]


def _lower_bound(logits, layer):
    e = jnp.exp(logits - jnp.max(logits, axis=0, keepdims=True))
    return jnp.sum(e[:layer + 1], axis=0, keepdims=True) / jnp.sum(e, axis=0, keepdims=True)


def _gate_inputs(zq, zf, lb):
    f = lb + (1.0 - lb) * _sigmoid(zf)
    return zq * _sigmoid(zq), 1.0 - f, jnp.log(f)


def _gated_output(o, zg, gain):
    return _rms(o, gain) * (zg * _sigmoid(zg))


def _cumsum_rows(g, seg=None):
    n = g.shape[0]
    row = lax.broadcasted_iota(jnp.int32, g.shape, 0)
    pos = row if seg is None else _div_mod(row, seg)[1]
    span = n if seg is None else seg
    s = 1
    while s < span:
        g = g + jnp.where(pos >= s, pltpu.roll(g, s, 0), 0.0)
        s *= 2
    return g


def _gla_prompt_kernel(zq_ref, zf_ref, zi_ref, zg_ref, lbl_ref, gain_ref, o_ref, s_ref, st_ref,
                       *, layer, hp):
    t = pl.program_id(1)
    n_t = pl.num_programs(1) - 1
    c = GLA_CHUNK
    mid = (c - 1) // 2

    @pl.when(t == n_t)
    def _():
        o_ref[...] = jnp.zeros_like(o_ref)

    @pl.when(t < n_t)
    def _():
        _gla_prompt_step(zq_ref, zf_ref, zi_ref, zg_ref, lbl_ref, gain_ref, o_ref, s_ref, st_ref,
                         t, n_t, c, mid, layer, hp)


def _gla_prompt_step(zq_ref, zf_ref, zi_ref, zg_ref, lbl_ref, gain_ref, o_ref, s_ref, st_ref,
                     t, n_t, c, mid, layer, hp):

    @pl.when(t == 0)
    def _():
        st_ref[...] = jnp.zeros_like(st_ref)

    lb_all = _lower_bound(lbl_ref[...], layer)
    gain = gain_ref[...]
    r2 = lax.broadcasted_iota(jnp.int32, (c, c), 0)
    c2 = lax.broadcasted_iota(jnp.int32, (c, c), 1)
    st = [st_ref[j] for j in range(hp)]
    for i in range(o_ref.shape[0] // c):
        rows = slice(i * c, (i + 1) * c)
        for j in range(hp):
            cols = slice(j * HEAD, (j + 1) * HEAD)
            q, k, g = _gate_inputs(zq_ref[rows, cols], zf_ref[rows, cols], lb_all[:, cols])
            v16 = zi_ref[rows, cols].astype(BF16)
            cum = _cumsum_rows(g)
            ref = cum[mid:mid + 1, :]
            last = cum[c - 1:c, :]
            att = _dot_nt((q * jnp.exp(cum - ref)).astype(BF16),
                          (k * jnp.exp(ref - cum)).astype(BF16))
            att = jnp.where(r2 >= c2, att, 0.0)
            o = _dot(att.astype(BF16), v16) + _dot_nt((q * jnp.exp(cum)).astype(BF16),
                                                       st[j].astype(BF16))
            k_out = (k * jnp.exp(last - cum)).astype(BF16)
            st[j] = jnp.exp(last) * st[j] + _dot_tn(v16, k_out)
            o_ref[rows, cols] = _gated_output(o, zg_ref[rows, cols], gain).astype(BF16)
    for j in range(hp):
        st_ref[j] = st[j]

    @pl.when(t == n_t - 1)
    def _():
        for j in range(hp):
            s_ref[j] = st[j].T


def _gla_prompt(z, lb_logits, gain, layer, seq, tl=512, hp=GLA_HEADS_PER_STEP):
    heads = z.shape[1] // (4 * HEAD)
    hp = _unroll(heads, hp)
    tl = min(tl, seq)
    slots = lb_logits.shape[0]
    groups = heads // hp
    n_t = seq // tl
    assert z.shape[0] - seq <= tl
    zspec = lambda off: pl.BlockSpec((tl, hp * HEAD),
                                     lambda h, t: (jnp.minimum(t, n_t - 1), h + off * groups))
    return pl.pallas_call(
        functools.partial(_gla_prompt_kernel, layer=layer, hp=hp),
        grid=(groups, n_t + 1),
        in_specs=[zspec(0), zspec(1), zspec(2), zspec(3),
                  pl.BlockSpec((slots, hp * HEAD), lambda h, t: (0, h)),
                  pl.BlockSpec((1, HEAD), lambda h, t: (0, 0))],
        out_specs=[pl.BlockSpec((tl, hp * HEAD), lambda h, t: (t, h)),
                   pl.BlockSpec((hp, HEAD, HEAD), lambda h, t: (h, 0, 0))],
        out_shape=[jax.ShapeDtypeStruct((z.shape[0], heads * HEAD), BF16),
                   jax.ShapeDtypeStruct((heads, HEAD, HEAD), F32)],
        scratch_shapes=[pltpu.VMEM((hp, HEAD, HEAD), F32)],
        compiler_params=_params("parallel", "arbitrary"),
        name="gla_prompt",
    )(z, z, z, z, lb_logits, gain.reshape(1, HEAD))


def _gla_sample_kernel(zq_ref, zf_ref, zi_ref, zg_ref, lbl_ref, gain_ref, s0_ref, o_in_ref,
                       o_ref, s_ref, last_ref, kout_ref, qin_ref, v_ref, *, layer, steps):
    del o_in_ref
    rows = zq_ref.shape[0]
    mid = (steps - 1) // 2
    lb = _lower_bound(lbl_ref[...], layer)
    q, k, g = _gate_inputs(zq_ref[...], zf_ref[...], lb)
    v16 = zi_ref[...].astype(BF16)
    cum = _cumsum_rows(g, seg=steps)

    row = lax.broadcasted_iota(jnp.int32, cum.shape, 0)
    pos = _div_mod(row, steps)[1]

    def spread(src_pos):
        picked = jnp.where(pos == src_pos, cum, 0.0)
        out = picked
        for d in range(steps):
            if d != src_pos:
                out = out + pltpu.roll(picked, (d - src_pos) % rows, 0)
        return out

    ref = spread(mid)
    last = spread(steps - 1)
    r2 = lax.broadcasted_iota(jnp.int32, (rows, rows), 0)
    c2 = lax.broadcasted_iota(jnp.int32, (rows, rows), 1)
    att = _dot_nt((q * jnp.exp(cum - ref)).astype(BF16), (k * jnp.exp(ref - cum)).astype(BF16))
    att = jnp.where((r2 >= c2) & (_div_mod(r2, steps)[0] == _div_mod(c2, steps)[0]), att, 0.0)
    o_intra = _dot(att.astype(BF16), v16)
    qin_ref[...] = (q * jnp.exp(cum)).astype(BF16)
    v_ref[...] = v16
    last_ref[...] = last
    kout_ref[...] = k * jnp.exp(last - cum)
    n_seq = s0_ref.shape[0]

    def body(b, o_inter):
        lo = b * steps
        mine = (row >= lo) & (row < lo + steps)
        st = s0_ref[b]
        o_inter = o_inter + jnp.where(mine, _dot(qin_ref[...], st.astype(BF16)), 0.0)
        k_out = jnp.where(mine, kout_ref[...], 0.0).astype(BF16)
        decay_row = jnp.exp(last_ref[pl.ds(lo, 1), :])
        decay_col = jnp.broadcast_to(decay_row, (SUBLANES, HEAD)).T[:, :1]
        s_ref[b] = decay_col * st + _dot_tn(k_out, v_ref[...])
        return o_inter

    o_inter = lax.fori_loop(0, n_seq, body, jnp.zeros(cum.shape, F32),
                            unroll=_unroll(n_seq, SAMPLE_SEQ_UNROLL))
    o_ref[...] = _gated_output(o_intra + o_inter, zg_ref[...], gain_ref[...]).astype(BF16)


def _gla_sample(z, o_all, row0, n_seq, steps, s0, layer_s0, lb_logits, gain, layer):
    heads = z.shape[1] // (4 * HEAD)
    rows = n_seq * steps
    slots = lb_logits.shape[0]
    rb = row0 // rows
    zspec = lambda off: pl.BlockSpec((rows, HEAD), lambda h: (rb, h + off * heads))
    s_in = pl.BlockSpec((None, n_seq, None, HEAD, HEAD), lambda h: (layer_s0, 0, h, 0, 0))
    s_out = pl.BlockSpec((n_seq, None, HEAD, HEAD), lambda h: (0, h, 0, 0))
    return pl.pallas_call(
        functools.partial(_gla_sample_kernel, layer=layer, steps=steps),
        grid=(heads,),
        in_specs=[zspec(0), zspec(1), zspec(2), zspec(3),
                  pl.BlockSpec((slots, HEAD), lambda h: (0, h)),
                  pl.BlockSpec((1, HEAD), lambda h: (0, 0)),
                  s_in,
                  pl.BlockSpec(memory_space=pl.ANY)],
        out_specs=[pl.BlockSpec((rows, HEAD), lambda h: (rb, h)), s_out],
        out_shape=[jax.ShapeDtypeStruct(o_all.shape, o_all.dtype),
                   jax.ShapeDtypeStruct(s0.shape[1:], s0.dtype)],
        scratch_shapes=[pltpu.VMEM((rows, HEAD), F32), pltpu.VMEM((rows, HEAD), F32),
                        pltpu.VMEM((rows, HEAD), BF16), pltpu.VMEM((rows, HEAD), BF16)],
        input_output_aliases={7: 0},
        compiler_params=_params("parallel"),
        name="gla_sample",
    )(z, z, z, z, lb_logits, gain.reshape(1, HEAD), s0, o_all)


def _top_blocks_bits(gate_t, blk_f, n_valid):
    gate_t = jnp.where(blk_f < n_valid, gate_t, -jnp.inf)
    bits = jnp.zeros((1, gate_t.shape[1]), jnp.int32)
    for _ in range(MOBA_TOPK):
        mx = jnp.max(gate_t, axis=0, keepdims=True)
        idx = jnp.min(jnp.where(gate_t == mx, blk_f, float(LANES)), axis=0, keepdims=True)
        live = mx > -jnp.inf
        gate_t = jnp.where(blk_f == idx, -jnp.inf, gate_t)
        shift = jnp.minimum(idx, 31.0).astype(jnp.int32)
        bits = bits | jnp.where(live, jnp.left_shift(1, shift), 0)
    return bits


def _moba_prompt_kernel(q_ref, k_ref, v_ref, o_ref, *scratch, n_blocks, group, chunk):
    qi = pl.program_id(1)

    @pl.when(qi == n_blocks)
    def _():
        o_ref[...] = jnp.zeros_like(o_ref)

    @pl.when(qi < n_blocks)
    def _():
        _moba_prompt_block(q_ref, k_ref, v_ref, o_ref, *scratch, qi=qi, n_blocks=n_blocks,
                           group=group, chunk=chunk)


def _moba_prompt_block(q_ref, k_ref, v_ref, o_ref, kmean_ref, qb_ref, sel_ref, m_ref, l_ref, acc_ref,
                       *, qi, n_blocks, group, chunk):
    blk = MOBA_BLOCK
    rows = group * blk
    n_chunks = rows // chunk

    @pl.when(qi == 0)
    def _():
        kmean_ref[...] = jnp.zeros_like(kmean_ref)
        for n in range(n_blocks):
            kmean_ref[n:n + 1, :] = jnp.mean(k_ref[n * blk:(n + 1) * blk, :], axis=0, keepdims=True)

    blk_f = lax.broadcasted_iota(jnp.int32, (kmean_ref.shape[0], blk), 0).astype(F32)
    qi_f = qi.astype(F32)
    for g in range(group):
        qg = q_ref[:, g * HEAD:(g + 1) * HEAD]
        qb_ref[g * blk:(g + 1) * blk, :] = (qg * SCALE_LOG2E).astype(BF16)
        gate_t = _dot_nt(kmean_ref[...], qg, precision=lax.Precision.HIGHEST)
        bits = _top_blocks_bits(gate_t, blk_f, qi_f)
        per_row = jnp.broadcast_to(bits, (SUBLANES, blk)).T
        sel_ref[g * blk:(g + 1) * blk, :] = jnp.broadcast_to(per_row[:, :1], (blk, LANES))

    own = pl.ds(pl.multiple_of(qi * blk, blk), blk)
    kd = k_ref[own, :].astype(BF16)
    vd = v_ref[own, :].astype(BF16)
    own_chunk = min(PROMPT_OWN_CHUNK, blk)
    for c in range(rows // own_chunk):
        r = slice(c * own_chunk, (c + 1) * own_chunk)
        q_pos = lax.broadcasted_iota(jnp.int32, (own_chunk, blk), 0) + (c * own_chunk) % blk
        k_pos = lax.broadcasted_iota(jnp.int32, (own_chunk, blk), 1)
        s = jnp.where(k_pos <= q_pos, _dot_nt(qb_ref[r, :], kd), NEG)
        m = jnp.max(s, axis=1, keepdims=True)
        p = jnp.exp2(s - m)
        m_ref[r, :] = jnp.broadcast_to(m, (own_chunk, LANES))
        l_ref[r, :] = jnp.broadcast_to(jnp.sum(p, axis=1, keepdims=True), (own_chunk, LANES))
        acc_ref[r, :] = _dot(p.astype(BF16), vd)

    def attend(first_pair, n_pairs):
        blocks = []
        for t in range(n_pairs):
            ja = 2 * (first_pair + t)
            jb = jnp.minimum(ja + 1, qi - 1)
            bit_a = jnp.left_shift(jnp.int32(1), ja)
            bit_b = jnp.where(ja + 1 < qi, jnp.left_shift(jnp.int32(1), jb), 0)
            rows_a = pl.ds(pl.multiple_of(ja * blk, blk), blk)
            rows_b = pl.ds(pl.multiple_of(jb * blk, blk), blk)
            blocks.append((bit_a, bit_b,
                           k_ref[rows_a, :].astype(BF16), k_ref[rows_b, :].astype(BF16),
                           v_ref[rows_a, :].astype(BF16), v_ref[rows_b, :].astype(BF16)))
        work = [(t, c) for t in range(n_pairs) for c in range(n_chunks)]

        def scores(item):
            t, c = item
            qc = qb_ref[c * chunk:(c + 1) * chunk, :]
            return _dot_nt(qc, blocks[t][2]), _dot_nt(qc, blocks[t][3])

        pending = [scores(w) for w in work[:SCORE_LOOKAHEAD]]
        for n, (t, c) in enumerate(work):
            bit_a, bit_b, _, _, va, vb = blocks[t]
            r = slice(c * chunk, (c + 1) * chunk)
            sel = sel_ref[r, :]
            on_a = (sel & bit_a) != 0
            on_b = (sel & bit_b) != 0
            sa, sb = pending.pop(0)
            if n + SCORE_LOOKAHEAD < len(work):
                pending.append(scores(work[n + SCORE_LOOKAHEAD]))
            parts = [jnp.where(on_a, sa[:, :LANES], NEG), jnp.where(on_a, sa[:, LANES:], NEG),
                     jnp.where(on_b, sb[:, :LANES], NEG), jnp.where(on_b, sb[:, LANES:], NEG)]
            m_prev = m_ref[r, :]
            top = jnp.maximum(jnp.maximum(parts[0], parts[1]), jnp.maximum(parts[2], parts[3]))
            m_new = jnp.maximum(m_prev, jnp.max(top, axis=1, keepdims=True))
            ps = [jnp.exp2(x - m_new) for x in parts]
            alpha = jnp.exp2(m_prev - m_new)
            l_ref[r, :] = alpha * l_ref[r, :] + jnp.sum((ps[0] + ps[1]) + (ps[2] + ps[3]), axis=1,
                                                        keepdims=True)
            pa = jnp.concatenate(ps[:2], axis=1).astype(BF16)
            pb = jnp.concatenate(ps[2:], axis=1).astype(BF16)
            acc_ref[r, :] = alpha * acc_ref[r, :] + (_dot(pa, va) + _dot(pb, vb))
            m_ref[r, :] = m_new

    n_pairs = (qi + 1) // 2
    trips = n_pairs // PAIRS_PER_TRIP

    def body(i, carry):
        attend(i * PAIRS_PER_TRIP, PAIRS_PER_TRIP)
        return carry

    lax.fori_loop(0, trips, body, 0)
    for left in range(1, PAIRS_PER_TRIP):
        @pl.when(n_pairs - trips * PAIRS_PER_TRIP == left)
        def _():
            attend(trips * PAIRS_PER_TRIP, left)

    for g in range(group):
        r = slice(g * blk, (g + 1) * blk)
        o_ref[:, g * HEAD:(g + 1) * HEAD] = (acc_ref[r, :] / l_ref[r, :]).astype(BF16)


def _moba_prompt(q, k, v, seq, n_kv, chunk=PROMPT_ROW_CHUNK):
    heads = q.shape[1] // HEAD
    group = heads // n_kv
    n_blocks = seq // MOBA_BLOCK
    blk = MOBA_BLOCK
    rows = group * blk
    assert q.shape[0] - seq <= blk
    return pl.pallas_call(
        functools.partial(_moba_prompt_kernel, n_blocks=n_blocks, group=group, chunk=chunk),
        grid=(n_kv, n_blocks + 1),
        in_specs=[pl.BlockSpec((blk, group * HEAD), lambda h, i: (jnp.minimum(i, n_blocks - 1), h)),
                  pl.BlockSpec((seq, HEAD), lambda h, i: (0, h)),
                  pl.BlockSpec((seq, HEAD), lambda h, i: (0, h))],
        out_specs=pl.BlockSpec((blk, group * HEAD), lambda h, i: (i, h)),
        out_shape=jax.ShapeDtypeStruct((q.shape[0], heads * HEAD), BF16),
        scratch_shapes=[pltpu.VMEM((-(-n_blocks // SUBLANES) * SUBLANES, HEAD), F32),
                        pltpu.VMEM((rows, HEAD), BF16),
                        pltpu.VMEM((rows, LANES), jnp.int32),
                        pltpu.VMEM((rows, LANES), F32),
                        pltpu.VMEM((rows, LANES), F32),
                        pltpu.VMEM((rows, HEAD), F32)],
        compiler_params=_params("arbitrary", "arbitrary"),
        name="moba_prompt",
    )(q, k, v)


def _moba_sample_kernel(pt_ref, q_ref, kn_ref, vn_ref, *refs, n_kv, group, steps, pages_per_block,
                        blocks_per_step):
    del pt_ref
    n_pages = pages_per_block * blocks_per_step
    kp = refs[:n_pages]
    vp = refs[n_pages:2 * n_pages]
    o_ref, m_s, l_s, g_s, o_s, bias_s = refs[2 * n_pages:]
    j = pl.program_id(1)
    rows = q_ref.shape[1]
    rq = group * steps
    tok_per_vreg = SUBLANES // n_kv
    n_col = bias_s.shape[1]

    @pl.when((pl.program_id(0) == 0) & (j == 0))
    def _():
        row_head = _div_mod(lax.broadcasted_iota(jnp.int32, (rows, n_col), 0), rq)[0]
        col_head = _div_mod(lax.broadcasted_iota(jnp.int32, (rows, n_col), 1), n_kv)[1]
        bias_s[...] = jnp.where(row_head == col_head, 0.0, NEG)

    qf = q_ref[0]
    q16 = qf.astype(BF16)
    head64 = _div_mod(lax.broadcasted_iota(jnp.int32, (rows, HEAD), 0), rq)[0]
    kbs, vbs, scores = [], [], []
    for t in range(blocks_per_step):
        pages = slice(t * pages_per_block, (t + 1) * pages_per_block)
        kbs.append(jnp.concatenate([r[0] for r in kp[pages]], axis=0))
        vbs.append(jnp.concatenate([r[0] for r in vp[pages]], axis=0))
        scores.append(_dot_nt(q16, kbs[t].astype(BF16)))
    for t in range(blocks_per_step):
        kb = kbs[t]
        s = scores[t] * SCALE + bias_s[...]
        m = jnp.max(s, axis=1, keepdims=True)
        p = jnp.exp(s - m)

        parts = [kb[SUBLANES * i:SUBLANES * (i + 1), :] for i in range(n_col // SUBLANES)]
        while len(parts) > 1:
            parts = [a + b for a, b in zip(parts[::2], parts[1::2])]
        folded = parts[0]
        kmean = jnp.zeros((rows, HEAD), F32)
        for h in range(n_kv):
            total = folded[h:h + 1, :]
            for i in range(1, tok_per_vreg):
                total = total + folded[h + i * n_kv:h + i * n_kv + 1, :]
            kmean = jnp.where(head64 == h, total * (1.0 / MOBA_BLOCK), kmean)
        gate = jnp.sum(qf * kmean, axis=1, keepdims=True)

        n = j * blocks_per_step + t
        m_s[n] = jnp.broadcast_to(m, (rows, LANES))
        l_s[n] = jnp.broadcast_to(jnp.sum(p, axis=1, keepdims=True), (rows, LANES))
        g_s[n] = jnp.broadcast_to(gate, (rows, LANES))
        o_s[n] = _dot(p.astype(BF16), vbs[t].astype(BF16))

    @pl.when(j == pl.num_programs(1) - 1)
    def _():
        gates = g_s[...]
        n_io = lax.broadcasted_iota(jnp.int32, gates.shape, 0)
        sel = jnp.zeros(gates.shape, jnp.bool_)
        for _ in range(min(MOBA_TOPK, g_s.shape[0])):
            mx = jnp.max(gates, axis=0, keepdims=True)
            idx = jnp.min(jnp.where(gates == mx, n_io, g_s.shape[0]), axis=0, keepdims=True)
            pick = n_io == idx
            sel = sel | pick
            gates = jnp.where(pick, -jnp.inf, gates)

        row = lax.broadcasted_iota(jnp.int32, (rows, LANES), 0)
        lane = lax.broadcasted_iota(jnp.int32, (rows, LANES), 1)
        q_head, q_step = _div_mod(row, rq)[0], _div_mod(row, steps)[1]
        k_step, k_head = _div_mod(lane, n_kv)
        ok = (q_head == k_head) & (k_step <= q_step) & (k_step < steps)
        s_own = jnp.where(ok, _dot_nt(q16, kn_ref[0].astype(BF16)) * SCALE, NEG)
        m_own = jnp.max(s_own, axis=1, keepdims=True)
        p_own = jnp.exp(s_own - m_own)
        l_own = jnp.sum(p_own, axis=1, keepdims=True)
        o_own = _dot(p_own.astype(BF16), vn_ref[0].astype(BF16))

        m_all = m_s[...]
        m_top = jnp.maximum(m_own, jnp.max(jnp.where(sel, m_all, -jnp.inf), axis=0))
        w = jnp.where(sel, jnp.exp(jnp.minimum(m_all - m_top[None], 0.0)), 0.0)
        w_own = jnp.exp(m_own - m_top)
        den = w_own * l_own + jnp.sum(w * l_s[...], axis=0)
        num = w_own * o_own + jnp.sum(w * o_s[...], axis=0)
        o_ref[0] = num / den


def _moba_sample(q, k_new, v_new, cache_k, cache_v, page_table, n_kv, steps):
    n_seq, n_pages = page_table.shape
    n_phys, page, _, _ = cache_k.shape
    ppb = MOBA_BLOCK // page
    n_blocks = n_pages // ppb
    bps = _unroll(n_blocks, SAMPLE_BLOCKS_PER_STEP)
    rows = q.shape[1]
    group = rows // (n_kv * steps)
    assert SUBLANES % n_kv == 0 and steps * n_kv <= LANES
    ck = cache_k.reshape(n_phys, page * n_kv, HEAD)
    cv = cache_v.reshape(n_phys, page * n_kv, HEAD)
    pages_per_step = ppb * bps

    def page_spec(p):
        return pl.BlockSpec((1, page * n_kv, HEAD),
                            lambda b, j, pt: (pt[b, j * pages_per_step + p], 0, 0))

    per_seq = lambda shape: pl.BlockSpec(shape, lambda b, j, pt: (b, 0, 0))
    grid_spec = pltpu.PrefetchScalarGridSpec(
        num_scalar_prefetch=1,
        grid=(n_seq, n_blocks // bps),
        in_specs=[per_seq((1, rows, HEAD)), per_seq((1, LANES, HEAD)), per_seq((1, LANES, HEAD))]
                 + [page_spec(p) for p in range(pages_per_step)] * 2,
        out_specs=per_seq((1, rows, HEAD)),
        scratch_shapes=[pltpu.VMEM((n_blocks, rows, LANES), F32) for _ in range(4)]
                       + [pltpu.VMEM((rows, MOBA_BLOCK * n_kv), F32)],
    )
    return pl.pallas_call(
        functools.partial(_moba_sample_kernel, n_kv=n_kv, group=group, steps=steps,
                          pages_per_block=ppb, blocks_per_step=bps),
        grid_spec=grid_spec,
        out_shape=jax.ShapeDtypeStruct((n_seq, rows, HEAD), F32),
        compiler_params=_params("arbitrary", "arbitrary"),
        name="moba_sample",
    )(page_table, q, k_new, v_new, *([ck] * pages_per_step), *([cv] * pages_per_step))


def kernel(x_prompt, x_sample, state_hgrn, cache_k, cache_v, page_table, norm_mix_a, w_in_a, lb_logits,
           onorm_a, w_out_a, norm_kv, w_kv, k_norm, norm_mix_b, w_q_b, q_norm, w_o_b, norm_ffn,
           w_gate_up, w_down):
    batch, seq, d = x_prompt.shape
    n_seq, steps, _ = x_sample.shape
    n_a = w_in_a.shape[0]
    depth = norm_ffn.shape[0]
    heads = d // HEAD
    n_kv = cache_k.shape[2]
    group = heads // n_kv
    n_dec = n_seq * steps
    past = page_table.shape[1] * cache_k.shape[1]
    assert batch == 1 and seq % MOBA_BLOCK == 0 and seq // MOBA_BLOCK <= 32
    assert past % MOBA_BLOCK == 0 and MOBA_BLOCK % cache_k.shape[1] == 0
    assert steps <= LANES and seq % n_dec == 0 and seq % GLA_CHUNK == 0

    x_p, x_s = x_prompt.reshape(seq, d), x_sample.reshape(n_dec, d)
    h = None
    states_p, states_s = [], []
    k_p = v_p = k_d = v_d = None
    for layer in range(depth):
        if layer < n_a:
            if h is None:
                z = _norm_matmul(x_p, norm_mix_a[layer], w_in_a, layer, name="hgrn_in", tail=x_s)
            else:
                z = _norm_matmul(h, norm_mix_a[layer], w_in_a, layer, name="hgrn_in")
            o, s_p = _gla_prompt(z, lb_logits, onorm_a[layer], layer, seq)
            o, s_s = _gla_sample(z, o, seq, n_seq, steps, state_hgrn, layer, lb_logits,
                                 onorm_a[layer], layer)
            if h is None:
                h = _matmul_residual(o, w_out_a, layer, x_p, 512, "hgrn_out", res_tail=x_s)
            else:
                h = _matmul_residual(o, w_out_a, layer, h, 512, "hgrn_out")
            states_p.append(s_p.reshape(1, heads, HEAD, HEAD).astype(state_hgrn.dtype))
            states_s.append(s_s)
        else:
            if h is None:
                h = jnp.concatenate([x_p, x_s], axis=0)
            jb = layer - n_a
            if k_p is None:
                k_p, k_d, v_p, v_d, q = _kv_q_proj(h, norm_kv, w_kv, k_norm, norm_mix_b[jb], w_q_b,
                                                   jb, q_norm[jb], seq)
            else:
                q = _norm_matmul(h, norm_mix_b[jb], w_q_b, jb, head_gain=q_norm[jb],
                                 norm_blocks=heads, name="q_proj")
            att = _moba_prompt(q, k_p, v_p, seq, n_kv)
            q_s = q[seq:].reshape(n_seq, steps, n_kv, group, HEAD).transpose(0, 2, 3, 1, 4)
            pad = ((0, 0), (0, LANES - steps * n_kv), (0, 0))
            k_s = jnp.pad(k_d.reshape(n_seq, steps * n_kv, HEAD), pad)
            v_s = jnp.pad(v_d.reshape(n_seq, steps * n_kv, HEAD), pad)
            att_s = _moba_sample(q_s.reshape(n_seq, n_kv * group * steps, HEAD), k_s, v_s,
                                 cache_k, cache_v, page_table, n_kv, steps)
            att_s = att_s.reshape(n_seq, n_kv, group, steps, HEAD).transpose(0, 3, 1, 2, 4)
            att = lax.dynamic_update_slice(att, att_s.reshape(n_dec, d).astype(BF16), (seq, 0))
            h = _matmul_residual(att, w_o_b, jb, h, 512, "attn_out")
        hf = _swiglu_up(h, norm_ffn[layer], w_gate_up, layer, "ffn_up")
        if layer + 1 < depth:
            h = _matmul_residual(hf, w_down, layer, h, 512, "ffn_down", tm_cap=FFN_ROW_TILE_CAP,
                                 a_buffers=1)
        else:
            y_p, y_s = _matmul_residual(hf, w_down, layer, h, 512, "ffn_down",
                                        tm_cap=FFN_ROW_TILE_CAP, head_rows=seq, a_buffers=1)

    return (y_p.reshape(batch, seq, d),
            y_s.reshape(n_seq, steps, d),
            jnp.stack(states_p),
            jnp.stack(states_s),
            k_p.reshape(batch, seq, n_kv, HEAD),
            v_p.reshape(batch, seq, n_kv, HEAD),
            k_d.reshape(n_seq, steps, n_kv, HEAD),
            v_d.reshape(n_seq, steps, n_kv, HEAD))
```

```python
import functools

import jax
import jax.numpy as jnp
from jax import lax
from jax.experimental import pallas as pl
from jax.experimental.pallas import tpu as pltpu

F32 = jnp.float32
BF16 = jnp.bfloat16

EPS = 1e-6
HEAD = 128
GLA_CHUNK = 64
MOBA_BLOCK = 256
MOBA_TOPK = 3
SCALE = HEAD ** -0.5
SCALE_LOG2E = SCALE * 1.4426950408889634
NEG = -1e30
LANES = 128
SUBLANES = 8
GLA_HEADS_PER_STEP = 4
NORM_ROWS = 16
NORM_UNROLL = 5
ROW_TILE_CAP = 2080
FFN_ROW_TILE_CAP = 1040
PROMPT_ROW_CHUNK = 256
PROMPT_OWN_CHUNK = 128
SCORE_LOOKAHEAD = 2
PAIRS_PER_TRIP = 4
SAMPLE_BLOCKS_PER_STEP = 8
SAMPLE_SEQ_UNROLL = 4
VMEM_LIMIT = 56 * 1024 * 1024


def _dot(a, b):
    return jnp.dot(a, b, preferred_element_type=F32)


def _dot_nt(a, b, precision=None):
    return lax.dot_general(a, b, (((1,), (1,)), ((), ())), precision=precision,
                           preferred_element_type=F32)


def _dot_tn(a, b):
    return lax.dot_general(a, b, (((0,), (0,)), ((), ())), preferred_element_type=F32)


def _div_mod(x, n):
    if n & (n - 1) == 0:
        return jnp.right_shift(x, n.bit_length() - 1), x & (n - 1)
    q = x // n
    return q, x - q * n


def _sigmoid(x):
    return 1.0 / (1.0 + jnp.exp(-x))


def _rms(x, g):
    return x * lax.rsqrt(jnp.mean(x * x, axis=-1, keepdims=True) + EPS) * g


def _row_tile(m, cap=ROW_TILE_CAP):
    for t in range(min(m, cap), 0, -1):
        if m % t == 0 and t % NORM_ROWS == 0:
            return t
    raise ValueError(f"no row tile for {m} rows")


def _col_tile(n, cap):
    for t in range(min(n, cap), 0, -1):
        if n % t == 0 and t % LANES == 0:
            return t
    raise ValueError(f"no column tile for {n} columns")


def _unroll(trips, cap):
    for u in range(min(trips, cap), 0, -1):
        if trips % u == 0:
            return u
    return 1


def _params(*sem):
    return pltpu.CompilerParams(dimension_semantics=sem, vmem_limit_bytes=VMEM_LIMIT)


def _normalise_rows(x_ref, g_ref, xn_ref, dst_row, n_rows):
    g = g_ref[...]
    trips = n_rows // NORM_ROWS

    def body(i, carry):
        src = pl.ds(pl.multiple_of(i * NORM_ROWS, NORM_ROWS), NORM_ROWS)
        dst = pl.ds(pl.multiple_of(dst_row + i * NORM_ROWS, NORM_ROWS), NORM_ROWS)
        xn_ref[dst, :] = _rms(x_ref[src, :], g).astype(BF16)
        return carry

    lax.fori_loop(0, trips, body, 0, unroll=_unroll(trips, NORM_UNROLL))


def _norm_matmul_kernel(x_ref, tail_ref, g_ref, w_ref, hg_ref, o_ref, xn_ref, *, norm_blocks,
                        tail_start):
    j = pl.program_id(1)

    if tail_start is None:
        @pl.when(j == 0)
        def _():
            _normalise_rows(x_ref, g_ref, xn_ref, 0, x_ref.shape[0])
    else:
        last = pl.program_id(0) == pl.num_programs(0) - 1

        @pl.when((j == 0) & jnp.logical_not(last))
        def _():
            _normalise_rows(x_ref, g_ref, xn_ref, 0, x_ref.shape[0])

        @pl.when((j == 0) & last)
        def _():
            _normalise_rows(x_ref, g_ref, xn_ref, 0, tail_start)
            _normalise_rows(tail_ref, g_ref, xn_ref, tail_start, tail_ref.shape[0])

    acc = _dot(xn_ref[...], w_ref[...].astype(BF16))
    if norm_blocks == 0:
        o_ref[...] = acc
        return

    @pl.when(j < norm_blocks)
    def _():
        for h in range(acc.shape[1] // HEAD):
            cols = slice(h * HEAD, (h + 1) * HEAD)
            o_ref[:, cols] = _rms(acc[:, cols], hg_ref[...])

    @pl.when(j >= norm_blocks)
    def _():
        o_ref[...] = acc


def _norm_matmul(x, g, w, layer, head_gain=None, norm_blocks=0, tn_cap=512, name="norm_matmul",
                 tail=None):
    d = x.shape[1]
    m = x.shape[0] + (0 if tail is None else tail.shape[0])
    n = w.shape[-1]
    tm, tn = _row_tile(m), _col_tile(n, tn_cap)
    if head_gain is None:
        head_gain = jnp.ones((HEAD,), F32)
    if w.ndim == 2:
        w = w[None]
    tail_start = None
    if tail is not None:
        tail_start = x.shape[0] - (m // tm - 1) * tm
        assert tail_start > 0 and tail_start + tail.shape[0] == tm
        assert tail_start % NORM_ROWS == 0 and tail.shape[0] % NORM_ROWS == 0
    else:
        tail = jnp.zeros((NORM_ROWS, d), x.dtype)
    return pl.pallas_call(
        functools.partial(_norm_matmul_kernel, norm_blocks=norm_blocks, tail_start=tail_start),
        grid=(m // tm, n // tn),
        in_specs=[
            pl.BlockSpec((tm, d), lambda i, j: (i, 0), pipeline_mode=pl.Buffered(1)),
            pl.BlockSpec(tail.shape, lambda i, j: (0, 0)),
            pl.BlockSpec((1, d), lambda i, j: (0, 0)),
            pl.BlockSpec((None, d, tn), lambda i, j: (layer, 0, j)),
            pl.BlockSpec((1, HEAD), lambda i, j: (0, 0)),
        ],
        out_specs=pl.BlockSpec((tm, tn), lambda i, j: (i, j)),
        out_shape=jax.ShapeDtypeStruct((m, n), F32),
        scratch_shapes=[pltpu.VMEM((tm, d), BF16)],
        compiler_params=_params("parallel", "arbitrary"),
        name=name,
    )(x, tail, g.reshape(1, d), w, head_gain.reshape(1, HEAD))


def _kv_q_proj_kernel(x_ref, gkv_ref, gq_ref, wkv_ref, wq_ref, kn_ref, qn_ref, k_ref, kt_ref, v_ref,
                      vt_ref, q_ref, xkv_ref, xq_ref, *, tail_start):
    j = pl.program_id(1)
    last = pl.program_id(0) == pl.num_programs(0) - 1

    @pl.when(j == 0)
    def _():
        _normalise_rows(x_ref, gkv_ref, xkv_ref, 0, x_ref.shape[0])
        _normalise_rows(x_ref, gq_ref, xq_ref, 0, x_ref.shape[0])

    def head_norm(acc, gain_ref, o_ref):
        for h in range(acc.shape[1] // HEAD):
            cols = slice(h * HEAD, (h + 1) * HEAD)
            o_ref[:, cols] = _rms(acc[:, cols], gain_ref[...])

    def copy_tail(head_ref, tail_ref):
        @pl.when(last)
        def _():
            tail_ref[...] = head_ref[tail_start:tail_start + tail_ref.shape[0], :]

        @pl.when(jnp.logical_not(last))
        def _():
            tail_ref[...] = jnp.zeros_like(tail_ref)

    @pl.when(j == 0)
    def _():
        head_norm(_dot(xkv_ref[...], wkv_ref[...].astype(BF16)), kn_ref, k_ref)
        copy_tail(k_ref, kt_ref)

    @pl.when(j == 1)
    def _():
        v_ref[...] = _dot(xkv_ref[...], wkv_ref[...].astype(BF16))
        copy_tail(v_ref, vt_ref)

    @pl.when(j >= 2)
    def _():
        head_norm(_dot(xq_ref[...], wq_ref[...].astype(BF16)), qn_ref, q_ref)


def _kv_q_proj(x, g_kv, w_kv, k_norm, g_q, w_q, layer_q, q_norm, head_rows):
    m, d = x.shape
    tn, n_q = w_kv.shape[-1] // 2, w_q.shape[-1]
    tm = _row_tile(m, FFN_ROW_TILE_CAP)
    assert tn % HEAD == 0 and n_q % tn == 0
    n_tiles = m // tm
    tail_rows = m - head_rows
    tail_start = head_rows - (n_tiles - 1) * tm
    assert 0 < tail_start and tail_start + tail_rows == tm and tail_start % SUBLANES == 0
    rows_only = lambda i, j: (i, 0)
    k, kt, v, vt, q = pl.pallas_call(
        functools.partial(_kv_q_proj_kernel, tail_start=tail_start),
        grid=(n_tiles, 2 + n_q // tn),
        in_specs=[
            pl.BlockSpec((tm, d), lambda i, j: (i, 0), pipeline_mode=pl.Buffered(1)),
            pl.BlockSpec((1, d), lambda i, j: (0, 0)),
            pl.BlockSpec((1, d), lambda i, j: (0, 0)),
            pl.BlockSpec((d, tn), lambda i, j: (0, jnp.minimum(j, 1))),
            pl.BlockSpec((None, d, tn), lambda i, j: (layer_q, 0, jnp.maximum(j - 2, 0))),
            pl.BlockSpec((1, HEAD), lambda i, j: (0, 0)),
            pl.BlockSpec((1, HEAD), lambda i, j: (0, 0)),
        ],
        out_specs=[pl.BlockSpec((tm, tn), rows_only), pl.BlockSpec((tail_rows, tn), rows_only),
                   pl.BlockSpec((tm, tn), rows_only), pl.BlockSpec((tail_rows, tn), rows_only),
                   pl.BlockSpec((tm, tn), lambda i, j: (i, jnp.maximum(j - 2, 0)))],
        out_shape=[jax.ShapeDtypeStruct((head_rows, tn), F32),
                   jax.ShapeDtypeStruct((n_tiles * tail_rows, tn), F32),
                   jax.ShapeDtypeStruct((head_rows, tn), F32),
                   jax.ShapeDtypeStruct((n_tiles * tail_rows, tn), F32),
                   jax.ShapeDtypeStruct((m, n_q), F32)],
        scratch_shapes=[pltpu.VMEM((tm, d), BF16), pltpu.VMEM((tm, d), BF16)],
        compiler_params=_params("parallel", "arbitrary"),
        name="kv_q_proj",
    )(x, g_kv.reshape(1, d), g_q.reshape(1, d), w_kv, w_q, k_norm.reshape(1, HEAD),
      q_norm.reshape(1, HEAD))
    keep = (n_tiles - 1) * tail_rows
    return k, kt[keep:], v, vt[keep:], q


def _swiglu_up_kernel(x_ref, g_ref, wa_ref, wu_ref, o_ref, xn_ref):
    @pl.when(pl.program_id(1) == 0)
    def _():
        _normalise_rows(x_ref, g_ref, xn_ref, 0, x_ref.shape[0])

    xn = xn_ref[...]
    a = _dot(xn, wa_ref[...].astype(BF16))
    u = _dot(xn, wu_ref[...].astype(BF16))
    o_ref[...] = (a * _sigmoid(a) * u).astype(BF16)


def _swiglu_up(x, g, w_gu, layer, name):
    m, d = x.shape
    d_ff = w_gu.shape[-1] // 2
    tm, tn = _row_tile(m, FFN_ROW_TILE_CAP), _col_tile(d_ff, 512)
    nblk = d_ff // tn
    return pl.pallas_call(
        _swiglu_up_kernel,
        grid=(m // tm, nblk),
        in_specs=[
            pl.BlockSpec((tm, d), lambda i, j: (i, 0)),
            pl.BlockSpec((1, d), lambda i, j: (0, 0)),
            pl.BlockSpec((None, d, tn), lambda i, j: (layer, 0, j)),
            pl.BlockSpec((None, d, tn), lambda i, j: (layer, 0, j + nblk)),
        ],
        out_specs=pl.BlockSpec((tm, tn), lambda i, j: (i, j)),
        out_shape=jax.ShapeDtypeStruct((m, d_ff), BF16),
        scratch_shapes=[pltpu.VMEM((tm, d), BF16)],
        compiler_params=_params("parallel", "arbitrary"),
        name=name,
    )(x, g.reshape(1, d), w_gu, w_gu)


def _matmul_residual_kernel(a_ref, w_ref, r_ref, o_ref):
    o_ref[...] = r_ref[...] + _dot(a_ref[...], w_ref[...].astype(BF16))


def _matmul_residual_joined_kernel(a_ref, w_ref, r_ref, rt_ref, o_ref, *, tail_start):
    acc = _dot(a_ref[...], w_ref[...].astype(BF16))
    last = pl.program_id(0) == pl.num_programs(0) - 1

    @pl.when(jnp.logical_not(last))
    def _():
        o_ref[...] = r_ref[...] + acc

    @pl.when(last)
    def _():
        o_ref[:tail_start, :] = r_ref[:tail_start, :] + acc[:tail_start, :]
        o_ref[tail_start:, :] = rt_ref[...] + acc[tail_start:, :]


def _matmul_residual_split_kernel(a_ref, w_ref, r_ref, head_ref, tail_ref, *, tail_start):
    out = r_ref[...] + _dot(a_ref[...], w_ref[...].astype(BF16))
    head_ref[...] = out
    last = pl.program_id(0) == pl.num_programs(0) - 1

    @pl.when(last)
    def _():
        tail_ref[...] = out[tail_start:tail_start + tail_ref.shape[0], :]

    @pl.when(jnp.logical_not(last))
    def _():
        tail_ref[...] = jnp.zeros_like(tail_ref)


def _matmul_residual(a, w, layer, res, tn_cap, name, tm_cap=ROW_TILE_CAP, head_rows=None,
                     res_tail=None):
    m, k = a.shape
    n = w.shape[-1]
    tm, tn = _row_tile(m, tm_cap), _col_tile(n, tn_cap)
    if res_tail is not None:
        tail_start = res.shape[0] - (m // tm - 1) * tm
        assert tail_start > 0 and tail_start + res_tail.shape[0] == tm
        assert tail_start % SUBLANES == 0 and head_rows is None
        return pl.pallas_call(
            functools.partial(_matmul_residual_joined_kernel, tail_start=tail_start),
            grid=(m // tm, n // tn),
            in_specs=[
                pl.BlockSpec((tm, k), lambda i, j: (i, 0)),
                pl.BlockSpec((None, k, tn), lambda i, j: (layer, 0, j)),
                pl.BlockSpec((tm, tn), lambda i, j: (i, j)),
                pl.BlockSpec((res_tail.shape[0], tn), lambda i, j: (0, j)),
            ],
            out_specs=pl.BlockSpec((tm, tn), lambda i, j: (i, j)),
            out_shape=jax.ShapeDtypeStruct((m, n), F32),
            compiler_params=_params("parallel", "arbitrary"),
            name=name,
        )(a, w, res, res_tail)
    in_specs = [
        pl.BlockSpec((tm, k), lambda i, j: (i, 0)),
        pl.BlockSpec((None, k, tn), lambda i, j: (layer, 0, j)),
        pl.BlockSpec((tm, tn), lambda i, j: (i, j)),
    ]
    if head_rows is None:
        return pl.pallas_call(
            _matmul_residual_kernel,
            grid=(m // tm, n // tn),
            in_specs=in_specs,
            out_specs=pl.BlockSpec((tm, tn), lambda i, j: (i, j)),
            out_shape=jax.ShapeDtypeStruct((m, n), F32),
            compiler_params=_params("parallel", "arbitrary"),
            name=name,
        )(a, w, res)
    tail_rows = m - head_rows
    tail_start = head_rows - (m // tm - 1) * tm
    assert 0 < tail_start and tail_start + tail_rows == tm and tail_start % SUBLANES == 0
    head, tail = pl.pallas_call(
        functools.partial(_matmul_residual_split_kernel, tail_start=tail_start),
        grid=(m // tm, n // tn),
        in_specs=in_specs,
        out_specs=[pl.BlockSpec((tm, tn), lambda i, j: (i, j)),
                   pl.BlockSpec((tail_rows, tn), lambda i, j: (i, j))],
        out_shape=[jax.ShapeDtypeStruct((head_rows, n), F32),
                   jax.ShapeDtypeStruct((m // tm * tail_rows, n), F32)],
        compiler_params=_params("parallel", "arbitrary"),
        name=name,
    )(a, w, res)
    return head, tail[(m // tm - 1) * tail_rows:]


def _lower_bound(logits, layer):
    e = jnp.exp(logits - jnp.max(logits, axis=0, keepdims=True))
    return jnp.sum(e[:layer + 1], axis=0, keepdims=True) / jnp.sum(e, axis=0, keepdims=True)


def _gate_inputs(zq, zf, lb):
    f = lb + (1.0 - lb) * _sigmoid(zf)
    return zq * _sigmoid(zq), 1.0 - f, jnp.log(f)


def _gated_output(o, zg, gain):
    return _rms(o, gain) * (zg * _sigmoid(zg))


def _cumsum_rows(g, seg=None):
    n = g.shape[0]
    row = lax.broadcasted_iota(jnp.int32, g.shape, 0)
    pos = row if seg is None else _div_mod(row, seg)[1]
    span = n if seg is None else seg
    s = 1
    while s < span:
        g = g + jnp.where(pos >= s, pltpu.roll(g, s, 0), 0.0)
        s *= 2
    return g


def _gla_prompt_kernel(zq_ref, zf_ref, zi_ref, zg_ref, lbl_ref, gain_ref, o_ref, s_ref, st_ref,
                       *, layer, hp):
    t = pl.program_id(1)
    n_t = pl.num_programs(1) - 1
    c = GLA_CHUNK
    mid = (c - 1) // 2

    @pl.when(t == n_t)
    def _():
        o_ref[...] = jnp.zeros_like(o_ref)

    @pl.when(t < n_t)
    def _():
        _gla_prompt_step(zq_ref, zf_ref, zi_ref, zg_ref, lbl_ref, gain_ref, o_ref, s_ref, st_ref,
                         t, n_t, c, mid, layer, hp)


def _gla_prompt_step(zq_ref, zf_ref, zi_ref, zg_ref, lbl_ref, gain_ref, o_ref, s_ref, st_ref,
                     t, n_t, c, mid, layer, hp):

    @pl.when(t == 0)
    def _():
        st_ref[...] = jnp.zeros_like(st_ref)

    lb_all = _lower_bound(lbl_ref[...], layer)
    gain = gain_ref[...]
    r2 = lax.broadcasted_iota(jnp.int32, (c, c), 0)
    c2 = lax.broadcasted_iota(jnp.int32, (c, c), 1)
    st = [st_ref[j] for j in range(hp)]
    for i in range(o_ref.shape[0] // c):
        rows = slice(i * c, (i + 1) * c)
        for j in range(hp):
            cols = slice(j * HEAD, (j + 1) * HEAD)
            q, k, g = _gate_inputs(zq_ref[rows, cols], zf_ref[rows, cols], lb_all[:, cols])
            v16 = zi_ref[rows, cols].astype(BF16)
            cum = _cumsum_rows(g)
            ref = cum[mid:mid + 1, :]
            last = cum[c - 1:c, :]
            att = _dot_nt((q * jnp.exp(cum - ref)).astype(BF16),
                          (k * jnp.exp(ref - cum)).astype(BF16))
            att = jnp.where(r2 >= c2, att, 0.0)
            o = _dot(att.astype(BF16), v16) + _dot_nt((q * jnp.exp(cum)).astype(BF16),
                                                       st[j].astype(BF16))
            k_out = (k * jnp.exp(last - cum)).astype(BF16)
            st[j] = jnp.exp(last) * st[j] + _dot_tn(v16, k_out)
            o_ref[rows, cols] = _gated_output(o, zg_ref[rows, cols], gain).astype(BF16)
    for j in range(hp):
        st_ref[j] = st[j]

    @pl.when(t == n_t - 1)
    def _():
        for j in range(hp):
            s_ref[j] = st[j].T


def _gla_prompt(z, lb_logits, gain, layer, seq, tl=512, hp=GLA_HEADS_PER_STEP):
    heads = z.shape[1] // (4 * HEAD)
    hp = _unroll(heads, hp)
    tl = min(tl, seq)
    slots = lb_logits.shape[0]
    groups = heads // hp
    n_t = seq // tl
    assert z.shape[0] - seq <= tl
    zspec = lambda off: pl.BlockSpec((tl, hp * HEAD),
                                     lambda h, t: (jnp.minimum(t, n_t - 1), h + off * groups))
    return pl.pallas_call(
        functools.partial(_gla_prompt_kernel, layer=layer, hp=hp),
        grid=(groups, n_t + 1),
        in_specs=[zspec(0), zspec(1), zspec(2), zspec(3),
                  pl.BlockSpec((slots, hp * HEAD), lambda h, t: (0, h)),
                  pl.BlockSpec((1, HEAD), lambda h, t: (0, 0))],
        out_specs=[pl.BlockSpec((tl, hp * HEAD), lambda h, t: (t, h)),
                   pl.BlockSpec((hp, HEAD, HEAD), lambda h, t: (h, 0, 0))],
        out_shape=[jax.ShapeDtypeStruct((z.shape[0], heads * HEAD), BF16),
                   jax.ShapeDtypeStruct((heads, HEAD, HEAD), F32)],
        scratch_shapes=[pltpu.VMEM((hp, HEAD, HEAD), F32)],
        compiler_params=_params("parallel", "arbitrary"),
        name="gla_prompt",
    )(z, z, z, z, lb_logits, gain.reshape(1, HEAD))


def _gla_sample_kernel(zq_ref, zf_ref, zi_ref, zg_ref, lbl_ref, gain_ref, s0_ref, o_in_ref,
                       o_ref, s_ref, last_ref, kout_ref, qin_ref, v_ref, *, layer, steps):
    del o_in_ref
    rows = zq_ref.shape[0]
    mid = (steps - 1) // 2
    lb = _lower_bound(lbl_ref[...], layer)
    q, k, g = _gate_inputs(zq_ref[...], zf_ref[...], lb)
    v16 = zi_ref[...].astype(BF16)
    cum = _cumsum_rows(g, seg=steps)

    row = lax.broadcasted_iota(jnp.int32, cum.shape, 0)
    pos = _div_mod(row, steps)[1]

    def spread(src_pos):
        picked = jnp.where(pos == src_pos, cum, 0.0)
        out = picked
        for d in range(steps):
            if d != src_pos:
                out = out + pltpu.roll(picked, (d - src_pos) % rows, 0)
        return out

    ref = spread(mid)
    last = spread(steps - 1)
    r2 = lax.broadcasted_iota(jnp.int32, (rows, rows), 0)
    c2 = lax.broadcasted_iota(jnp.int32, (rows, rows), 1)
    att = _dot_nt((q * jnp.exp(cum - ref)).astype(BF16), (k * jnp.exp(ref - cum)).astype(BF16))
    att = jnp.where((r2 >= c2) & (_div_mod(r2, steps)[0] == _div_mod(c2, steps)[0]), att, 0.0)
    o_intra = _dot(att.astype(BF16), v16)
    qin_ref[...] = (q * jnp.exp(cum)).astype(BF16)
    v_ref[...] = v16
    last_ref[...] = last
    kout_ref[...] = k * jnp.exp(last - cum)
    n_seq = s0_ref.shape[0]

    def body(b, o_inter):
        lo = b * steps
        mine = (row >= lo) & (row < lo + steps)
        st = s0_ref[b]
        o_inter = o_inter + jnp.where(mine, _dot(qin_ref[...], st.astype(BF16)), 0.0)
        k_out = jnp.where(mine, kout_ref[...], 0.0).astype(BF16)
        decay_row = jnp.exp(last_ref[pl.ds(lo, 1), :])
        decay_col = jnp.broadcast_to(decay_row, (SUBLANES, HEAD)).T[:, :1]
        s_ref[b] = decay_col * st + _dot_tn(k_out, v_ref[...])
        return o_inter

    o_inter = lax.fori_loop(0, n_seq, body, jnp.zeros(cum.shape, F32),
                            unroll=_unroll(n_seq, SAMPLE_SEQ_UNROLL))
    o_ref[...] = _gated_output(o_intra + o_inter, zg_ref[...], gain_ref[...]).astype(BF16)


def _gla_sample(z, o_all, row0, n_seq, steps, s0, layer_s0, lb_logits, gain, layer):
    heads = z.shape[1] // (4 * HEAD)
    rows = n_seq * steps
    slots = lb_logits.shape[0]
    rb = row0 // rows
    zspec = lambda off: pl.BlockSpec((rows, HEAD), lambda h: (rb, h + off * heads))
    s_in = pl.BlockSpec((None, n_seq, None, HEAD, HEAD), lambda h: (layer_s0, 0, h, 0, 0))
    s_out = pl.BlockSpec((n_seq, None, HEAD, HEAD), lambda h: (0, h, 0, 0))
    return pl.pallas_call(
        functools.partial(_gla_sample_kernel, layer=layer, steps=steps),
        grid=(heads,),
        in_specs=[zspec(0), zspec(1), zspec(2), zspec(3),
                  pl.BlockSpec((slots, HEAD), lambda h: (0, h)),
                  pl.BlockSpec((1, HEAD), lambda h: (0, 0)),
                  s_in,
                  pl.BlockSpec(memory_space=pl.ANY)],
        out_specs=[pl.BlockSpec((rows, HEAD), lambda h: (rb, h)), s_out],
        out_shape=[jax.ShapeDtypeStruct(o_all.shape, o_all.dtype),
                   jax.ShapeDtypeStruct(s0.shape[1:], s0.dtype)],
        scratch_shapes=[pltpu.VMEM((rows, HEAD), F32), pltpu.VMEM((rows, HEAD), F32),
                        pltpu.VMEM((rows, HEAD), BF16), pltpu.VMEM((rows, HEAD), BF16)],
        input_output_aliases={7: 0},
        compiler_params=_params("parallel"),
        name="gla_sample",
    )(z, z, z, z, lb_logits, gain.reshape(1, HEAD), s0, o_all)


def _top_blocks_bits(gate_t, blk_f, n_valid):
    gate_t = jnp.where(blk_f < n_valid, gate_t, -jnp.inf)
    bits = jnp.zeros((1, gate_t.shape[1]), jnp.int32)
    for _ in range(MOBA_TOPK):
        mx = jnp.max(gate_t, axis=0, keepdims=True)
        idx = jnp.min(jnp.where(gate_t == mx, blk_f, float(LANES)), axis=0, keepdims=True)
        live = mx > -jnp.inf
        gate_t = jnp.where(blk_f == idx, -jnp.inf, gate_t)
        shift = jnp.minimum(idx, 31.0).astype(jnp.int32)
        bits = bits | jnp.where(live, jnp.left_shift(1, shift), 0)
    return bits


def _moba_prompt_kernel(q_ref, k_ref, v_ref, o_ref, *scratch, n_blocks, group, chunk):
    qi = pl.program_id(1)

    @pl.when(qi == n_blocks)
    def _():
        o_ref[...] = jnp.zeros_like(o_ref)

    @pl.when(qi < n_blocks)
    def _():
        _moba_prompt_block(q_ref, k_ref, v_ref, o_ref, *scratch, qi=qi, n_blocks=n_blocks,
                           group=group, chunk=chunk)


def _moba_prompt_block(q_ref, k_ref, v_ref, o_ref, kmean_ref, qb_ref, sel_ref, m_ref, l_ref, acc_ref,
                       k16_ref, v16_ref, *, qi, n_blocks, group, chunk):
    blk = MOBA_BLOCK
    rows = group * blk
    n_chunks = rows // chunk

    @pl.when(qi == 0)
    def _():
        kmean_ref[...] = jnp.zeros_like(kmean_ref)
        for n in range(n_blocks):
            kmean_ref[n:n + 1, :] = jnp.mean(k_ref[n * blk:(n + 1) * blk, :], axis=0, keepdims=True)
            k16_ref[n * blk:(n + 1) * blk, :] = k_ref[n * blk:(n + 1) * blk, :].astype(BF16)
            v16_ref[n * blk:(n + 1) * blk, :] = v_ref[n * blk:(n + 1) * blk, :].astype(BF16)

    blk_f = lax.broadcasted_iota(jnp.int32, (kmean_ref.shape[0], blk), 0).astype(F32)
    qi_f = qi.astype(F32)
    for g in range(group):
        qg = q_ref[:, g * HEAD:(g + 1) * HEAD]
        qb_ref[g * blk:(g + 1) * blk, :] = (qg * SCALE_LOG2E).astype(BF16)
        gate_t = _dot_nt(kmean_ref[...], qg, precision=lax.Precision.HIGHEST)
        bits = _top_blocks_bits(gate_t, blk_f, qi_f)
        per_row = jnp.broadcast_to(bits, (SUBLANES, blk)).T
        sel_ref[g * blk:(g + 1) * blk, :] = jnp.broadcast_to(per_row[:, :1], (blk, LANES))

    own = pl.ds(pl.multiple_of(qi * blk, blk), blk)
    kd = k16_ref[own, :]
    vd = v16_ref[own, :]
    own_chunk = min(PROMPT_OWN_CHUNK, blk)
    for c in range(rows // own_chunk):
        r = slice(c * own_chunk, (c + 1) * own_chunk)
        q_pos = lax.broadcasted_iota(jnp.int32, (own_chunk, blk), 0) + (c * own_chunk) % blk
        k_pos = lax.broadcasted_iota(jnp.int32, (own_chunk, blk), 1)
        s = jnp.where(k_pos <= q_pos, _dot_nt(qb_ref[r, :], kd), NEG)
        m = jnp.max(s, axis=1, keepdims=True)
        p = jnp.exp2(s - m)
        m_ref[r, :] = jnp.broadcast_to(m, (own_chunk, LANES))
        l_ref[r, :] = jnp.broadcast_to(jnp.sum(p, axis=1, keepdims=True), (own_chunk, LANES))
        acc_ref[r, :] = _dot(p.astype(BF16), vd)

    def attend(first_pair, n_pairs):
        blocks = []
        for t in range(n_pairs):
            ja = 2 * (first_pair + t)
            jb = jnp.minimum(ja + 1, qi - 1)
            bit_a = jnp.left_shift(jnp.int32(1), ja)
            bit_b = jnp.where(ja + 1 < qi, jnp.left_shift(jnp.int32(1), jb), 0)
            rows_a = pl.ds(pl.multiple_of(ja * blk, blk), blk)
            rows_b = pl.ds(pl.multiple_of(jb * blk, blk), blk)
            blocks.append((bit_a, bit_b,
                           k16_ref[rows_a, :], k16_ref[rows_b, :],
                           v16_ref[rows_a, :], v16_ref[rows_b, :]))
        work = [(t, c) for t in range(n_pairs) for c in range(n_chunks)]

        def scores(item):
            t, c = item
            qc = qb_ref[c * chunk:(c + 1) * chunk, :]
            return _dot_nt(qc, blocks[t][2]), _dot_nt(qc, blocks[t][3])

        pending = [scores(w) for w in work[:SCORE_LOOKAHEAD]]
        for n, (t, c) in enumerate(work):
            bit_a, bit_b, _, _, va, vb = blocks[t]
            r = slice(c * chunk, (c + 1) * chunk)
            sel = sel_ref[r, :]
            on_a = (sel & bit_a) != 0
            on_b = (sel & bit_b) != 0
            sa, sb = pending.pop(0)
            if n + SCORE_LOOKAHEAD < len(work):
                pending.append(scores(work[n + SCORE_LOOKAHEAD]))
            parts = [jnp.where(on_a, sa[:, :LANES], NEG), jnp.where(on_a, sa[:, LANES:], NEG),
                     jnp.where(on_b, sb[:, :LANES], NEG), jnp.where(on_b, sb[:, LANES:], NEG)]
            m_prev = m_ref[r, :]
            top = jnp.maximum(jnp.maximum(parts[0], parts[1]), jnp.maximum(parts[2], parts[3]))
            m_new = jnp.maximum(m_prev, jnp.max(top, axis=1, keepdims=True))
            ps = [jnp.exp2(x - m_new) for x in parts]
            alpha = jnp.exp2(m_prev - m_new)
            l_ref[r, :] = alpha * l_ref[r, :] + jnp.sum((ps[0] + ps[1]) + (ps[2] + ps[3]), axis=1,
                                                        keepdims=True)
            pa = jnp.concatenate(ps[:2], axis=1).astype(BF16)
            pb = jnp.concatenate(ps[2:], axis=1).astype(BF16)
            acc_ref[r, :] = alpha * acc_ref[r, :] + (_dot(pa, va) + _dot(pb, vb))
            m_ref[r, :] = m_new

    n_pairs = (qi + 1) // 2
    trips = n_pairs // PAIRS_PER_TRIP

    def body(i, carry):
        attend(i * PAIRS_PER_TRIP, PAIRS_PER_TRIP)
        return carry

    lax.fori_loop(0, trips, body, 0)
    for left in range(1, PAIRS_PER_TRIP):
        @pl.when(n_pairs - trips * PAIRS_PER_TRIP == left)
        def _():
            attend(trips * PAIRS_PER_TRIP, left)

    for g in range(group):
        r = slice(g * blk, (g + 1) * blk)
        o_ref[:, g * HEAD:(g + 1) * HEAD] = (acc_ref[r, :] / l_ref[r, :]).astype(BF16)


def _moba_prompt(q, k, v, seq, n_kv, chunk=PROMPT_ROW_CHUNK):
    heads = q.shape[1] // HEAD
    group = heads // n_kv
    n_blocks = seq // MOBA_BLOCK
    blk = MOBA_BLOCK
    rows = group * blk
    assert q.shape[0] - seq <= blk
    return pl.pallas_call(
        functools.partial(_moba_prompt_kernel, n_blocks=n_blocks, group=group, chunk=chunk),
        grid=(n_kv, n_blocks + 1),
        in_specs=[pl.BlockSpec((blk, group * HEAD), lambda h, i: (jnp.minimum(i, n_blocks - 1), h)),
                  pl.BlockSpec((seq, HEAD), lambda h, i: (0, h)),
                  pl.BlockSpec((seq, HEAD), lambda h, i: (0, h))],
        out_specs=pl.BlockSpec((blk, group * HEAD), lambda h, i: (i, h)),
        out_shape=jax.ShapeDtypeStruct((q.shape[0], heads * HEAD), BF16),
        scratch_shapes=[pltpu.VMEM((-(-n_blocks // SUBLANES) * SUBLANES, HEAD), F32),
                        pltpu.VMEM((rows, HEAD), BF16),
                        pltpu.VMEM((rows, LANES), jnp.int32),
                        pltpu.VMEM((rows, LANES), F32),
                        pltpu.VMEM((rows, LANES), F32),
                        pltpu.VMEM((rows, HEAD), F32),
                        pltpu.VMEM((seq, HEAD), BF16),
                        pltpu.VMEM((seq, HEAD), BF16)],
        compiler_params=_params("arbitrary", "arbitrary"),
        name="moba_prompt",
    )(q, k, v)


def _moba_sample_kernel(pt_ref, q_ref, kn_ref, vn_ref, *refs, n_kv, group, steps, pages_per_block,
                        blocks_per_step):
    del pt_ref
    n_pages = pages_per_block * blocks_per_step
    kp = refs[:n_pages]
    vp = refs[n_pages:2 * n_pages]
    o_ref, m_s, l_s, g_s, o_s, bias_s = refs[2 * n_pages:]
    j = pl.program_id(1)
    rows = q_ref.shape[1]
    rq = group * steps
    tok_per_vreg = SUBLANES // n_kv
    n_col = bias_s.shape[1]

    @pl.when((pl.program_id(0) == 0) & (j == 0))
    def _():
        row_head = _div_mod(lax.broadcasted_iota(jnp.int32, (rows, n_col), 0), rq)[0]
        col_head = _div_mod(lax.broadcasted_iota(jnp.int32, (rows, n_col), 1), n_kv)[1]
        bias_s[...] = jnp.where(row_head == col_head, 0.0, NEG)

    qf = q_ref[0]
    q16 = qf.astype(BF16)
    head64 = _div_mod(lax.broadcasted_iota(jnp.int32, (rows, HEAD), 0), rq)[0]
    kbs, vbs, scores = [], [], []
    for t in range(blocks_per_step):
        pages = slice(t * pages_per_block, (t + 1) * pages_per_block)
        kbs.append(jnp.concatenate([r[0] for r in kp[pages]], axis=0))
        vbs.append(jnp.concatenate([r[0] for r in vp[pages]], axis=0))
        scores.append(_dot_nt(q16, kbs[t].astype(BF16)))
    for t in range(blocks_per_step):
        kb = kbs[t]
        s = scores[t] * SCALE + bias_s[...]
        m = jnp.max(s, axis=1, keepdims=True)
        p = jnp.exp(s - m)

        parts = [kb[SUBLANES * i:SUBLANES * (i + 1), :] for i in range(n_col // SUBLANES)]
        while len(parts) > 1:
            parts = [a + b for a, b in zip(parts[::2], parts[1::2])]
        folded = parts[0]
        kmean = jnp.zeros((rows, HEAD), F32)
        for h in range(n_kv):
            total = folded[h:h + 1, :]
            for i in range(1, tok_per_vreg):
                total = total + folded[h + i * n_kv:h + i * n_kv + 1, :]
            kmean = jnp.where(head64 == h, total * (1.0 / MOBA_BLOCK), kmean)
        gate = jnp.sum(qf * kmean, axis=1, keepdims=True)

        n = j * blocks_per_step + t
        m_s[n] = jnp.broadcast_to(m, (rows, LANES))
        l_s[n] = jnp.broadcast_to(jnp.sum(p, axis=1, keepdims=True), (rows, LANES))
        g_s[n] = jnp.broadcast_to(gate, (rows, LANES))
        o_s[n] = _dot(p.astype(BF16), vbs[t].astype(BF16))

    @pl.when(j == pl.num_programs(1) - 1)
    def _():
        gates = g_s[...]
        n_io = lax.broadcasted_iota(jnp.int32, gates.shape, 0)
        sel = jnp.zeros(gates.shape, jnp.bool_)
        for _ in range(min(MOBA_TOPK, g_s.shape[0])):
            mx = jnp.max(gates, axis=0, keepdims=True)
            idx = jnp.min(jnp.where(gates == mx, n_io, g_s.shape[0]), axis=0, keepdims=True)
            pick = n_io == idx
            sel = sel | pick
            gates = jnp.where(pick, -jnp.inf, gates)

        row = lax.broadcasted_iota(jnp.int32, (rows, LANES), 0)
        lane = lax.broadcasted_iota(jnp.int32, (rows, LANES), 1)
        q_head, q_step = _div_mod(row, rq)[0], _div_mod(row, steps)[1]
        k_step, k_head = _div_mod(lane, n_kv)
        ok = (q_head == k_head) & (k_step <= q_step) & (k_step < steps)
        s_own = jnp.where(ok, _dot_nt(q16, kn_ref[0].astype(BF16)) * SCALE, NEG)
        m_own = jnp.max(s_own, axis=1, keepdims=True)
        p_own = jnp.exp(s_own - m_own)
        l_own = jnp.sum(p_own, axis=1, keepdims=True)
        o_own = _dot(p_own.astype(BF16), vn_ref[0].astype(BF16))

        m_all = m_s[...]
        m_top = jnp.maximum(m_own, jnp.max(jnp.where(sel, m_all, -jnp.inf), axis=0))
        w = jnp.where(sel, jnp.exp(jnp.minimum(m_all - m_top[None], 0.0)), 0.0)
        w_own = jnp.exp(m_own - m_top)
        den = w_own * l_own + jnp.sum(w * l_s[...], axis=0)
        num = w_own * o_own + jnp.sum(w * o_s[...], axis=0)
        o_ref[0] = num / den


def _moba_sample(q, k_new, v_new, cache_k, cache_v, page_table, n_kv, steps):
    n_seq, n_pages = page_table.shape
    n_phys, page, _, _ = cache_k.shape
    ppb = MOBA_BLOCK // page
    n_blocks = n_pages // ppb
    bps = _unroll(n_blocks, SAMPLE_BLOCKS_PER_STEP)
    rows = q.shape[1]
    group = rows // (n_kv * steps)
    assert SUBLANES % n_kv == 0 and steps * n_kv <= LANES
    ck = cache_k.reshape(n_phys, page * n_kv, HEAD)
    cv = cache_v.reshape(n_phys, page * n_kv, HEAD)
    pages_per_step = ppb * bps

    def page_spec(p):
        return pl.BlockSpec((1, page * n_kv, HEAD),
                            lambda b, j, pt: (pt[b, j * pages_per_step + p], 0, 0))

    per_seq = lambda shape: pl.BlockSpec(shape, lambda b, j, pt: (b, 0, 0))
    grid_spec = pltpu.PrefetchScalarGridSpec(
        num_scalar_prefetch=1,
        grid=(n_seq, n_blocks // bps),
        in_specs=[per_seq((1, rows, HEAD)), per_seq((1, LANES, HEAD)), per_seq((1, LANES, HEAD))]
                 + [page_spec(p) for p in range(pages_per_step)] * 2,
        out_specs=per_seq((1, rows, HEAD)),
        scratch_shapes=[pltpu.VMEM((n_blocks, rows, LANES), F32) for _ in range(4)]
                       + [pltpu.VMEM((rows, MOBA_BLOCK * n_kv), F32)],
    )
    return pl.pallas_call(
        functools.partial(_moba_sample_kernel, n_kv=n_kv, group=group, steps=steps,
                          pages_per_block=ppb, blocks_per_step=bps),
        grid_spec=grid_spec,
        out_shape=jax.ShapeDtypeStruct((n_seq, rows, HEAD), F32),
        compiler_params=_params("arbitrary", "arbitrary"),
        name="moba_sample",
    )(page_table, q, k_new, v_new, *([ck] * pages_per_step), *([cv] * pages_per_step))


def kernel(x_prompt, x_sample, state_hgrn, cache_k, cache_v, page_table, norm_mix_a, w_in_a, lb_logits,
           onorm_a, w_out_a, norm_kv, w_kv, k_norm, norm_mix_b, w_q_b, q_norm, w_o_b, norm_ffn,
           w_gate_up, w_down):
    batch, seq, d = x_prompt.shape
    n_seq, steps, _ = x_sample.shape
    n_a = w_in_a.shape[0]
    depth = norm_ffn.shape[0]
    heads = d // HEAD
    n_kv = cache_k.shape[2]
    group = heads // n_kv
    n_dec = n_seq * steps
    past = page_table.shape[1] * cache_k.shape[1]
    assert batch == 1 and seq % MOBA_BLOCK == 0 and seq // MOBA_BLOCK <= 32
    assert past % MOBA_BLOCK == 0 and MOBA_BLOCK % cache_k.shape[1] == 0
    assert steps <= LANES and seq % n_dec == 0 and seq % GLA_CHUNK == 0

    x_p, x_s = x_prompt.reshape(seq, d), x_sample.reshape(n_dec, d)
    h = None
    states_p, states_s = [], []
    k_p = v_p = k_d = v_d = None
    for layer in range(depth):
        if layer < n_a:
            if h is None:
                z = _norm_matmul(x_p, norm_mix_a[layer], w_in_a, layer, name="hgrn_in", tail=x_s)
            else:
                z = _norm_matmul(h, norm_mix_a[layer], w_in_a, layer, name="hgrn_in")
            o, s_p = _gla_prompt(z, lb_logits, onorm_a[layer], layer, seq)
            o, s_s = _gla_sample(z, o, seq, n_seq, steps, state_hgrn, layer, lb_logits,
                                 onorm_a[layer], layer)
            if h is None:
                h = _matmul_residual(o, w_out_a, layer, x_p, 512, "hgrn_out", res_tail=x_s)
            else:
                h = _matmul_residual(o, w_out_a, layer, h, 512, "hgrn_out")
            states_p.append(s_p.reshape(1, heads, HEAD, HEAD).astype(state_hgrn.dtype))
            states_s.append(s_s)
        else:
            if h is None:
                h = jnp.concatenate([x_p, x_s], axis=0)
            jb = layer - n_a
            if k_p is None:
                k_p, k_d, v_p, v_d, q = _kv_q_proj(h, norm_kv, w_kv, k_norm, norm_mix_b[jb], w_q_b,
                                                   jb, q_norm[jb], seq)
            else:
                q = _norm_matmul(h, norm_mix_b[jb], w_q_b, jb, head_gain=q_norm[jb],
                                 norm_blocks=heads, name="q_proj")
            att = _moba_prompt(q, k_p, v_p, seq, n_kv)
            q_s = q[seq:].reshape(n_seq, steps, n_kv, group, HEAD).transpose(0, 2, 3, 1, 4)
            pad = ((0, 0), (0, LANES - steps * n_kv), (0, 0))
            k_s = jnp.pad(k_d.reshape(n_seq, steps * n_kv, HEAD), pad)
            v_s = jnp.pad(v_d.reshape(n_seq, steps * n_kv, HEAD), pad)
            att_s = _moba_sample(q_s.reshape(n_seq, n_kv * group * steps, HEAD), k_s, v_s,
                                 cache_k, cache_v, page_table, n_kv, steps)
            att_s = att_s.reshape(n_seq, n_kv, group, steps, HEAD).transpose(0, 3, 1, 2, 4)
            att = lax.dynamic_update_slice(att, att_s.reshape(n_dec, d).astype(BF16), (seq, 0))
            h = _matmul_residual(att, w_o_b, jb, h, 512, "attn_out")
        hf = _swiglu_up(h, norm_ffn[layer], w_gate_up, layer, "ffn_up")
        if layer + 1 < depth:
            h = _matmul_residual(hf, w_down, layer, h, 256, "ffn_down", tm_cap=FFN_ROW_TILE_CAP)
        else:
            y_p, y_s = _matmul_residual(hf, w_down, layer, h, 256, "ffn_down",
                                        tm_cap=FFN_ROW_TILE_CAP, head_rows=seq)

    return (y_p.reshape(batch, seq, d),
            y_s.reshape(n_seq, steps, d),
            jnp.stack(states_p),
            jnp.stack(states_s),
            k_p.reshape(batch, seq, n_kv, HEAD),
            v_p.reshape(batch, seq, n_kv, HEAD),
            k_d.reshape(n_seq, steps, n_kv, HEAD),
            v_d.reshape(n_seq, steps, n_kv, HEAD))
```
